```python
import math
import jax
import jax.numpy as jnp
from jax import lax
import numpy as np

D_MODEL = 2048
BATCH = 1
SEQ = 16384
DEPTH = 4

N_META = 16
M_HEADS = 4
M_QK = D_MODEL // 16
M_V = D_MODEL // 8
M_CHUNK = 64
CONV_W = 4
A_HEADS = 8
A_DIM = D_MODEL // 16
Q_BLOCK = 128
TOPK_MAX = 256
IDX_HEADS = 8
IDX_DIM = 64
REL_BUCKETS = 32
REL_MAX_DIST = 128
N_GROUPS = 4
EXP_PER_GROUP = 8
N_EXPERTS = N_GROUPS * EXP_PER_GROUP
TOP_K_INNER = 2
D_FF = D_MODEL // 4
MOE_BLOCK = 128
ALPHA = (2 * DEPTH) ** 0.25
BETA = (8 * DEPTH) ** -0.25
LN_EPS = 1e-5
NEG = -1e30
BIG = 1e30

M_QK_W = M_HEADS * M_QK
M_V_W = M_HEADS * M_V
A_W = A_HEADS * A_DIM
IDX_Q_W = IDX_HEADS * IDX_DIM
IN_SPLITS = (M_QK_W, M_QK_W, M_V_W, M_V_W, M_HEADS, M_HEADS,
             A_W, A_W, A_W, IDX_Q_W, IDX_DIM, IDX_HEADS, D_MODEL, D_MODEL)
IN_W = sum(IN_SPLITS)

kernel_name = 'hybrid_mlstm_dsa_hiermoe_trunk'


def layer_norm(x, g, b):
    xf = x.astype(jnp.float32)
    mu = jnp.mean(xf, axis=-1, keepdims=True)
    var = jnp.mean(jnp.square(xf - mu), axis=-1, keepdims=True)
    y = (xf - mu) * lax.rsqrt(var + LN_EPS) * g.astype(jnp.float32) + b.astype(jnp.float32)
    return y.astype(x.dtype)


def split_cols(a, sizes):
    out, off = [], 0
    for s in sizes:
        out.append(a[..., off:off + s])
        off += s
    return out


def causal_dwconv(u, w, b):
    y = lax.conv_general_dilated(u, w[:, None, :].astype(u.dtype), window_strides=(1,),
                                 padding=[(CONV_W - 1, 0)],
                                 dimension_numbers=('NWC', 'WIO', 'NWC'),
                                 feature_group_count=u.shape[-1])
    return y + b.astype(u.dtype)


def t5_bucket(rel):
    rel = jnp.maximum(rel, 0)
    max_exact = REL_BUCKETS // 2
    rel_f = jnp.maximum(rel, 1).astype(jnp.float32)
    large = max_exact + (jnp.log(rel_f / max_exact) / math.log(REL_MAX_DIST / max_exact)
                         * (REL_BUCKETS - max_exact)).astype(jnp.int32)
    large = jnp.minimum(large, REL_BUCKETS - 1)
    return jnp.where(rel < max_exact, rel, large)


def mlstm_branch(mq, mk, mv, mo, mi, mf, b_ig, b_fg, mh_g):
    B, L, _ = mq.shape
    f32 = jnp.float32
    q = mq.reshape(B, L, M_HEADS, M_QK).astype(f32) * (M_QK ** -0.5)
    k = mk.reshape(B, L, M_HEADS, M_QK).astype(f32)
    v = mv.reshape(B, L, M_HEADS, M_V).astype(f32)
    ig = mi.astype(f32) + b_ig.astype(f32)
    lf = jax.nn.log_sigmoid(mf.astype(f32) + b_fg.astype(f32))
    p0 = (-N_META) % M_CHUNK
    p1 = (-(L + p0)) % M_CHUNK
    nc = (L + p0 + p1) // M_CHUNK

    def to_chunks(a, fill):
        a = jnp.pad(a, [(0, 0), (p0, p1)] + [(0, 0)] * (a.ndim - 2), constant_values=fill)
        a = a.reshape((B, nc, M_CHUNK) + a.shape[2:])
        return jnp.moveaxis(jnp.moveaxis(a, 1, 0), 3, 2)

    xs = (to_chunks(q, 0.0), to_chunks(k, 0.0), to_chunks(v, 0.0),
          to_chunks(ig, NEG), to_chunks(lf, 0.0))
    causal = jnp.tril(jnp.ones((M_CHUNK, M_CHUNK), dtype=bool))

    def step(carry, chunk):
        C, n, m = carry
        qc, kc, vc, igc, lfc = chunk
        b = jnp.cumsum(lfc, axis=-1)
        dmat = jnp.where(causal, b[..., :, None] - b[..., None, :] + igc[..., None, :], -jnp.inf)
        inter = b + m[..., None]
        m_t = jnp.maximum(jnp.max(dmat, axis=-1), inter)
        w_intra = jnp.exp(dmat - m_t[..., None]) * jnp.einsum('bhtd,bhsd->bhts', qc, kc)
        w_inter = jnp.exp(inter - m_t)
        num = (jnp.einsum('bhts,bhsv->bhtv', w_intra, vc)
               + w_inter[..., None] * jnp.einsum('bhtd,bhdv->bhtv', qc, C))
        den = jnp.sum(w_intra, axis=-1) + w_inter * jnp.einsum('bhtd,bhd->bht', qc, n)
        h = num / jnp.maximum(jnp.abs(den), jnp.exp(-m_t))[..., None]
        b_last = b[..., -1]
        dec = b_last[..., None] - b + igc
        m_new = jnp.maximum(b_last + m, jnp.max(dec, axis=-1))
        w_src = jnp.exp(dec - m_new[..., None])
        carry_scale = jnp.exp(b_last + m - m_new)
        C = carry_scale[..., None, None] * C + jnp.einsum('bhs,bhsd,bhsv->bhdv', w_src, kc, vc)
        n = carry_scale[..., None] * n + jnp.einsum('bhs,bhsd->bhd', w_src, kc)
        return (C, n, m_new), h

    init = (jnp.zeros((B, M_HEADS, M_QK, M_V), f32),
            jnp.zeros((B, M_HEADS, M_QK), f32),
            jnp.zeros((B, M_HEADS), f32))
    _, h = lax.scan(step, init, xs)
    h = jnp.moveaxis(jnp.moveaxis(h, 0, 1), 2, 3).reshape(B, nc * M_CHUNK, M_HEADS, M_V)[:, p0:p0 + L]
    mu = jnp.mean(h, axis=-1, keepdims=True)
    var = jnp.mean(jnp.square(h - mu), axis=-1, keepdims=True)
    h = (h - mu) * lax.rsqrt(var + LN_EPS) * mh_g.reshape(M_HEADS, M_V).astype(f32)
    return (jax.nn.sigmoid(mo.astype(f32)) * h.reshape(B, L, M_V_W)).astype(mq.dtype)


def sparse_attention(q, k, v, qi, ki, wi, rel_bias, topk):
    B, L = q.shape[0], q.shape[1]
    nb = -(-L // Q_BLOCK)
    Lp = nb * Q_BLOCK

    def pad_q(a):
        return jnp.pad(a, [(0, 0), (0, Lp - L)] + [(0, 0)] * (a.ndim - 2))

    qp, qip, wip = pad_q(q), pad_q(qi), pad_q(wi)
    key_pos = jnp.arange(L, dtype=jnp.int32)
    scale = A_DIM ** -0.5

    def block(bidx):
        start = bidx * Q_BLOCK
        qb = lax.dynamic_slice_in_dim(qp, start, Q_BLOCK, axis=1)
        qib = lax.dynamic_slice_in_dim(qip, start, Q_BLOCK, axis=1)
        wib = lax.dynamic_slice_in_dim(wip, start, Q_BLOCK, axis=1)
        q_pos = start + jnp.arange(Q_BLOCK, dtype=jnp.int32)
        s_h = jax.nn.relu(jnp.einsum('bqhd,bsd->bqhs', qib, ki))
        s_idx = jnp.einsum('bqh,bqhs->bqs', wib, s_h).astype(jnp.float32)
        visible = key_pos[None, :] <= q_pos[:, None]
        s_idx = jnp.where(visible, jnp.where(key_pos[None, :] < N_META, BIG, s_idx), NEG)
        vals, sel = lax.top_k(s_idx, topk)
        valid = vals > 0.5 * NEG
        k_sel = jax.vmap(lambda kb, ib: kb[ib])(k, sel)
        v_sel = jax.vmap(lambda vb, ib: vb[ib])(v, sel)
        logits = jnp.einsum('bqhd,bqkhd->bqhk', qb, k_sel).astype(jnp.float32) * scale
        bias = rel_bias[t5_bucket(q_pos[None, :, None] - sel)].astype(jnp.float32)
        logits = logits + jnp.moveaxis(bias, -1, 2)
        logits = jnp.where(valid[:, :, None, :], logits, -jnp.inf)
        p = jax.nn.softmax(logits, axis=-1)
        return jnp.einsum('bqhk,bqkhd->bqhd', p.astype(v.dtype), v_sel)

    outs = lax.map(block, jnp.arange(nb, dtype=jnp.int32))
    outs = jnp.moveaxis(outs, 0, 1).reshape(B, Lp, A_HEADS, A_DIM)
    return outs[:, :L]


def mixer(x, w_in, conv_w, conv_b, b_ig, b_fg, mh_g, w_pm, w_pa, w_out, rel_bias, topk):
    B, L, _ = x.shape
    (mq, mk, mv, mo, mi, mf, aq, ak, av, iq, ik, iw, gm, ga) = split_cols(x @ w_in, IN_SPLITS)
    qk = jax.nn.silu(causal_dwconv(jnp.concatenate([mq, mk], axis=-1), conv_w, conv_b))
    mq, mk = qk[..., :M_QK_W], qk[..., M_QK_W:]
    y_m = mlstm_branch(mq, mk, mv, mo, mi, mf, b_ig, b_fg, mh_g)
    y_a = sparse_attention(aq.reshape(B, L, A_HEADS, A_DIM),
                           ak.reshape(B, L, A_HEADS, A_DIM),
                           av.reshape(B, L, A_HEADS, A_DIM),
                           iq.reshape(B, L, IDX_HEADS, IDX_DIM) * (IDX_DIM ** -0.5),
                           ik, iw * (IDX_HEADS ** -0.5), rel_bias, topk)
    merged = (jax.nn.sigmoid(gm) * (y_m @ w_pm)
              + jax.nn.sigmoid(ga) * (y_a.reshape(B, L, A_W) @ w_pa))
    return merged @ w_out


def hier_moe(x2, w_group, b_group, w_router, b_router, w_gate_up, w_down):
    N = x2.shape[0]
    f32 = jnp.float32
    g_logits = (x2 @ w_group).astype(f32) + b_group.astype(f32)
    g_prob = jax.nn.softmax(g_logits, axis=-1)
    g_sel = jnp.argmax(g_logits, axis=-1).astype(jnp.int32)
    e_logits = ((x2 @ w_router).astype(f32) + b_router.astype(f32)).reshape(N, N_GROUPS, EXP_PER_GROUP)
    e_in_group = jnp.take_along_axis(e_logits, g_sel[:, None, None], axis=1)[:, 0]
    top_v, top_i = lax.top_k(e_in_group, TOP_K_INNER)
    gate = jax.nn.softmax(top_v, axis=-1) * jnp.take_along_axis(g_prob, g_sel[:, None], axis=1)
    expert = g_sel[:, None] * EXP_PER_GROUP + top_i
    A = N * TOP_K_INNER
    e_flat = expert.reshape(A)
    tok_flat = jnp.repeat(jnp.arange(N, dtype=jnp.int32), TOP_K_INNER)
    w_flat = gate.reshape(A)
    order = jnp.argsort(e_flat)
    e_s, tok_s, w_s = e_flat[order], tok_flat[order], w_flat[order]
    counts = jnp.zeros((N_EXPERTS,), jnp.int32).at[e_flat].add(1)
    start = jnp.cumsum(counts) - counts
    padded = (counts + MOE_BLOCK - 1) // MOE_BLOCK * MOE_BLOCK
    pend = jnp.cumsum(padded)
    pstart = pend - padded
    dest = pstart[e_s] + (jnp.arange(A, dtype=jnp.int32) - start[e_s])
    n_blocks = -(-A // MOE_BLOCK) + N_EXPERTS
    R = n_blocks * MOE_BLOCK
    row_tok = jnp.zeros((R,), jnp.int32).at[dest].set(tok_s)
    row_w = jnp.zeros((R,), f32).at[dest].set(w_s)
    blk_e = jnp.minimum(jnp.searchsorted(pend, jnp.arange(n_blocks, dtype=jnp.int32) * MOE_BLOCK,
                                         side='right'), N_EXPERTS - 1).astype(jnp.int32)

    def run_block(args):
        toks, ws, e = args
        xb = x2[toks]
        gu = xb @ w_gate_up[e]
        hb = jax.nn.silu(gu[:, :D_FF]) * gu[:, D_FF:]
        return ((hb @ w_down[e]) * ws[:, None]).astype(x2.dtype)

    ys = lax.map(run_block, (row_tok.reshape(n_blocks, MOE_BLOCK),
                             row_w.reshape(n_blocks, MOE_BLOCK), blk_e))
    return jax.ops.segment_sum(ys.reshape(R, x2.shape[1]), row_tok, num_segments=N)


def setup_inputs(seed: int = 0) -> dict:
    key = jax.random.key(seed)
    ks = jax.random.split(key, 26)
    D, NL = D_MODEL, DEPTH

    def nrm(k, shape, s):
        return jax.random.normal(k, shape, jnp.float32) * s

    return {
        'x': nrm(ks[0], (BATCH, SEQ, D), 1.0),
        'meta_tokens': nrm(ks[1], (N_META, D), 1.0),
        'ln_emb_g': 1.0 + nrm(ks[2], (D,), 0.02),
        'ln_emb_b': nrm(ks[3], (D,), 0.02),
        'rel_bias': nrm(ks[4], (REL_BUCKETS, A_HEADS), 0.5),
        'w_in': nrm(ks[5], (NL, D, IN_W), D ** -0.5),
        'conv_w': nrm(ks[6], (NL, CONV_W, 2 * M_QK_W), CONV_W ** -0.5),
        'conv_b': nrm(ks[7], (NL, 2 * M_QK_W), 0.02),
        'b_igate': nrm(ks[8], (NL, M_HEADS), 0.1),
        'b_fgate': 3.0 + 3.0 * jax.random.uniform(ks[9], (NL, M_HEADS), jnp.float32),
        'mh_norm_g': 1.0 + nrm(ks[10], (NL, M_V_W), 0.02),
        'w_proj_m': nrm(ks[11], (NL, M_V_W, D), M_V_W ** -0.5),
        'w_proj_a': nrm(ks[12], (NL, A_W, D), A_W ** -0.5),
        'w_out': nrm(ks[13], (NL, D, D), BETA * D ** -0.5),
        'ln1_g': 1.0 + nrm(ks[14], (NL, D), 0.02),
        'ln1_b': nrm(ks[15], (NL, D), 0.02),
        'w_group': nrm(ks[16], (NL, D, N_GROUPS), D ** -0.5),
        'b_group': nrm(ks[17], (NL, N_GROUPS), 0.01),
        'w_router': nrm(ks[18], (NL, D, N_EXPERTS), D ** -0.5),
        'b_router': nrm(ks[19], (NL, N_EXPERTS), 0.01),
        'w_gate_up': nrm(ks[20], (NL, N_EXPERTS, D, 2 * D_FF), D ** -0.5),
        'w_down': nrm(ks[21], (NL, N_EXPERTS, D_FF, D), BETA * D_FF ** -0.5),
        'ln2_g': 1.0 + nrm(ks[22], (NL, D), 0.02),
        'ln2_b': nrm(ks[23], (NL, D), 0.02),
    }


def reference(x, meta_tokens, ln_emb_g, ln_emb_b, rel_bias, w_in, conv_w, conv_b, b_igate,
              b_fgate, mh_norm_g, w_proj_m, w_proj_a, w_out, ln1_g, ln1_b, w_group, b_group,
              w_router, b_router, w_gate_up, w_down, ln2_g, ln2_b):
    B = x.shape[0]
    meta = jnp.broadcast_to(meta_tokens[None].astype(x.dtype), (B, N_META, D_MODEL))
    h = layer_norm(jnp.concatenate([meta, x], axis=1), ln_emb_g, ln_emb_b)
    L = h.shape[1]
    topk = min(TOPK_MAX, L // 4)
    for l in range(DEPTH):
        mix = mixer(h, w_in[l], conv_w[l], conv_b[l], b_igate[l], b_fgate[l], mh_norm_g[l],
                    w_proj_m[l], w_proj_a[l], w_out[l], rel_bias, topk)
        h = layer_norm(ALPHA * h + mix, ln1_g[l], ln1_b[l])
        ffn = hier_moe(h.reshape(B * L, D_MODEL), w_group[l], b_group[l], w_router[l],
                       b_router[l], w_gate_up[l], w_down[l]).reshape(B, L, D_MODEL)
        h = layer_norm(ALPHA * h + ffn, ln2_g[l], ln2_b[l])
    return h[:, N_META:]
```

```python
import functools
import math

import jax
import jax.numpy as jnp
import numpy as np
from jax import lax
from jax.experimental import pallas as pl
from jax.experimental.pallas import tpu as pltpu

D_MODEL = 2048
N_META = 16
M_HEADS = 4
M_QK = 128
M_V = 256
CONV_W = 4
A_HEADS = 8
A_DIM = 128
TOPK_MAX = 256
IDX_HEADS = 8
IDX_DIM = 64
REL_BUCKETS = 32
REL_MAX_DIST = 128
N_GROUPS = 4
EXP_PER_GROUP = 8
N_EXPERTS = 32
D_FF = 512
MOE_BLOCK = 128
LN_EPS = 1e-5
NEG = -1e30
BIG = 1e30

M_QK_W = M_HEADS * M_QK
M_V_W = M_HEADS * M_V
A_W = A_HEADS * A_DIM
IDX_Q_W = IDX_HEADS * IDX_DIM

COL_GM = 0
COL_GA = 2048
COL_MV = 4096
COL_MO = 5120
COL_AQ = 6144
COL_AK = 7168
COL_AV = 8192
COL_MQK = 9216
COL_IQ = 10240
BIG_W = 10752
SM_W = 128

ROW_TILE = 256
F32 = jnp.float32
BF16 = jnp.bfloat16
VMEM_LIMIT = 56 * 1024 * 1024
HI = lax.Precision.HIGHEST


def _cparams(sem):
    return pltpu.CompilerParams(dimension_semantics=sem, vmem_limit_bytes=VMEM_LIMIT)


def _pick(n, cands):
    for c in cands:
        if n % c == 0:
            return c
    raise ValueError(f"no tile for {n}")


def _ln(x, g, b):
    mu = jnp.mean(x, axis=-1, keepdims=True)
    xc = x - mu
    var = jnp.mean(xc * xc, axis=-1, keepdims=True)
    return xc * lax.rsqrt(var + LN_EPS) * g + b


def _sigmoid(x):
    return 1.0 / (1.0 + jnp.exp(-x))


def _embed_ln_kernel(x_ref, g_ref, b_ref, h_ref, hb_ref):
    y = _ln(x_ref[...], g_ref[...], b_ref[...])
    h_ref[...] = y
    hb_ref[...] = y.astype(BF16)


def embed_ln(xp, g, b):
    lp = xp.shape[0]
    tm = ROW_TILE
    return pl.pallas_call(
        _embed_ln_kernel,
        grid=(lp // tm,),
        in_specs=[pl.BlockSpec((tm, D_MODEL), lambda i: (i, 0)),
                  pl.BlockSpec((1, D_MODEL), lambda i: (0, 0)),
                  pl.BlockSpec((1, D_MODEL), lambda i: (0, 0))],
        out_specs=[pl.BlockSpec((tm, D_MODEL), lambda i: (i, 0)),
                   pl.BlockSpec((tm, D_MODEL), lambda i: (i, 0))],
        out_shape=[jax.ShapeDtypeStruct((lp, D_MODEL), F32),
                   jax.ShapeDtypeStruct((lp, D_MODEL), BF16)],
        compiler_params=_cparams(("parallel",)),
        name="embed_ln",
    )(xp, g.reshape(1, -1), b.reshape(1, -1))


def _mm_kernel(a_ref, b_ref, o_ref):
    o_ref[...] = jnp.dot(a_ref[...], b_ref[...], preferred_element_type=F32).astype(o_ref.dtype)


def in_proj_big(hb, w):
    lp = hb.shape[0]
    tm = _pick(lp, (1280, 768, 512, 256))
    tn = 1536
    return pl.pallas_call(
        _mm_kernel,
        grid=(BIG_W // tn, lp // tm),
        in_specs=[pl.BlockSpec((tm, D_MODEL), lambda j, i: (i, 0)),
                  pl.BlockSpec((D_MODEL, tn), lambda j, i: (0, j))],
        out_specs=pl.BlockSpec((tm, tn), lambda j, i: (i, j)),
        out_shape=jax.ShapeDtypeStruct((lp, BIG_W), BF16),
        compiler_params=_cparams(("parallel", "parallel")),
        name="in_proj_big",
    )(hb, w)


def _mm_hi_kernel(a_ref, b_ref, o_ref):
    o_ref[...] = jnp.dot(a_ref[...], b_ref[...], preferred_element_type=F32, precision=HI)


def in_proj_small(h, w):
    lp = h.shape[0]
    tm = ROW_TILE
    return pl.pallas_call(
        _mm_hi_kernel,
        grid=(lp // tm,),
        in_specs=[pl.BlockSpec((tm, D_MODEL), lambda i: (i, 0)),
                  pl.BlockSpec((D_MODEL, SM_W), lambda i: (0, 0))],
        out_specs=pl.BlockSpec((tm, SM_W), lambda i: (i, 0)),
        out_shape=jax.ShapeDtypeStruct((lp, SM_W), F32),
        compiler_params=_cparams(("parallel",)),
        name="in_proj_small",
    )(h, w)


def _log_sigmoid(x):
    return jnp.minimum(x, 0.0) - jnp.log1p(jnp.exp(-jnp.abs(x)))


def _mlstm_kernel(qk_ref, v_ref, o_ref, sm_ref, smt_ref, cw_ref, cb_ref, gb_ref, gbt_ref, mhg_ref,
                  y_ref, ubuf, c_ref, n_ref, m_ref):
    t = ROW_TILE
    step = pl.program_id(0)

    @pl.when(step == 0)
    def _():
        ubuf[0:8, :] = jnp.zeros((8, 2 * M_QK_W), F32)
        c_ref[...] = jnp.zeros_like(c_ref)
        n_ref[...] = jnp.zeros_like(n_ref)
        m_ref[...] = jnp.zeros_like(m_ref)

    ubuf[8:8 + t, :] = qk_ref[...].astype(F32)
    conv = cb_ref[...]
    for j in range(CONV_W):
        d = CONV_W - 1 - j
        conv = conv + cw_ref[j:j + 1, :] * ubuf[8 - d:8 - d + t, :]
    ubuf[0:8, :] = ubuf[t:t + 8, :]
    qk = conv * _sigmoid(conv)

    gcol = sm_ref[...] + gb_ref[...]
    grow = smt_ref[...] + gbt_ref[...]
    lane = lax.broadcasted_iota(jnp.int32, (t, SM_W), 1)
    gcol = jnp.where(lane < M_HEADS, gcol, _log_sigmoid(gcol))
    srow = lax.broadcasted_iota(jnp.int32, (8, t), 0)
    grow = jnp.where(srow < M_HEADS, grow, _log_sigmoid(grow))
    ri = lax.broadcasted_iota(jnp.int32, (t, t), 0)
    ci = lax.broadcasted_iota(jnp.int32, (t, t), 1)
    causal = ci <= ri
    tri = causal.astype(F32)
    bcol = jnp.dot(tri, gcol, preferred_element_type=F32, precision=HI)
    brow = lax.dot_general(grow, tri, (((1,), (1,)), ((), ())), preferred_element_type=F32,
                           precision=HI)

    v_all = v_ref[...]
    sig_o = _sigmoid(o_ref[...].astype(F32))
    for h in range(M_HEADS):
        q = qk[:, h * M_QK:(h + 1) * M_QK] * (M_QK ** -0.5)
        k = qk[:, M_QK_W + h * M_QK:M_QK_W + (h + 1) * M_QK]
        v = v_all[:, h * M_V:(h + 1) * M_V]
        qb = q.astype(BF16)
        kb = k.astype(BF16)
        b_c = bcol[:, M_HEADS + h:M_HEADS + h + 1]
        ig_c = gcol[:, h:h + 1]
        b_r = brow[M_HEADS + h:M_HEADS + h + 1, :]
        ig_r = grow[h:h + 1, :]
        m_prev = m_ref[h, 0:1, 0:1]
        dmat = jnp.where(causal, b_c + (ig_r - b_r), NEG)
        inter = b_c + m_prev
        m_t = jnp.maximum(jnp.max(dmat, axis=-1, keepdims=True), inter)
        s = lax.dot_general(qb, kb, (((1,), (1,)), ((), ())), preferred_element_type=F32)
        w_intra = jnp.exp(dmat - m_t) * s
        w_inter = jnp.exp(inter - m_t)
        c_old = c_ref[h]
        n_old = n_ref[h, 0:1, :]
        num = (jnp.dot(w_intra.astype(BF16), v, preferred_element_type=F32)
               + w_inter * jnp.dot(qb, c_old.astype(BF16), preferred_element_type=F32))
        den = (jnp.sum(w_intra, axis=-1, keepdims=True)
               + w_inter * jnp.sum(q * n_old, axis=-1, keepdims=True))
        hh = num / jnp.maximum(jnp.abs(den), jnp.exp(-m_t))
        mu = jnp.mean(hh, axis=-1, keepdims=True)
        hc = hh - mu
        var = jnp.mean(hc * hc, axis=-1, keepdims=True)
        hn = hc * lax.rsqrt(var + LN_EPS) * mhg_ref[:, h * M_V:(h + 1) * M_V]
        y_ref[:, h * M_V:(h + 1) * M_V] = (sig_o[:, h * M_V:(h + 1) * M_V] * hn).astype(y_ref.dtype)
        b_last = b_c[t - 1:t, :]
        dec = b_last - b_c + ig_c
        m_new = jnp.maximum(b_last + m_prev, jnp.max(dec, axis=0, keepdims=True))
        w_src = jnp.exp(dec - m_new)
        cs = jnp.exp(b_last + m_prev - m_new)
        wv = (w_src * v.astype(F32)).astype(BF16)
        c_ref[h] = cs * c_old + jnp.dot(k.T.astype(BF16), wv, preferred_element_type=F32)
        n_new = cs * n_old + jnp.sum(w_src * k, axis=0, keepdims=True)
        n_ref[h] = jnp.broadcast_to(n_new, (8, M_QK))
        m_ref[h] = jnp.broadcast_to(m_new, (8, 128))


def mlstm(proj, small, small_t, conv_w, conv_b, b_ig, b_fg, mh_g):
    lp = proj.shape[0]
    t = ROW_TILE
    gb = jnp.zeros((1, SM_W), F32).at[0, 0:M_HEADS].set(b_ig).at[0, M_HEADS:2 * M_HEADS].set(b_fg)
    gbt = gb[0, 0:8].reshape(8, 1)
    cq = COL_MQK // (2 * M_QK_W)
    cv = COL_MV // M_V_W
    co = COL_MO // M_V_W
    return pl.pallas_call(
        _mlstm_kernel,
        grid=(lp // t,),
        in_specs=[pl.BlockSpec((t, 2 * M_QK_W), lambda i: (i, cq)),
                  pl.BlockSpec((t, M_V_W), lambda i: (i, cv)),
                  pl.BlockSpec((t, M_V_W), lambda i: (i, co)),
                  pl.BlockSpec((t, SM_W), lambda i: (i, 0)),
                  pl.BlockSpec((8, t), lambda i: (0, i)),
                  pl.BlockSpec((CONV_W, 2 * M_QK_W), lambda i: (0, 0)),
                  pl.BlockSpec((1, 2 * M_QK_W), lambda i: (0, 0)),
                  pl.BlockSpec((1, SM_W), lambda i: (0, 0)),
                  pl.BlockSpec((8, 1), lambda i: (0, 0)),
                  pl.BlockSpec((1, M_V_W), lambda i: (0, 0))],
        out_specs=pl.BlockSpec((t, M_V_W), lambda i: (i, 0)),
        out_shape=jax.ShapeDtypeStruct((lp, M_V_W), BF16),
        scratch_shapes=[pltpu.VMEM((t + 8, 2 * M_QK_W), F32),
                        pltpu.VMEM((M_HEADS, M_QK, M_V), F32),
                        pltpu.VMEM((M_HEADS, 8, M_QK), F32),
                        pltpu.VMEM((M_HEADS, 8, 128), F32)],
        compiler_params=_cparams(("arbitrary",)),
        name="mlstm",
    )(proj, proj, proj, small, small_t, conv_w, conv_b.reshape(1, -1), gb, gbt, mh_g.reshape(1, -1))


def _f2key(f):
    b = pltpu.bitcast(f, jnp.int32)
    return jnp.where(b >= 0, b, b ^ jnp.int32(0x7FFFFFFF))


def _key2f(k):
    return pltpu.bitcast(jnp.where(k >= 0, k, k ^ jnp.int32(0x7FFFFFFF)), F32)


IDX_RG = 64
IDX_MAXIT = 96


def _index_mask_kernel(iq_ref, sm_ref, kit_ref, mask_ref, s_ref, *, topk):
    t = ROW_TILE
    i = pl.program_id(0)
    nk = i + 1
    nkt = pl.num_programs(0)
    q_pos = i * t + lax.broadcasted_iota(jnp.int32, (t, 1), 0)
    iq = iq_ref[...]
    wts = sm_ref[:, 8:8 + IDX_HEADS]

    def score_tile(kj, carry):
        vmin, vmax = carry
        kt = kit_ref[:, pl.ds(pl.multiple_of(kj * t, t), t)]
        acc = jnp.zeros((t, t), F32)
        for h in range(IDX_HEADS):
            x = jnp.dot(iq[:, h * IDX_DIM:(h + 1) * IDX_DIM], kt, preferred_element_type=F32)
            acc = acc + wts[:, h:h + 1] * jnp.maximum(x, 0.0)
        key_pos = kj * t + lax.broadcasted_iota(jnp.int32, (1, t), 1)
        visible = key_pos <= q_pos
        ordinary = jnp.logical_and(visible, key_pos >= N_META)
        s = jnp.where(visible, jnp.where(key_pos < N_META, BIG, acc), NEG)
        s_ref[:, pl.ds(pl.multiple_of(kj * t, t), t)] = s
        vmin = jnp.minimum(vmin, jnp.min(jnp.where(ordinary, acc, BIG), axis=-1, keepdims=True))
        vmax = jnp.maximum(vmax, jnp.max(jnp.where(ordinary, acc, NEG), axis=-1, keepdims=True))
        return vmin, vmax

    vmin, vmax = lax.fori_loop(0, nk, score_tile,
                               (jnp.full((t, 1), BIG, F32), jnp.full((t, 1), NEG, F32)))

    def count_ge(x, strict=False):
        outs = []
        for g in range(t // IDX_RG):
            xg = jnp.broadcast_to(x[g * IDX_RG:(g + 1) * IDX_RG, :], (IDX_RG, 128))

            def body(kj, acc):
                for c in range(t // 128):
                    tile = s_ref[g * IDX_RG:(g + 1) * IDX_RG, pl.ds(pl.multiple_of(kj * t + c * 128, 128), 128)]
                    hit = (tile > xg) if strict else (tile >= xg)
                    acc = acc + jnp.where(hit, 1.0, 0.0)
                return acc

            acc = lax.fori_loop(0, nk, body, jnp.zeros((IDX_RG, 128), F32))
            outs.append(jnp.sum(acc, axis=-1, keepdims=True))
        return jnp.concatenate(outs, axis=0)

    kf = float(topk)
    n_vis = (q_pos + 1).astype(F32)
    n_meta = jnp.minimum(n_vis, float(N_META))
    few = n_vis <= kf
    lo_k0 = _f2key(vmin)
    hi_k0 = _f2key(vmax) + 1
    done0 = jnp.logical_or(few, n_vis - n_meta < 0.5)
    thr0 = jnp.full((t, 1), NEG, F32)

    def cond(st):
        it, lo_k, hi_k, clo, chi, thr, done_i, cthr = st
        return jnp.logical_and(it < IDX_MAXIT, jnp.min(done_i) < 1)

    def step(st):
        it, lo_k, hi_k, clo, chi, thr, done_i, cthr = st
        done = done_i > 0
        lo_v = _key2f(lo_k)
        hi_v = _key2f(hi_k)
        frac = jnp.log(clo / (kf + 0.5)) / jnp.log(clo / jnp.maximum(chi, 0.5))
        frac = jnp.clip(frac, 0.0, 1.0)
        pk_i = _f2key(lo_v + (hi_v - lo_v) * frac)
        pk_b = (lo_k >> 1) + (hi_k >> 1) + (lo_k & hi_k & 1)
        pk = jnp.where(it % 2 == 0, pk_i, pk_b)
        pk = jnp.minimum(jnp.maximum(pk, lo_k + 1), hi_k - 1)
        pk = jnp.where(done, lo_k, pk)
        x = _key2f(pk)
        c = count_ge(x)
        ge = c >= kf
        act = jnp.logical_not(done)
        lo_k = jnp.where(jnp.logical_and(act, ge), pk, lo_k)
        clo = jnp.where(jnp.logical_and(act, ge), c, clo)
        hi_k = jnp.where(jnp.logical_and(act, jnp.logical_not(ge)), pk, hi_k)
        chi = jnp.where(jnp.logical_and(act, jnp.logical_not(ge)), c, chi)
        exact = jnp.logical_and(act, c == kf)
        collapsed = jnp.logical_and(act, hi_k <= lo_k + 1)
        fin = jnp.logical_or(exact, collapsed)
        thr = jnp.where(exact, x, jnp.where(collapsed, _key2f(lo_k), thr))
        cthr = jnp.where(exact, c, jnp.where(collapsed, clo, cthr))
        return it + 1, lo_k, hi_k, clo, chi, thr, jnp.where(jnp.logical_or(done, fin), 1, 0), cthr

    col0 = jnp.logical_and(jnp.logical_not(done0), hi_k0 <= lo_k0 + 1)
    thr0 = jnp.where(col0, vmin, thr0)
    st = (jnp.int32(0), lo_k0, hi_k0, n_vis, n_meta, thr0, jnp.where(jnp.logical_or(done0, col0), 1, 0),
          jnp.where(col0, n_vis, kf))
    _, _, _, _, _, thr, _, cthr = lax.while_loop(cond, step, st)
    thr = jnp.where(few, NEG, thr)
    any_tie = jnp.max(jnp.where(jnp.logical_and(jnp.logical_not(few), cthr > kf), 1.0, 0.0)) > 0.5

    @pl.when(jnp.logical_not(any_tie))
    def _():
        def emit(kj, _):
            s = s_ref[:, pl.ds(pl.multiple_of(kj * t, t), t)]
            sel = jnp.logical_and(s >= thr, s > 0.5 * NEG)
            mask_ref[:, pl.ds(pl.multiple_of(kj * t, t), t)] = jnp.where(sel, 1, 0).astype(jnp.int8)
            return 0
        lax.fori_loop(0, nk, emit, 0)

    @pl.when(any_tie)
    def _():
        allowed = kf - count_ge(thr, strict=True)
        ri = lax.broadcasted_iota(jnp.int32, (t, t), 0)
        ci = lax.broadcasted_iota(jnp.int32, (t, t), 1)
        before = (ri < ci).astype(BF16)

        def emit(kj, run):
            s = s_ref[:, pl.ds(pl.multiple_of(kj * t, t), t)]
            eq = s == thr
            eqf = jnp.where(eq, 1.0, 0.0)
            rank = run + jnp.dot(eqf.astype(BF16), before, preferred_element_type=F32)
            sel = jnp.logical_or(s > thr, jnp.logical_and(eq, rank < allowed))
            sel = jnp.logical_and(sel, s > 0.5 * NEG)
            mask_ref[:, pl.ds(pl.multiple_of(kj * t, t), t)] = jnp.where(sel, 1, 0).astype(jnp.int8)
            return run + jnp.sum(eqf, axis=-1, keepdims=True)
        lax.fori_loop(0, nk, emit, jnp.zeros((t, 1), F32))

    def clear(kj, _):
        mask_ref[:, pl.ds(pl.multiple_of(kj * t, t), t)] = jnp.zeros((t, t), jnp.int8)
        return 0
    lax.fori_loop(nk, nkt, clear, 0)


def index_mask(proj, small, kit, topk):
    lp = proj.shape[0]
    t = ROW_TILE
    ciq = COL_IQ // IDX_Q_W
    return pl.pallas_call(
        functools.partial(_index_mask_kernel, topk=topk),
        grid=(lp // t,),
        in_specs=[pl.BlockSpec((t, IDX_Q_W), lambda i: (i, ciq)),
                  pl.BlockSpec((t, SM_W), lambda i: (i, 0)),
                  pl.BlockSpec((IDX_DIM, lp), lambda i: (0, 0))],
        out_specs=pl.BlockSpec((t, lp), lambda i: (i, 0)),
        out_shape=jax.ShapeDtypeStruct((lp, lp), jnp.int8),
        scratch_shapes=[pltpu.VMEM((t, lp), F32)],
        compiler_params=_cparams(("arbitrary",)),
        name="index_mask",
    )(proj, small, kit)


def _attn_kernel(qi_ref, kj_ref, q_ref, k_ref, v_ref, mask_ref, d0_ref, d1_ref, o_ref, m_ref, l_ref, acc_ref):
    sidx = pl.program_id(0)
    qi = qi_ref[sidx]
    kj = kj_ref[sidx]

    @pl.when(kj == 0)
    def _():
        m_ref[...] = jnp.full(m_ref.shape, NEG, F32)
        l_ref[...] = jnp.zeros(l_ref.shape, F32)
        acc_ref[...] = jnp.zeros(acc_ref.shape, F32)

    def tile(bias_ref):
        sel = mask_ref[...].astype(F32) > 0.5
        for h in range(A_HEADS):
            hs = slice(h * A_DIM, (h + 1) * A_DIM)
            s = lax.dot_general(q_ref[:, hs], k_ref[:, hs], (((1,), (1,)), ((), ())), preferred_element_type=F32)
            if bias_ref is not None:
                s = s + bias_ref[h]
            s = jnp.where(sel, s, NEG)
            m_old = m_ref[h]
            m_new = jnp.maximum(m_old, jnp.max(s, axis=-1, keepdims=True))
            alpha = jnp.exp(m_old - m_new)
            p = jnp.exp(s - m_new)
            l_ref[h] = alpha * l_ref[h] + jnp.sum(p, axis=-1, keepdims=True)
            acc_ref[:, hs] = alpha * acc_ref[:, hs] + jnp.dot(p.astype(BF16), v_ref[:, hs],
                                                              preferred_element_type=F32)
            m_ref[h] = m_new

    pl.when(kj == qi)(lambda: tile(d0_ref))
    pl.when(kj == qi - 1)(lambda: tile(d1_ref))
    pl.when(kj < qi - 1)(lambda: tile(None))

    @pl.when(kj == qi)
    def _():
        for h in range(A_HEADS):
            hs = slice(h * A_DIM, (h + 1) * A_DIM)
            o_ref[:, hs] = (acc_ref[:, hs] / l_ref[h]).astype(o_ref.dtype)


def _t5_bucket_np(rel):
    rel = np.maximum(rel, 0)
    max_exact = REL_BUCKETS // 2
    rel_f = np.maximum(rel, 1).astype(np.float32)
    large = max_exact + (np.log(rel_f / np.float32(max_exact)) / np.float32(math.log(REL_MAX_DIST / max_exact))
                         * np.float32(REL_BUCKETS - max_exact)).astype(np.int32)
    large = np.minimum(large, REL_BUCKETS - 1)
    return np.where(rel < max_exact, rel, large)


def attention(proj, mask, rel_bias):
    lp = proj.shape[0]
    t = ROW_TILE
    nq = lp // t
    qi_l, kj_l = [], []
    for a in range(nq):
        for b in range(a + 1):
            qi_l.append(a)
            kj_l.append(b)
    qi_arr = jnp.asarray(np.array(qi_l, np.int32))
    kj_arr = jnp.asarray(np.array(kj_l, np.int32))
    ii = np.arange(t)[:, None]
    jj = np.arange(t)[None, :]
    far = rel_bias[REL_BUCKETS - 1].astype(F32)
    assert t >= REL_MAX_DIST
    d0 = jnp.transpose(rel_bias[_t5_bucket_np(ii - jj)].astype(F32) - far, (2, 0, 1))
    d1 = jnp.transpose(rel_bias[_t5_bucket_np(t + ii - jj)].astype(F32) - far, (2, 0, 1))
    cq, ck, cv = COL_AQ // A_W, COL_AK // A_W, COL_AV // A_W
    grid_spec = pltpu.PrefetchScalarGridSpec(
        num_scalar_prefetch=2,
        grid=(len(qi_l),),
        in_specs=[pl.BlockSpec((t, A_W), lambda s, qi, kj: (qi[s], cq)),
                  pl.BlockSpec((t, A_W), lambda s, qi, kj: (kj[s], ck)),
                  pl.BlockSpec((t, A_W), lambda s, qi, kj: (kj[s], cv)),
                  pl.BlockSpec((t, t), lambda s, qi, kj: (qi[s], kj[s])),
                  pl.BlockSpec((A_HEADS, t, t), lambda s, qi, kj: (0, 0, 0)),
                  pl.BlockSpec((A_HEADS, t, t), lambda s, qi, kj: (0, 0, 0))],
        out_specs=pl.BlockSpec((t, A_W), lambda s, qi, kj: (qi[s], 0)),
        scratch_shapes=[pltpu.VMEM((A_HEADS, t, 1), F32),
                        pltpu.VMEM((A_HEADS, t, 1), F32),
                        pltpu.VMEM((t, A_W), F32)],
    )
    return pl.pallas_call(
        _attn_kernel,
        grid_spec=grid_spec,
        out_shape=jax.ShapeDtypeStruct((lp, A_W), BF16),
        compiler_params=_cparams(("arbitrary",)),
        name="attention",
    )(qi_arr, kj_arr, proj, proj, proj, mask, d0, d1)


def _merge_kernel(ym_ref, ya_ref, gm_ref, ga_ref, h_ref, wpm_ref, wpa_ref, wo_ref, g_ref, b_ref,
                  hn_ref, hb_ref, *, alpha):
    pm = jnp.dot(ym_ref[...], wpm_ref[...], preferred_element_type=F32)
    pa = jnp.dot(ya_ref[...], wpa_ref[...], preferred_element_type=F32)
    merged = _sigmoid(gm_ref[...].astype(F32)) * pm + _sigmoid(ga_ref[...].astype(F32)) * pa
    mix = jnp.dot(merged.astype(BF16), wo_ref[...], preferred_element_type=F32)
    y = _ln(alpha * h_ref[...] + mix, g_ref[...], b_ref[...])
    hn_ref[...] = y
    hb_ref[...] = y.astype(BF16)


def merge_out_ln(ym, ya, proj, h, wpm, wpa, wo, g, b, alpha):
    lp = h.shape[0]
    tm = ROW_TILE
    cgm, cga = COL_GM // D_MODEL, COL_GA // D_MODEL
    const = lambda i: (0, 0)
    return pl.pallas_call(
        functools.partial(_merge_kernel, alpha=alpha),
        grid=(lp // tm,),
        in_specs=[pl.BlockSpec((tm, M_V_W), lambda i: (i, 0)),
                  pl.BlockSpec((tm, A_W), lambda i: (i, 0)),
                  pl.BlockSpec((tm, D_MODEL), lambda i: (i, cgm)),
                  pl.BlockSpec((tm, D_MODEL), lambda i: (i, cga)),
                  pl.BlockSpec((tm, D_MODEL), lambda i: (i, 0)),
                  pl.BlockSpec((M_V_W, D_MODEL), const, pipeline_mode=pl.Buffered(1)),
                  pl.BlockSpec((A_W, D_MODEL), const, pipeline_mode=pl.Buffered(1)),
                  pl.BlockSpec((D_MODEL, D_MODEL), const, pipeline_mode=pl.Buffered(1)),
                  pl.BlockSpec((1, D_MODEL), const),
                  pl.BlockSpec((1, D_MODEL), const)],
        out_specs=[pl.BlockSpec((tm, D_MODEL), lambda i: (i, 0)),
                   pl.BlockSpec((tm, D_MODEL), lambda i: (i, 0))],
        out_shape=[jax.ShapeDtypeStruct((lp, D_MODEL), F32),
                   jax.ShapeDtypeStruct((lp, D_MODEL), BF16)],
        compiler_params=_cparams(("parallel",)),
        name="merge_out_ln",
    )(ym, ya, proj, proj, h, wpm, wpa, wo, g.reshape(1, -1), b.reshape(1, -1))


def _router_kernel(h_ref, w_ref, b_ref, e_ref, g_ref):
    lg = jnp.dot(h_ref[...], w_ref[...], preferred_element_type=F32, precision=HI) + b_ref[...]
    tm = lg.shape[0]
    col = lax.broadcasted_iota(jnp.int32, (tm, SM_W), 1).astype(F32)
    far = 1e9

    def first_argmax(x):
        mx = jnp.max(x, axis=-1, keepdims=True)
        return mx, jnp.min(jnp.where(x == mx, col, far), axis=-1, keepdims=True)

    gl = jnp.where(col < N_GROUPS, lg, -jnp.inf)
    gmax, gsel = first_argmax(gl)
    gprob = 1.0 / jnp.sum(jnp.exp(gl - gmax), axis=-1, keepdims=True)
    c0 = N_GROUPS + EXP_PER_GROUP * gsel
    el = jnp.where(jnp.logical_and(col >= c0, col < c0 + EXP_PER_GROUP), lg, -jnp.inf)
    v1, i1 = first_argmax(el)
    v2, i2 = first_argmax(jnp.where(col == i1, -jnp.inf, el))
    e21 = jnp.exp(v2 - v1)
    g1 = gprob / (1.0 + e21)
    g2 = gprob * e21 / (1.0 + e21)
    e_ref[...] = jnp.where(col == 0, i1 - N_GROUPS, jnp.where(col == 1, i2 - N_GROUPS, 0.0)).astype(jnp.int32)
    g_ref[...] = jnp.where(col == 0, g1, jnp.where(col == 1, g2, 0.0))


def router(h, w, b):
    lp = h.shape[0]
    tm = ROW_TILE
    return pl.pallas_call(
        _router_kernel,
        grid=(lp // tm,),
        in_specs=[pl.BlockSpec((tm, D_MODEL), lambda i: (i, 0)),
                  pl.BlockSpec((D_MODEL, SM_W), lambda i: (0, 0)),
                  pl.BlockSpec((1, SM_W), lambda i: (0, 0))],
        out_specs=[pl.BlockSpec((tm, SM_W), lambda i: (i, 0)),
                   pl.BlockSpec((tm, SM_W), lambda i: (i, 0))],
        out_shape=[jax.ShapeDtypeStruct((lp, SM_W), jnp.int32),
                   jax.ShapeDtypeStruct((lp, SM_W), F32)],
        compiler_params=_cparams(("parallel",)),
        name="router",
    )(h, w, b)


def _expert_kernel(blk_e_ref, tok_ref, slot_ref, nused_ref, x_hbm, w_ref, wgu_ref, wd_ref, y_hbm,
                   xbuf, ybuf, wgu_b, wd_b, gsem, ssem):
    b = pl.program_id(0)
    nb = pl.num_programs(0)
    nused = nused_ref[0]
    blk = MOE_BLOCK

    def gather_copy(block, r, slot):
        tok = tok_ref[block * blk + r]
        return pltpu.make_async_copy(x_hbm.at[pl.ds(tok, 1), :], xbuf.at[slot, pl.ds(r, 1), :], gsem.at[slot])

    def start_gather(block, slot):
        def body(r, _):
            gather_copy(block, r, slot).start()
            return 0
        lax.fori_loop(0, blk, body, 0)

    def wait_gather(block, slot):
        def body(r, _):
            gather_copy(block, r, slot).wait()
            return 0
        lax.fori_loop(0, blk, body, 0)

    def scatter_copy(block, r):
        dst = slot_ref[block * blk + r]
        return dst, pltpu.make_async_copy(ybuf.at[pl.ds(r, 1), :], y_hbm.at[pl.ds(jnp.maximum(dst, 0), 1), :], ssem.at[0])

    def start_scatter(block):
        def body(r, _):
            dst, cp = scatter_copy(block, r)

            @pl.when(dst >= 0)
            def _():
                cp.start()
            return 0
        lax.fori_loop(0, blk, body, 0)

    def wait_scatter(block):
        def body(r, _):
            dst, cp = scatter_copy(block, r)

            @pl.when(dst >= 0)
            def _():
                cp.wait()
            return 0
        lax.fori_loop(0, blk, body, 0)

    @pl.when(jnp.logical_and(b == 0, nused > 0))
    def _():
        start_gather(0, 0)

    @pl.when(b + 1 < nused)
    def _():
        start_gather(b + 1, (b + 1) % 2)

    @pl.when(b < nused)
    def _():
        changed = jnp.logical_or(b == 0, blk_e_ref[b] != blk_e_ref[jnp.maximum(b - 1, 0)])

        @pl.when(changed)
        def _():
            wgu_b[...] = wgu_ref[0].astype(BF16)
            wd_b[...] = wd_ref[0].astype(BF16)

        slot = b % 2
        wait_gather(b, slot)
        xb = xbuf[slot].astype(BF16)
        gu = jnp.dot(xb, wgu_b[...], preferred_element_type=F32)
        g = gu[:, :D_FF]
        hb = (g * _sigmoid(g)) * gu[:, D_FF:]
        y = jnp.dot(hb.astype(BF16), wd_b[...], preferred_element_type=F32) * w_ref[...]

        @pl.when(b > 0)
        def _():
            wait_scatter(b - 1)

        ybuf[...] = y
        start_scatter(b)

    @pl.when(jnp.logical_and(b < nused, jnp.logical_or(b == nb - 1, b + 1 >= nused)))
    def _():
        wait_scatter(b)


def expert_ffn(h, blk_e, row_tok, row_slot, nused, row_w, wgu, wd):
    lp = h.shape[0]
    n_assign = 2 * lp
    n_blocks = row_tok.shape[0] // MOE_BLOCK
    grid_spec = pltpu.PrefetchScalarGridSpec(
        num_scalar_prefetch=4,
        grid=(n_blocks,),
        in_specs=[pl.BlockSpec(memory_space=pl.ANY),
                  pl.BlockSpec((MOE_BLOCK, 1), lambda b, e, t, s, n: (b, 0)),
                  pl.BlockSpec((1, D_MODEL, 2 * D_FF), lambda b, e, t, s, n: (e[b], 0, 0)),
                  pl.BlockSpec((1, D_FF, D_MODEL), lambda b, e, t, s, n: (e[b], 0, 0))],
        out_specs=pl.BlockSpec(memory_space=pl.ANY),
        scratch_shapes=[pltpu.VMEM((2, MOE_BLOCK, D_MODEL), F32),
                        pltpu.VMEM((MOE_BLOCK, D_MODEL), F32),
                        pltpu.VMEM((D_MODEL, 2 * D_FF), BF16),
                        pltpu.VMEM((D_FF, D_MODEL), BF16),
                        pltpu.SemaphoreType.DMA((2,)),
                        pltpu.SemaphoreType.DMA((1,))],
    )
    return pl.pallas_call(
        _expert_kernel,
        grid_spec=grid_spec,
        out_shape=jax.ShapeDtypeStruct((n_assign, D_MODEL), F32),
        compiler_params=_cparams(("arbitrary",)),
        name="expert_ffn",
    )(blk_e, row_tok, row_slot, nused, h, row_w, wgu, wd)


def moe_dispatch(e_ids, gates):
    n_assign = e_ids.shape[0] * 2
    e_flat = e_ids.reshape(n_assign)
    w_flat = gates.reshape(n_assign)
    onehot = (e_flat[:, None] == jnp.arange(N_EXPERTS, dtype=jnp.int32)[None, :]).astype(jnp.int32)
    csum = jnp.cumsum(onehot, axis=0)
    rank = jnp.sum(onehot * csum, axis=1) - 1
    counts = csum[-1]
    padded = (counts + MOE_BLOCK - 1) // MOE_BLOCK * MOE_BLOCK
    pend = jnp.cumsum(padded)
    pstart = pend - padded
    dest = pstart[e_flat] + rank
    n_blocks = n_assign // MOE_BLOCK + N_EXPERTS
    n_rows = n_blocks * MOE_BLOCK
    row_slot = jnp.full((n_rows,), -1, jnp.int32).at[dest].set(jnp.arange(n_assign, dtype=jnp.int32))
    row_tok = jnp.maximum(row_slot, 0) >> 1
    row_w = jnp.where(row_slot >= 0, w_flat[jnp.maximum(row_slot, 0)], 0.0).reshape(n_rows, 1)
    blk_e = jnp.minimum(jnp.searchsorted(pend, jnp.arange(n_blocks, dtype=jnp.int32) * MOE_BLOCK, side='right'),
                        N_EXPERTS - 1).astype(jnp.int32)
    nused = (pend[-1] // MOE_BLOCK).astype(jnp.int32).reshape(1)
    return blk_e, row_tok, row_slot, nused, row_w


def _combine_kernel(h_ref, y_ref, g_ref, b_ref, hn_ref, hb_ref, *, alpha):
    ffn = y_ref[:, :D_MODEL] + y_ref[:, D_MODEL:]
    y = _ln(alpha * h_ref[...] + ffn, g_ref[...], b_ref[...])
    hn_ref[...] = y
    hb_ref[...] = y.astype(BF16)


def combine_ln(h, y2, g, b, alpha):
    lp = h.shape[0]
    tm = ROW_TILE
    return pl.pallas_call(
        functools.partial(_combine_kernel, alpha=alpha),
        grid=(lp // tm,),
        in_specs=[pl.BlockSpec((tm, D_MODEL), lambda i: (i, 0)),
                  pl.BlockSpec((tm, 2 * D_MODEL), lambda i: (i, 0)),
                  pl.BlockSpec((1, D_MODEL), lambda i: (0, 0)),
                  pl.BlockSpec((1, D_MODEL), lambda i: (0, 0))],
        out_specs=[pl.BlockSpec((tm, D_MODEL), lambda i: (i, 0)),
                   pl.BlockSpec((tm, D_MODEL), lambda i: (i, 0))],
        out_shape=[jax.ShapeDtypeStruct((lp, D_MODEL), F32),
                   jax.ShapeDtypeStruct((lp, D_MODEL), BF16)],
        compiler_params=_cparams(("parallel",)),
        name="combine_ln",
    )(h, y2.reshape(lp, 2 * D_MODEL), g.reshape(1, -1), b.reshape(1, -1))


def _split_w_in(w):
    off = {}
    o = 0
    for name, width in (("mq", M_QK_W), ("mk", M_QK_W), ("mv", M_V_W), ("mo", M_V_W), ("mi", M_HEADS),
                        ("mf", M_HEADS), ("aq", A_W), ("ak", A_W), ("av", A_W), ("iq", IDX_Q_W),
                        ("ik", IDX_DIM), ("iw", IDX_HEADS), ("gm", D_MODEL), ("ga", D_MODEL)):
        off[name] = (o, o + width)
        o += width
    seg = lambda n: w[:, off[n][0]:off[n][1]]
    big = jnp.concatenate([seg("gm"), seg("ga"), seg("mv"), seg("mo"), seg("aq") * (A_DIM ** -0.5),
                           seg("ak"), seg("av"), seg("mq"), seg("mk"), seg("iq") * (IDX_DIM ** -0.5)],
                          axis=1).astype(BF16)
    small = jnp.concatenate([seg("mi"), seg("mf"), seg("iw") * (IDX_HEADS ** -0.5),
                             jnp.zeros((w.shape[0], SM_W - 2 * M_HEADS - IDX_HEADS - IDX_DIM), F32),
                             seg("ik")], axis=1)
    return big, small


def _layer(h, hb, p, rel_bias, topk, alpha):
    w_big, w_small = _split_w_in(p["w_in"])
    proj = in_proj_big(hb, w_big)
    small = in_proj_small(h, w_small)
    small_t = small[:, 0:8].T
    kit = small[:, SM_W - IDX_DIM:].T.astype(BF16)
    y_m = mlstm(proj, small, small_t, p["conv_w"], p["conv_b"], p["b_igate"], p["b_fgate"], p["mh_norm_g"])
    mask = index_mask(proj, small, kit, topk)
    y_a = attention(proj, mask, rel_bias)
    h, hb = merge_out_ln(y_m, y_a, proj, h, p["w_proj_m"].astype(BF16), p["w_proj_a"].astype(BF16),
                         p["w_out"].astype(BF16), p["ln1_g"], p["ln1_b"], alpha)
    w_r = jnp.concatenate([p["w_group"], p["w_router"],
                           jnp.zeros((D_MODEL, SM_W - N_GROUPS - N_EXPERTS), F32)], axis=1)
    b_r = jnp.concatenate([p["b_group"], p["b_router"],
                           jnp.zeros((SM_W - N_GROUPS - N_EXPERTS,), F32)]).reshape(1, SM_W)
    e_out, g_out = router(h, w_r, b_r)
    blk_e, row_tok, row_slot, nused, row_w = moe_dispatch(e_out[:, 0:2], g_out[:, 0:2])
    y2 = expert_ffn(h, blk_e, row_tok, row_slot, nused, row_w, p["w_gate_up"], p["w_down"])
    return combine_ln(h, y2, p["ln2_g"], p["ln2_b"], alpha)


def _trunk(x2, meta_tokens, ln_emb_g, ln_emb_b, rel_bias, layers):
    depth = len(layers)
    alpha = (2 * depth) ** 0.25
    seq = x2.shape[0]
    length = N_META + seq
    lp = -(-length // ROW_TILE) * ROW_TILE
    topk = min(TOPK_MAX, length // 4)
    xp = jnp.concatenate([meta_tokens.astype(x2.dtype), x2, jnp.zeros((lp - length, D_MODEL), x2.dtype)], axis=0)
    h, hb = embed_ln(xp, ln_emb_g, ln_emb_b)
    for p in layers:
        h, hb = _layer(h, hb, p, rel_bias, topk, alpha)
    return h[N_META:length]


def kernel(x, meta_tokens, ln_emb_g, ln_emb_b, rel_bias, w_in, conv_w, conv_b, b_igate, b_fgate, mh_norm_g,
           w_proj_m, w_proj_a, w_out, ln1_g, ln1_b, w_group, b_group, w_router, b_router, w_gate_up, w_down,
           ln2_g, ln2_b):
    depth = w_in.shape[0]
    layers = [dict(w_in=w_in[l], conv_w=conv_w[l], conv_b=conv_b[l], b_igate=b_igate[l], b_fgate=b_fgate[l],
                   mh_norm_g=mh_norm_g[l], w_proj_m=w_proj_m[l], w_proj_a=w_proj_a[l], w_out=w_out[l],
                   ln1_g=ln1_g[l], ln1_b=ln1_b[l], w_group=w_group[l], b_group=b_group[l],
                   w_router=w_router[l], b_router=b_router[l], w_gate_up=w_gate_up[l], w_down=w_down[l],
                   ln2_g=ln2_g[l], ln2_b=ln2_b[l]) for l in range(depth)]
    outs = [_trunk(x[b], meta_tokens, ln_emb_g, ln_emb_b, rel_bias, layers) for b in range(x.shape[0])]
    return jnp.stack(outs, axis=0)
```

```python
import functools
import math

import jax
import jax.numpy as jnp
import numpy as np
from jax import lax
from jax.experimental import pallas as pl
from jax.experimental.pallas import tpu as pltpu

D_MODEL = 2048
N_META = 16
M_HEADS = 4
M_QK = 128
M_V = 256
CONV_W = 4
A_HEADS = 8
A_DIM = 128
TOPK_MAX = 256
IDX_HEADS = 8
IDX_DIM = 64
REL_BUCKETS = 32
REL_MAX_DIST = 128
N_GROUPS = 4
EXP_PER_GROUP = 8
N_EXPERTS = 32
D_FF = 512
MOE_BLOCK = 128
LN_EPS = 1e-5
LOG2E = math.log2(math.e)
NEG = -1e30
BIG = 1e30

M_QK_W = M_HEADS * M_QK
M_V_W = M_HEADS * M_V
A_W = A_HEADS * A_DIM
IDX_Q_W = IDX_HEADS * IDX_DIM

COL_GM = 0
COL_GA = 2048
COL_MV = 4096
COL_MO = 5120
COL_AQ = 6144
COL_AK = 7168
COL_AV = 8192
COL_MQK = 9216
COL_IQ = 10240
BIG_W = 10752
SM_W = 128

ROW_TILE = 256
Q_TILE = 512
PAD_TO = 512
F32 = jnp.float32
BF16 = jnp.bfloat16
VMEM_LIMIT = 56 * 1024 * 1024
HI = lax.Precision.HIGHEST


def _cparams(sem):
    return pltpu.CompilerParams(dimension_semantics=sem, vmem_limit_bytes=VMEM_LIMIT)


def _pick(n, cands):
    for c in cands:
        if n % c == 0:
            return c
    raise ValueError(f"no tile for {n}")


def _ln(x, g, b):
    mu = jnp.mean(x, axis=-1, keepdims=True)
    xc = x - mu
    var = jnp.mean(xc * xc, axis=-1, keepdims=True)
    return xc * lax.rsqrt(var + LN_EPS) * g + b


def _sigmoid(x):
    return 1.0 / (1.0 + jnp.exp(-x))


def _embed_ln_kernel(x_ref, g_ref, b_ref, h_ref, hb_ref):
    y = _ln(x_ref[...], g_ref[...], b_ref[...])
    h_ref[...] = y
    hb_ref[...] = y.astype(BF16)


def embed_ln(xp, g, b):
    lp = xp.shape[0]
    tm = ROW_TILE
    return pl.pallas_call(
        _embed_ln_kernel,
        grid=(lp // tm,),
        in_specs=[pl.BlockSpec((tm, D_MODEL), lambda i: (i, 0)),
                  pl.BlockSpec((1, D_MODEL), lambda i: (0, 0)),
                  pl.BlockSpec((1, D_MODEL), lambda i: (0, 0))],
        out_specs=[pl.BlockSpec((tm, D_MODEL), lambda i: (i, 0)),
                   pl.BlockSpec((tm, D_MODEL), lambda i: (i, 0))],
        out_shape=[jax.ShapeDtypeStruct((lp, D_MODEL), F32),
                   jax.ShapeDtypeStruct((lp, D_MODEL), BF16)],
        compiler_params=_cparams(("parallel",)),
        name="embed_ln",
    )(xp, g.reshape(1, -1), b.reshape(1, -1))


def _mm_kernel(a_ref, b_ref, o_ref):
    o_ref[...] = jnp.dot(a_ref[...], b_ref[...], preferred_element_type=F32).astype(o_ref.dtype)


def in_proj_big(hb, w):
    lp = hb.shape[0]
    tm = _pick(lp, (1536, 1280, 1024, 768, 512))
    tn = 1536
    return pl.pallas_call(
        _mm_kernel,
        grid=(BIG_W // tn, lp // tm),
        in_specs=[pl.BlockSpec((tm, D_MODEL), lambda j, i: (i, 0)),
                  pl.BlockSpec((D_MODEL, tn), lambda j, i: (0, j))],
        out_specs=pl.BlockSpec((tm, tn), lambda j, i: (i, j)),
        out_shape=jax.ShapeDtypeStruct((lp, BIG_W), BF16),
        compiler_params=_cparams(("parallel", "parallel")),
        name="in_proj_big",
    )(hb, w)


def _mm_hi_kernel(a_ref, b_ref, o_ref):
    o_ref[...] = jnp.dot(a_ref[...], b_ref[...], preferred_element_type=F32, precision=HI)


def in_proj_small(h, w):
    lp = h.shape[0]
    tm = ROW_TILE
    return pl.pallas_call(
        _mm_hi_kernel,
        grid=(lp // tm,),
        in_specs=[pl.BlockSpec((tm, D_MODEL), lambda i: (i, 0)),
                  pl.BlockSpec((D_MODEL, SM_W), lambda i: (0, 0))],
        out_specs=pl.BlockSpec((tm, SM_W), lambda i: (i, 0)),
        out_shape=jax.ShapeDtypeStruct((lp, SM_W), F32),
        compiler_params=_cparams(("parallel",)),
        name="in_proj_small",
    )(h, w)


def _log_sigmoid(x):
    return jnp.minimum(x, 0.0) - jnp.log1p(jnp.exp(-jnp.abs(x)))


def _mlstm_kernel(qk_ref, v_ref, o_ref, sm_ref, smt_ref, cw_ref, cb_ref, gb_ref, gbt_ref, mhg_ref,
                  y_ref, ubuf, c_ref, n_ref, m_ref):
    t = ROW_TILE
    step = pl.program_id(0)

    @pl.when(step == 0)
    def _():
        ubuf[0:8, :] = jnp.zeros((8, 2 * M_QK_W), F32)
        c_ref[...] = jnp.zeros_like(c_ref)
        n_ref[...] = jnp.zeros_like(n_ref)
        m_ref[...] = jnp.zeros_like(m_ref)

    ubuf[8:8 + t, :] = qk_ref[...].astype(F32)
    conv = cb_ref[...]
    for j in range(CONV_W):
        d = CONV_W - 1 - j
        conv = conv + cw_ref[j:j + 1, :] * ubuf[8 - d:8 - d + t, :]
    ubuf[0:8, :] = ubuf[t:t + 8, :]
    qk = conv * _sigmoid(conv)

    gcol = sm_ref[...] + gb_ref[...]
    grow = smt_ref[0:8, :] + gbt_ref[...]
    lane = lax.broadcasted_iota(jnp.int32, (t, SM_W), 1)
    gcol = jnp.where(lane < M_HEADS, gcol, _log_sigmoid(gcol))
    srow = lax.broadcasted_iota(jnp.int32, (8, t), 0)
    grow = jnp.where(srow < M_HEADS, grow, _log_sigmoid(grow))
    ri = lax.broadcasted_iota(jnp.int32, (t, t), 0)
    ci = lax.broadcasted_iota(jnp.int32, (t, t), 1)
    causal = ci <= ri
    tri = causal.astype(F32)
    bcol = jnp.dot(tri, gcol, preferred_element_type=F32, precision=HI)
    brow = lax.dot_general(grow, tri, (((1,), (1,)), ((), ())), preferred_element_type=F32,
                           precision=HI)

    v_all = v_ref[...]
    sig_o = _sigmoid(o_ref[...].astype(F32))
    for h in range(M_HEADS):
        q = qk[:, h * M_QK:(h + 1) * M_QK] * (M_QK ** -0.5)
        k = qk[:, M_QK_W + h * M_QK:M_QK_W + (h + 1) * M_QK]
        v = v_all[:, h * M_V:(h + 1) * M_V]
        qb = q.astype(BF16)
        kb = k.astype(BF16)
        b_c = bcol[:, M_HEADS + h:M_HEADS + h + 1]
        ig_c = gcol[:, h:h + 1]
        b_r = brow[M_HEADS + h:M_HEADS + h + 1, :]
        ig_r = grow[h:h + 1, :]
        m_prev = m_ref[h, 0:1, 0:1]
        dmat = jnp.where(causal, b_c + (ig_r - b_r), NEG)
        inter = b_c + m_prev
        m_t = jnp.maximum(jnp.max(dmat, axis=-1, keepdims=True), inter)
        s = lax.dot_general(qb, kb, (((1,), (1,)), ((), ())), preferred_element_type=F32)
        w_intra = jnp.exp(dmat - m_t) * s
        w_inter = jnp.exp(inter - m_t)
        c_old = c_ref[h]
        n_old = n_ref[h, 0:1, :]
        num = (jnp.dot(w_intra.astype(BF16), v, preferred_element_type=F32)
               + w_inter * jnp.dot(qb, c_old.astype(BF16), preferred_element_type=F32))
        den = (jnp.sum(w_intra, axis=-1, keepdims=True)
               + w_inter * jnp.sum(q * n_old, axis=-1, keepdims=True))
        hh = num / jnp.maximum(jnp.abs(den), jnp.exp(-m_t))
        mu = jnp.mean(hh, axis=-1, keepdims=True)
        hc = hh - mu
        var = jnp.mean(hc * hc, axis=-1, keepdims=True)
        hn = hc * lax.rsqrt(var + LN_EPS) * mhg_ref[:, h * M_V:(h + 1) * M_V]
        y_ref[:, h * M_V:(h + 1) * M_V] = (sig_o[:, h * M_V:(h + 1) * M_V] * hn).astype(y_ref.dtype)
        b_last = b_c[t - 1:t, :]
        dec = b_last - b_c + ig_c
        m_new = jnp.maximum(b_last + m_prev, jnp.max(dec, axis=0, keepdims=True))
        w_src = jnp.exp(dec - m_new)
        cs = jnp.exp(b_last + m_prev - m_new)
        wv = (w_src * v.astype(F32)).astype(BF16)
        c_ref[h] = cs * c_old + jnp.dot(k.T.astype(BF16), wv, preferred_element_type=F32)
        n_new = cs * n_old + jnp.sum(w_src * k, axis=0, keepdims=True)
        n_ref[h] = jnp.broadcast_to(n_new, (8, M_QK))
        m_ref[h] = jnp.broadcast_to(m_new, (8, 128))


def mlstm(proj, small, small_t, conv_w, conv_b, b_ig, b_fg, mh_g):
    lp = proj.shape[0]
    t = ROW_TILE
    gb = jnp.zeros((1, SM_W), F32).at[0, 0:M_HEADS].set(b_ig).at[0, M_HEADS:2 * M_HEADS].set(b_fg)
    gbt = gb[0, 0:8].reshape(8, 1)
    cq = COL_MQK // (2 * M_QK_W)
    cv = COL_MV // M_V_W
    co = COL_MO // M_V_W
    return pl.pallas_call(
        _mlstm_kernel,
        grid=(lp // t,),
        in_specs=[pl.BlockSpec((t, 2 * M_QK_W), lambda i: (i, cq)),
                  pl.BlockSpec((t, M_V_W), lambda i: (i, cv)),
                  pl.BlockSpec((t, M_V_W), lambda i: (i, co)),
                  pl.BlockSpec((t, SM_W), lambda i: (i, 0)),
                  pl.BlockSpec((16, t), lambda i: (0, i)),
                  pl.BlockSpec((CONV_W, 2 * M_QK_W), lambda i: (0, 0)),
                  pl.BlockSpec((1, 2 * M_QK_W), lambda i: (0, 0)),
                  pl.BlockSpec((1, SM_W), lambda i: (0, 0)),
                  pl.BlockSpec((8, 1), lambda i: (0, 0)),
                  pl.BlockSpec((1, M_V_W), lambda i: (0, 0))],
        out_specs=pl.BlockSpec((t, M_V_W), lambda i: (i, 0)),
        out_shape=jax.ShapeDtypeStruct((lp, M_V_W), BF16),
        scratch_shapes=[pltpu.VMEM((t + 8, 2 * M_QK_W), F32),
                        pltpu.VMEM((M_HEADS, M_QK, M_V), F32),
                        pltpu.VMEM((M_HEADS, 8, M_QK), F32),
                        pltpu.VMEM((M_HEADS, 8, 128), F32)],
        compiler_params=_cparams(("arbitrary",)),
        name="mlstm",
    )(proj, proj, proj, small, small_t, conv_w, conv_b.reshape(1, -1), gb, gbt, mh_g.reshape(1, -1))


def _f2key(f):
    b = pltpu.bitcast(f, jnp.int32)
    return jnp.where(b >= 0, b, b ^ jnp.int32(0x7FFFFFFF))


def _key2f(k):
    return pltpu.bitcast(jnp.where(k >= 0, k, k ^ jnp.int32(0x7FFFFFFF)), F32)


IDX_MAXIT = 128
IDX_VALUE_BISECT_ITERS = 48
KEY_MIN_NORMAL = 0x00800000
IDX_TQ = 256


def _col_reduce(x, op):
    n = x.shape[0] // 8
    parts = [x[g * 8:(g + 1) * 8, :] for g in range(n)]
    while len(parts) > 1:
        nxt = [op(parts[a], parts[a + 1]) for a in range(0, len(parts) - 1, 2)]
        if len(parts) % 2:
            nxt.append(parts[-1])
        parts = nxt
    return parts[0]


def _index_mask_kernel(iqt_ref, wt_ref, ki_ref, mask_ref, s_ref, *, topk):
    tq = IDX_TQ
    tk = ROW_TILE
    i = pl.program_id(0)
    nk = i + 1
    nkt = pl.num_programs(0)
    q_pos = i * tq + lax.broadcasted_iota(jnp.int32, (1, tq), 1)
    k_iota = lax.broadcasted_iota(jnp.int32, (tk, 1), 0)

    def raw_scores(kj):
        kt = ki_ref[pl.ds(pl.multiple_of(kj * tk, tk), tk), :]
        acc = jnp.zeros((tk, tq), F32)
        for h in range(IDX_HEADS):
            x = jnp.dot(kt, iqt_ref[h * IDX_DIM:(h + 1) * IDX_DIM, :], preferred_element_type=F32)
            acc = acc + wt_ref[8 + h:9 + h, :] * jnp.maximum(x, 0.0)
        return acc

    def edge_tile(kj, vmin, vmax):
        acc = raw_scores(kj)
        key_pos = kj * tk + k_iota
        visible = key_pos <= q_pos
        ordinary = jnp.logical_and(visible, key_pos >= N_META)
        s_ref[pl.ds(pl.multiple_of(kj * tk, tk), tk), :] = jnp.where(
            visible, jnp.where(key_pos < N_META, BIG, acc), NEG)
        vmin = jnp.minimum(vmin, _col_reduce(jnp.where(ordinary, acc, BIG), jnp.minimum))
        vmax = jnp.maximum(vmax, _col_reduce(jnp.where(ordinary, acc, NEG), jnp.maximum))
        return vmin, vmax

    def inner_tile(kj, carry):
        vmin, vmax = carry
        acc = raw_scores(kj)
        s_ref[pl.ds(pl.multiple_of(kj * tk, tk), tk), :] = acc
        return (jnp.minimum(vmin, _col_reduce(acc, jnp.minimum)),
                jnp.maximum(vmax, _col_reduce(acc, jnp.maximum)))

    vmin, vmax = edge_tile(0, jnp.full((8, tq), BIG, F32), jnp.full((8, tq), NEG, F32))
    vmin, vmax = lax.fori_loop(1, jnp.maximum(i, 1), inner_tile, (vmin, vmax))
    vmin, vmax = lax.cond(i > 0, lambda a, b: edge_tile(i, a, b), lambda a, b: (a, b), vmin, vmax)
    vmin = jnp.min(vmin, axis=0, keepdims=True)
    vmax = jnp.max(vmax, axis=0, keepdims=True)

    def count(x, strict=False):
        xb = jnp.broadcast_to(x, (8, tq))

        def body(kj, acc):
            tile = s_ref[pl.ds(pl.multiple_of(kj * tk, tk), tk), :]
            n = tk // 8
            hits = []
            for g in range(n):
                row = tile[g * 8:(g + 1) * 8, :]
                hit = (row > xb) if strict else (row >= xb)
                hits.append(jnp.where(hit, 1.0, 0.0))
            while len(hits) > 1:
                hits = [hits[a] + hits[a + 1] for a in range(0, len(hits), 2)]
            return acc + hits[0]

        acc = lax.fori_loop(0, nk, body, jnp.zeros((8, tq), F32))
        return jnp.sum(acc, axis=0, keepdims=True)

    kf = float(topk)
    n_vis = (q_pos + 1).astype(F32)
    n_meta = jnp.minimum(n_vis, float(N_META))
    few = n_vis <= kf
    lo_k0 = _f2key(vmin)
    hi_k0 = _f2key(vmax) + 1
    done0 = jnp.logical_or(few, n_vis - n_meta < 0.5)
    thr0 = jnp.full((1, tq), 0.5 * NEG, F32)

    def cond(st):
        it, lo_k, hi_k, clo, chi, thr, done_i, cthr = st
        return jnp.logical_and(it < IDX_MAXIT, jnp.min(done_i) < 1)

    def step(st):
        it, lo_k, hi_k, clo, chi, thr, done_i, cthr = st
        done = done_i > 0
        lo_v = _key2f(lo_k)
        hi_v = _key2f(hi_k)
        frac = jnp.log(clo / (kf + 0.5)) / jnp.log(clo / jnp.maximum(chi, 0.5))
        frac = jnp.clip(frac, 0.0, 1.0)
        pk_i = _f2key(lo_v + (hi_v - lo_v) * frac)
        pk_m = jnp.where(it < IDX_VALUE_BISECT_ITERS, _f2key(lo_v + 0.5 * (hi_v - lo_v)),
                         (lo_k >> 1) + (hi_k >> 1) + (lo_k & hi_k & 1))
        pk_m = jnp.where(jnp.logical_and(lo_k == 0, hi_k > KEY_MIN_NORMAL), KEY_MIN_NORMAL, pk_m)
        pk_m = jnp.where(jnp.logical_and(lo_k < 0, hi_k > 0), 0, pk_m)
        pk = jnp.where(it % 2 == 0, pk_i, pk_m)
        pk = jnp.minimum(jnp.maximum(pk, lo_k + 1), hi_k - 1)
        pk = jnp.where(done, lo_k, pk)
        x = _key2f(pk)
        c = count(x)
        ge = c >= kf
        act = jnp.logical_not(done)
        up = jnp.logical_and(act, ge)
        dn = jnp.logical_and(act, jnp.logical_not(ge))
        lo_k = jnp.where(up, pk, lo_k)
        clo = jnp.where(up, c, clo)
        hi_k = jnp.where(dn, pk, hi_k)
        chi = jnp.where(dn, c, chi)
        exact = jnp.logical_and(act, c == kf)
        collapsed = jnp.logical_and(act, jnp.logical_or(
            hi_k <= lo_k + 1, jnp.logical_and(lo_k == 0, hi_k <= KEY_MIN_NORMAL)))
        fin = jnp.logical_or(exact, collapsed)
        thr = jnp.where(exact, x, jnp.where(collapsed, _key2f(lo_k), thr))
        cthr = jnp.where(exact, c, jnp.where(collapsed, clo, cthr))
        return it + 1, lo_k, hi_k, clo, chi, thr, jnp.where(jnp.logical_or(done, fin), 1, 0), cthr

    col0 = jnp.logical_and(jnp.logical_not(done0), hi_k0 <= lo_k0 + 1)
    thr0 = jnp.where(col0, vmin, thr0)
    st = (jnp.int32(0), lo_k0, hi_k0, n_vis, n_meta, thr0, jnp.where(jnp.logical_or(done0, col0), 1, 0),
          jnp.where(col0, n_vis, kf))
    _, _, _, _, _, thr, _, cthr = lax.while_loop(cond, step, st)
    thr = jnp.where(few, 0.5 * NEG, thr)
    any_tie = jnp.max(jnp.where(jnp.logical_and(jnp.logical_not(few), cthr > kf), 1.0, 0.0)) > 0.5

    @pl.when(jnp.logical_not(any_tie))
    def _():
        def emit(kj, _):
            s = s_ref[pl.ds(pl.multiple_of(kj * tk, tk), tk), :]
            mask_ref[pl.ds(pl.multiple_of(kj * tk, tk), tk), :] = jnp.where(s >= thr, 1, 0).astype(jnp.int8)
            return 0
        lax.fori_loop(0, nk, emit, 0)

    @pl.when(any_tie)
    def _():
        allowed = kf - count(thr, strict=True)
        ri = lax.broadcasted_iota(jnp.int32, (tk, tk), 0)
        ci = lax.broadcasted_iota(jnp.int32, (tk, tk), 1)
        before = (ci < ri).astype(BF16)

        def emit(kj, run):
            s = s_ref[pl.ds(pl.multiple_of(kj * tk, tk), tk), :]
            eq = s == thr
            eqf = jnp.where(eq, 1.0, 0.0)
            rank = run + jnp.dot(before, eqf.astype(BF16), preferred_element_type=F32)
            sel = jnp.logical_or(s > thr, jnp.logical_and(eq, rank < allowed))
            mask_ref[pl.ds(pl.multiple_of(kj * tk, tk), tk), :] = jnp.where(sel, 1, 0).astype(jnp.int8)
            return run + jnp.sum(eqf, axis=0, keepdims=True)
        lax.fori_loop(0, nk, emit, jnp.zeros((1, tq), F32))

    def clear(kj, _):
        mask_ref[pl.ds(pl.multiple_of(kj * tk, tk), tk), :] = jnp.zeros((tk, tq), jnp.int8)
        return 0
    lax.fori_loop(nk, nkt, clear, 0)


def index_mask(iq_t, small_t, ki, topk):
    lp = iq_t.shape[1]
    tq = IDX_TQ
    assert IDX_TQ == ROW_TILE
    return pl.pallas_call(
        functools.partial(_index_mask_kernel, topk=topk),
        grid=(lp // tq,),
        in_specs=[pl.BlockSpec((IDX_Q_W, tq), lambda i: (0, i)),
                  pl.BlockSpec((16, tq), lambda i: (0, i)),
                  pl.BlockSpec((lp, IDX_DIM), lambda i: (0, 0), pipeline_mode=pl.Buffered(1))],
        out_specs=pl.BlockSpec((lp, tq), lambda i: (0, i)),
        out_shape=jax.ShapeDtypeStruct((lp, lp), jnp.int8),
        scratch_shapes=[pltpu.VMEM((lp, tq), F32)],
        compiler_params=_cparams(("arbitrary",)),
        name="index_mask",
    )(iq_t, small_t, ki)


ATTN_AHEAD = 2
ONES_ROWS = 16


def _attn_kernel(qi_ref, kj_ref, qt_ref, k_ref, vt_ref, mask_ref, bp_ref, b0_ref, bm_ref, o_ref,
                 m_ref, l_ref, acc_ref, neg_ref):
    tq, tk = Q_TILE, ROW_TILE
    sidx = pl.program_id(0)
    qi = qi_ref[sidx]
    kj = kj_ref[sidx]
    delta = kj - 2 * qi

    @pl.when(kj == 0)
    def _():
        m_ref[...] = jnp.full(m_ref.shape, NEG, F32)
        l_ref[...] = jnp.zeros(l_ref.shape, F32)
        acc_ref[...] = jnp.zeros(acc_ref.shape, F32)

    def tile(bias_ref):
        neg_ref[...] = jnp.where(mask_ref[...].astype(F32) > 0.5, 0.0, NEG)
        ones = jnp.ones((ONES_ROWS, tk), BF16)

        def logits(h):
            hs = slice(h * A_DIM, (h + 1) * A_DIM)
            s = jnp.dot(k_ref[:, hs], qt_ref[hs, :], preferred_element_type=F32) + neg_ref[...]
            if bias_ref is not None:
                s = s + bias_ref[h]
            return s

        ahead = [logits(h) for h in range(min(ATTN_AHEAD, A_HEADS))]
        for h in range(A_HEADS):
            hs = slice(h * A_DIM, (h + 1) * A_DIM)
            if h + ATTN_AHEAD < A_HEADS:
                ahead.append(logits(h + ATTN_AHEAD))
            s = ahead.pop(0)
            m_old = m_ref[h]
            m_new = jnp.maximum(m_old, jnp.max(s, axis=0, keepdims=True))
            alpha = jnp.exp2(m_old - m_new)
            p = jnp.exp2(s - m_new).astype(BF16)
            pv = jnp.dot(jnp.concatenate([vt_ref[hs, :], ones], axis=0), p, preferred_element_type=F32)
            acc_ref[hs, :] = alpha * acc_ref[hs, :] + pv[0:A_DIM, :]
            l_ref[h] = alpha * l_ref[h] + pv[A_DIM:A_DIM + 1, :]
            m_ref[h] = m_new

    pl.when(delta == 1)(lambda: tile(bp_ref))
    pl.when(delta == 0)(lambda: tile(b0_ref))
    pl.when(delta == -1)(lambda: tile(bm_ref))
    pl.when(delta < -1)(lambda: tile(None))

    @pl.when(delta == 1)
    def _():
        for h in range(A_HEADS):
            hs = slice(h * A_DIM, (h + 1) * A_DIM)
            o_ref[:, hs] = (acc_ref[hs, :] / l_ref[h]).T.astype(o_ref.dtype)


def _t5_bucket_np(rel):
    rel = np.maximum(rel, 0)
    max_exact = REL_BUCKETS // 2
    rel_f = np.maximum(rel, 1).astype(np.float32)
    large = max_exact + (np.log(rel_f / np.float32(max_exact)) / np.float32(math.log(REL_MAX_DIST / max_exact))
                         * np.float32(REL_BUCKETS - max_exact)).astype(np.int32)
    large = np.minimum(large, REL_BUCKETS - 1)
    return np.where(rel < max_exact, rel, large)


def attention(q_t, proj, v_t, mask_t, rel_bias):
    lp = proj.shape[0]
    tq, tk = Q_TILE, ROW_TILE
    assert tq == 2 * tk and tk >= REL_MAX_DIST
    nq = lp // tq
    qi_l, kj_l = [], []
    for a in range(nq):
        for b in range(2 * a + 2):
            qi_l.append(a)
            kj_l.append(b)
    qi_arr = jnp.asarray(np.array(qi_l, np.int32))
    kj_arr = jnp.asarray(np.array(kj_l, np.int32))
    cc = np.arange(tk)[:, None]
    rr = np.arange(tq)[None, :]
    far = rel_bias[REL_BUCKETS - 1].astype(F32)
    btile = lambda d: jnp.transpose((rel_bias[_t5_bucket_np(rr - cc - tk * d)].astype(F32) - far) * LOG2E,
                                    (2, 0, 1))
    ck = COL_AK // A_W
    const3 = lambda s, qi, kj: (0, 0, 0)
    grid_spec = pltpu.PrefetchScalarGridSpec(
        num_scalar_prefetch=2,
        grid=(len(qi_l),),
        in_specs=[pl.BlockSpec((A_W, tq), lambda s, qi, kj: (0, qi[s])),
                  pl.BlockSpec((tk, A_W), lambda s, qi, kj: (kj[s], ck)),
                  pl.BlockSpec((A_W, tk), lambda s, qi, kj: (0, kj[s])),
                  pl.BlockSpec((tk, tq), lambda s, qi, kj: (kj[s], qi[s])),
                  pl.BlockSpec((A_HEADS, tk, tq), const3, pipeline_mode=pl.Buffered(1)),
                  pl.BlockSpec((A_HEADS, tk, tq), const3, pipeline_mode=pl.Buffered(1)),
                  pl.BlockSpec((A_HEADS, tk, tq), const3, pipeline_mode=pl.Buffered(1))],
        out_specs=pl.BlockSpec((tq, A_W), lambda s, qi, kj: (qi[s], 0)),
        scratch_shapes=[pltpu.VMEM((A_HEADS, 1, tq), F32),
                        pltpu.VMEM((A_HEADS, 1, tq), F32),
                        pltpu.VMEM((A_W, tq), F32),
                        pltpu.VMEM((tk, tq), F32)],
    )
    return pl.pallas_call(
        _attn_kernel,
        grid_spec=grid_spec,
        out_shape=jax.ShapeDtypeStruct((lp, A_W), BF16),
        compiler_params=_cparams(("arbitrary",)),
        name="attention",
    )(qi_arr, kj_arr, q_t, proj, v_t, mask_t, btile(1), btile(0), btile(-1))


def _merge_kernel(ym_ref, ya_ref, gm_ref, ga_ref, h_ref, wpm_ref, wpa_ref, wo_ref, g_ref, b_ref,
                  hn_ref, hb_ref, *, alpha):
    pm = jnp.dot(ym_ref[...], wpm_ref[...], preferred_element_type=F32)
    pa = jnp.dot(ya_ref[...], wpa_ref[...], preferred_element_type=F32)
    merged = _sigmoid(gm_ref[...].astype(F32)) * pm + _sigmoid(ga_ref[...].astype(F32)) * pa
    mix = jnp.dot(merged.astype(BF16), wo_ref[...], preferred_element_type=F32)
    y = _ln(alpha * h_ref[...] + mix, g_ref[...], b_ref[...])
    hn_ref[...] = y
    hb_ref[...] = y.astype(BF16)


def merge_out_ln(ym, ya, proj, h, wpm, wpa, wo, g, b, alpha):
    lp = h.shape[0]
    tm = ROW_TILE
    cgm, cga = COL_GM // D_MODEL, COL_GA // D_MODEL
    const = lambda i: (0, 0)
    return pl.pallas_call(
        functools.partial(_merge_kernel, alpha=alpha),
        grid=(lp // tm,),
        in_specs=[pl.BlockSpec((tm, M_V_W), lambda i: (i, 0)),
                  pl.BlockSpec((tm, A_W), lambda i: (i, 0)),
                  pl.BlockSpec((tm, D_MODEL), lambda i: (i, cgm)),
                  pl.BlockSpec((tm, D_MODEL), lambda i: (i, cga)),
                  pl.BlockSpec((tm, D_MODEL), lambda i: (i, 0)),
                  pl.BlockSpec((M_V_W, D_MODEL), const, pipeline_mode=pl.Buffered(1)),
                  pl.BlockSpec((A_W, D_MODEL), const, pipeline_mode=pl.Buffered(1)),
                  pl.BlockSpec((D_MODEL, D_MODEL), const, pipeline_mode=pl.Buffered(1)),
                  pl.BlockSpec((1, D_MODEL), const),
                  pl.BlockSpec((1, D_MODEL), const)],
        out_specs=[pl.BlockSpec((tm, D_MODEL), lambda i: (i, 0)),
                   pl.BlockSpec((tm, D_MODEL), lambda i: (i, 0))],
        out_shape=[jax.ShapeDtypeStruct((lp, D_MODEL), F32),
                   jax.ShapeDtypeStruct((lp, D_MODEL), BF16)],
        compiler_params=_cparams(("parallel",)),
        name="merge_out_ln",
    )(ym, ya, proj, proj, h, wpm, wpa, wo, g.reshape(1, -1), b.reshape(1, -1))


def _router_kernel(h_ref, w_ref, b_ref, e_ref, g_ref):
    lg = jnp.dot(h_ref[...], w_ref[...], preferred_element_type=F32, precision=HI) + b_ref[...]
    tm = lg.shape[0]
    col = lax.broadcasted_iota(jnp.int32, (tm, SM_W), 1).astype(F32)
    far = 1e9

    def first_argmax(x):
        mx = jnp.max(x, axis=-1, keepdims=True)
        return mx, jnp.min(jnp.where(x == mx, col, far), axis=-1, keepdims=True)

    gl = jnp.where(col < N_GROUPS, lg, -jnp.inf)
    gmax, gsel = first_argmax(gl)
    gprob = 1.0 / jnp.sum(jnp.exp(gl - gmax), axis=-1, keepdims=True)
    c0 = N_GROUPS + EXP_PER_GROUP * gsel
    el = jnp.where(jnp.logical_and(col >= c0, col < c0 + EXP_PER_GROUP), lg, -jnp.inf)
    v1, i1 = first_argmax(el)
    v2, i2 = first_argmax(jnp.where(col == i1, -jnp.inf, el))
    e21 = jnp.exp(v2 - v1)
    g1 = gprob / (1.0 + e21)
    g2 = gprob * e21 / (1.0 + e21)
    e_ref[...] = jnp.where(col == 0, i1 - N_GROUPS, jnp.where(col == 1, i2 - N_GROUPS, 0.0)).astype(jnp.int32)
    g_ref[...] = jnp.where(col == 0, g1, jnp.where(col == 1, g2, 0.0))


def router(h, w, b):
    lp = h.shape[0]
    tm = ROW_TILE
    return pl.pallas_call(
        _router_kernel,
        grid=(lp // tm,),
        in_specs=[pl.BlockSpec((tm, D_MODEL), lambda i: (i, 0)),
                  pl.BlockSpec((D_MODEL, SM_W), lambda i: (0, 0)),
                  pl.BlockSpec((1, SM_W), lambda i: (0, 0))],
        out_specs=[pl.BlockSpec((tm, SM_W), lambda i: (i, 0)),
                   pl.BlockSpec((tm, SM_W), lambda i: (i, 0))],
        out_shape=[jax.ShapeDtypeStruct((lp, SM_W), jnp.int32),
                   jax.ShapeDtypeStruct((lp, SM_W), F32)],
        compiler_params=_cparams(("parallel",)),
        name="router",
    )(h, w, b)


def _expert_kernel(blk_e_ref, tok_ref, slot_ref, nused_ref, x_hbm, w_ref, wgu_ref, wd_ref, y_hbm,
                   xbuf, ybuf, wgu_b, wd_b, gsem, ssem):
    b = pl.program_id(0)
    nb = pl.num_programs(0)
    nused = nused_ref[0]
    blk = MOE_BLOCK

    def gather_copy(block, r, slot):
        tok = tok_ref[block * blk + r]
        return pltpu.make_async_copy(x_hbm.at[pl.ds(tok, 1), :], xbuf.at[slot, pl.ds(r, 1), :], gsem.at[slot])

    def start_gather(block, slot):
        def body(r, _):
            gather_copy(block, r, slot).start()
            return 0
        lax.fori_loop(0, blk, body, 0)

    def wait_gather(block, slot):
        def body(r, _):
            gather_copy(block, r, slot).wait()
            return 0
        lax.fori_loop(0, blk, body, 0)

    def scatter_copy(block, r):
        dst = slot_ref[block * blk + r]
        return dst, pltpu.make_async_copy(ybuf.at[pl.ds(r, 1), :], y_hbm.at[pl.ds(jnp.maximum(dst, 0), 1), :], ssem.at[0])

    def start_scatter(block):
        def body(r, _):
            dst, cp = scatter_copy(block, r)

            @pl.when(dst >= 0)
            def _():
                cp.start()
            return 0
        lax.fori_loop(0, blk, body, 0)

    def wait_scatter(block):
        def body(r, _):
            dst, cp = scatter_copy(block, r)

            @pl.when(dst >= 0)
            def _():
                cp.wait()
            return 0
        lax.fori_loop(0, blk, body, 0)

    @pl.when(jnp.logical_and(b == 0, nused > 0))
    def _():
        start_gather(0, 0)

    @pl.when(b + 1 < nused)
    def _():
        start_gather(b + 1, (b + 1) % 2)

    @pl.when(b < nused)
    def _():
        changed = jnp.logical_or(b == 0, blk_e_ref[b] != blk_e_ref[jnp.maximum(b - 1, 0)])

        @pl.when(changed)
        def _():
            wgu_b[...] = wgu_ref[0].astype(BF16)
            wd_b[...] = wd_ref[0].astype(BF16)

        slot = b % 2
        wait_gather(b, slot)
        xb = xbuf[slot].astype(BF16)
        gu = jnp.dot(xb, wgu_b[...], preferred_element_type=F32)
        g = gu[:, :D_FF]
        hb = (g * _sigmoid(g)) * gu[:, D_FF:]
        y = jnp.dot(hb.astype(BF16), wd_b[...], preferred_element_type=F32) * w_ref[...]

        @pl.when(b > 0)
        def _():
            wait_scatter(b - 1)

        ybuf[...] = y
        start_scatter(b)

    @pl.when(jnp.logical_and(b < nused, jnp.logical_or(b == nb - 1, b + 1 >= nused)))
    def _():
        wait_scatter(b)


def expert_ffn(h, blk_e, row_tok, row_slot, nused, row_w, wgu, wd):
    lp = h.shape[0]
    n_assign = 2 * lp
    n_blocks = row_tok.shape[0] // MOE_BLOCK
    grid_spec = pltpu.PrefetchScalarGridSpec(
        num_scalar_prefetch=4,
        grid=(n_blocks,),
        in_specs=[pl.BlockSpec(memory_space=pl.ANY),
                  pl.BlockSpec((MOE_BLOCK, 1), lambda b, e, t, s, n: (b, 0)),
                  pl.BlockSpec((1, D_MODEL, 2 * D_FF), lambda b, e, t, s, n: (e[b], 0, 0)),
                  pl.BlockSpec((1, D_FF, D_MODEL), lambda b, e, t, s, n: (e[b], 0, 0))],
        out_specs=pl.BlockSpec(memory_space=pl.ANY),
        scratch_shapes=[pltpu.VMEM((2, MOE_BLOCK, D_MODEL), F32),
                        pltpu.VMEM((MOE_BLOCK, D_MODEL), F32),
                        pltpu.VMEM((D_MODEL, 2 * D_FF), BF16),
                        pltpu.VMEM((D_FF, D_MODEL), BF16),
                        pltpu.SemaphoreType.DMA((2,)),
                        pltpu.SemaphoreType.DMA((1,))],
    )
    return pl.pallas_call(
        _expert_kernel,
        grid_spec=grid_spec,
        out_shape=jax.ShapeDtypeStruct((n_assign, D_MODEL), F32),
        compiler_params=_cparams(("arbitrary",)),
        name="expert_ffn",
    )(blk_e, row_tok, row_slot, nused, h, row_w, wgu, wd)


def moe_dispatch(e_ids, gates):
    n_assign = e_ids.shape[0] * 2
    e_flat = e_ids.reshape(n_assign)
    w_flat = gates.reshape(n_assign)
    onehot = (e_flat[:, None] == jnp.arange(N_EXPERTS, dtype=jnp.int32)[None, :]).astype(jnp.int32)
    csum = jnp.cumsum(onehot, axis=0)
    rank = jnp.sum(onehot * csum, axis=1) - 1
    counts = csum[-1]
    padded = (counts + MOE_BLOCK - 1) // MOE_BLOCK * MOE_BLOCK
    pend = jnp.cumsum(padded)
    pstart = pend - padded
    dest = pstart[e_flat] + rank
    n_blocks = n_assign // MOE_BLOCK + N_EXPERTS
    n_rows = n_blocks * MOE_BLOCK
    row_slot = jnp.full((n_rows,), -1, jnp.int32).at[dest].set(jnp.arange(n_assign, dtype=jnp.int32))
    row_tok = jnp.maximum(row_slot, 0) >> 1
    row_w = jnp.where(row_slot >= 0, w_flat[jnp.maximum(row_slot, 0)], 0.0).reshape(n_rows, 1)
    blk_e = jnp.minimum(jnp.searchsorted(pend, jnp.arange(n_blocks, dtype=jnp.int32) * MOE_BLOCK, side='right'),
                        N_EXPERTS - 1).astype(jnp.int32)
    nused = (pend[-1] // MOE_BLOCK).astype(jnp.int32).reshape(1)
    return blk_e, row_tok, row_slot, nused, row_w


def _combine_kernel(h_ref, y_ref, g_ref, b_ref, hn_ref, hb_ref, *, alpha):
    ffn = y_ref[:, :D_MODEL] + y_ref[:, D_MODEL:]
    y = _ln(alpha * h_ref[...] + ffn, g_ref[...], b_ref[...])
    hn_ref[...] = y
    hb_ref[...] = y.astype(BF16)


def combine_ln(h, y2, g, b, alpha):
    lp = h.shape[0]
    tm = ROW_TILE
    return pl.pallas_call(
        functools.partial(_combine_kernel, alpha=alpha),
        grid=(lp // tm,),
        in_specs=[pl.BlockSpec((tm, D_MODEL), lambda i: (i, 0)),
                  pl.BlockSpec((tm, 2 * D_MODEL), lambda i: (i, 0)),
                  pl.BlockSpec((1, D_MODEL), lambda i: (0, 0)),
                  pl.BlockSpec((1, D_MODEL), lambda i: (0, 0))],
        out_specs=[pl.BlockSpec((tm, D_MODEL), lambda i: (i, 0)),
                   pl.BlockSpec((tm, D_MODEL), lambda i: (i, 0))],
        out_shape=[jax.ShapeDtypeStruct((lp, D_MODEL), F32),
                   jax.ShapeDtypeStruct((lp, D_MODEL), BF16)],
        compiler_params=_cparams(("parallel",)),
        name="combine_ln",
    )(h, y2.reshape(lp, 2 * D_MODEL), g.reshape(1, -1), b.reshape(1, -1))


def _split_w_in(w):
    off = {}
    o = 0
    for name, width in (("mq", M_QK_W), ("mk", M_QK_W), ("mv", M_V_W), ("mo", M_V_W), ("mi", M_HEADS),
                        ("mf", M_HEADS), ("aq", A_W), ("ak", A_W), ("av", A_W), ("iq", IDX_Q_W),
                        ("ik", IDX_DIM), ("iw", IDX_HEADS), ("gm", D_MODEL), ("ga", D_MODEL)):
        off[name] = (o, o + width)
        o += width
    seg = lambda n: w[:, off[n][0]:off[n][1]]
    big = jnp.concatenate([seg("gm"), seg("ga"), seg("mv"), seg("mo"), seg("aq") * (A_DIM ** -0.5 * LOG2E),
                           seg("ak"), seg("av"), seg("mq"), seg("mk"), seg("iq") * (IDX_DIM ** -0.5)],
                          axis=1).astype(BF16)
    small = jnp.concatenate([seg("mi"), seg("mf"), seg("iw") * (IDX_HEADS ** -0.5),
                             jnp.zeros((w.shape[0], SM_W - 2 * M_HEADS - IDX_HEADS - IDX_DIM), F32),
                             seg("ik")], axis=1)
    return big, small


def _layer(h, hb, p, rel_bias, topk, alpha):
    w_big, w_small = _split_w_in(p["w_in"])
    proj = in_proj_big(hb, w_big)
    small = in_proj_small(h, w_small)
    small_t = small[:, 0:16].T
    ki = small[:, SM_W - IDX_DIM:].astype(BF16)
    q_t = proj[:, COL_AQ:COL_AQ + A_W].T
    v_t = proj[:, COL_AV:COL_AV + A_W].T
    iq_t = proj[:, COL_IQ:COL_IQ + IDX_Q_W].T
    y_m = mlstm(proj, small, small_t, p["conv_w"], p["conv_b"], p["b_igate"], p["b_fgate"], p["mh_norm_g"])
    mask_t = index_mask(iq_t, small_t, ki, topk)
    y_a = attention(q_t, proj, v_t, mask_t, rel_bias)
    h, hb = merge_out_ln(y_m, y_a, proj, h, p["w_proj_m"].astype(BF16), p["w_proj_a"].astype(BF16),
                         p["w_out"].astype(BF16), p["ln1_g"], p["ln1_b"], alpha)
    w_r = jnp.concatenate([p["w_group"], p["w_router"],
                           jnp.zeros((D_MODEL, SM_W - N_GROUPS - N_EXPERTS), F32)], axis=1)
    b_r = jnp.concatenate([p["b_group"], p["b_router"],
                           jnp.zeros((SM_W - N_GROUPS - N_EXPERTS,), F32)]).reshape(1, SM_W)
    e_out, g_out = router(h, w_r, b_r)
    blk_e, row_tok, row_slot, nused, row_w = moe_dispatch(e_out[:, 0:2], g_out[:, 0:2])
    y2 = expert_ffn(h, blk_e, row_tok, row_slot, nused, row_w, p["w_gate_up"], p["w_down"])
    return combine_ln(h, y2, p["ln2_g"], p["ln2_b"], alpha)


def _trunk(x2, meta_tokens, ln_emb_g, ln_emb_b, rel_bias, layers):
    depth = len(layers)
    alpha = (2 * depth) ** 0.25
    seq = x2.shape[0]
    length = N_META + seq
    lp = -(-length // PAD_TO) * PAD_TO
    topk = min(TOPK_MAX, length // 4)
    xp = jnp.concatenate([meta_tokens.astype(x2.dtype), x2, jnp.zeros((lp - length, D_MODEL), x2.dtype)], axis=0)
    h, hb = embed_ln(xp, ln_emb_g, ln_emb_b)
    for p in layers:
        h, hb = _layer(h, hb, p, rel_bias, topk, alpha)
    return h[N_META:length]


def kernel(x, meta_tokens, ln_emb_g, ln_emb_b, rel_bias, w_in, conv_w, conv_b, b_igate, b_fgate, mh_norm_g,
           w_proj_m, w_proj_a, w_out, ln1_g, ln1_b, w_group, b_group, w_router, b_router, w_gate_up, w_down,
           ln2_g, ln2_b):
    depth = w_in.shape[0]
    layers = [dict(w_in=w_in[l], conv_w=conv_w[l], conv_b=conv_b[l], b_igate=b_igate[l], b_fgate=b_fgate[l],
                   mh_norm_g=mh_norm_g[l], w_proj_m=w_proj_m[l], w_proj_a=w_proj_a[l], w_out=w_out[l],
                   ln1_g=ln1_g[l], ln1_b=ln1_b[l], w_group=w_group[l], b_group=b_group[l],
                   w_router=w_router[l], b_router=b_router[l], w_gate_up=w_gate_up[l], w_down=w_down[l],
                   ln2_g=ln2_g[l], ln2_b=ln2_b[l]) for l in range(depth)]
    outs = [_trunk(x[b], meta_tokens, ln_emb_g, ln_emb_b, rel_bias, layers) for b in range(x.shape[0])]
    return jnp.stack(outs, axis=0)
```

```python
import functools
import math

import jax
import jax.numpy as jnp
import numpy as np
from jax import lax
from jax.experimental import pallas as pl
from jax.experimental.pallas import tpu as pltpu

D_MODEL = 2048
N_META = 16
M_HEADS = 4
M_QK = 128
M_V = 256
CONV_W = 4
A_HEADS = 8
A_DIM = 128
TOPK_MAX = 256
IDX_HEADS = 8
IDX_DIM = 64
REL_BUCKETS = 32
REL_MAX_DIST = 128
N_GROUPS = 4
EXP_PER_GROUP = 8
N_EXPERTS = 32
D_FF = 512
MOE_BLOCK = 128
LN_EPS = 1e-5
LOG2E = math.log2(math.e)
NEG = -1e30
BIG = 1e30

M_QK_W = M_HEADS * M_QK
M_V_W = M_HEADS * M_V
A_W = A_HEADS * A_DIM
IDX_Q_W = IDX_HEADS * IDX_DIM

COL_GM = 0
COL_GA = 2048
COL_MV = 4096
COL_MO = 5120
COL_AQ = 6144
COL_AK = 7168
COL_AV = 8192
COL_MQK = 9216
COL_IQ = 10240
BIG_W = 10752
SM_W = 128

ROW_TILE = 256
Q_TILE = 512
PAD_TO = 512
F32 = jnp.float32
BF16 = jnp.bfloat16
VMEM_LIMIT = 56 * 1024 * 1024
HI = lax.Precision.HIGHEST


def _cparams(sem):
    return pltpu.CompilerParams(dimension_semantics=sem, vmem_limit_bytes=VMEM_LIMIT)


def _pick(n, cands):
    for c in cands:
        if n % c == 0:
            return c
    raise ValueError(f"no tile for {n}")


def _ln(x, g, b):
    mu = jnp.mean(x, axis=-1, keepdims=True)
    xc = x - mu
    var = jnp.mean(xc * xc, axis=-1, keepdims=True)
    return xc * lax.rsqrt(var + LN_EPS) * g + b


def _sigmoid(x):
    return 1.0 / (1.0 + jnp.exp(-x))


def _embed_ln_kernel(x_ref, g_ref, b_ref, h_ref, hb_ref):
    y = _ln(x_ref[...], g_ref[...], b_ref[...])
    h_ref[...] = y
    hb_ref[...] = y.astype(BF16)


def embed_ln(xp, g, b):
    lp = xp.shape[0]
    tm = ROW_TILE
    return pl.pallas_call(
        _embed_ln_kernel,
        grid=(lp // tm,),
        in_specs=[pl.BlockSpec((tm, D_MODEL), lambda i: (i, 0)),
                  pl.BlockSpec((1, D_MODEL), lambda i: (0, 0)),
                  pl.BlockSpec((1, D_MODEL), lambda i: (0, 0))],
        out_specs=[pl.BlockSpec((tm, D_MODEL), lambda i: (i, 0)),
                   pl.BlockSpec((tm, D_MODEL), lambda i: (i, 0))],
        out_shape=[jax.ShapeDtypeStruct((lp, D_MODEL), F32),
                   jax.ShapeDtypeStruct((lp, D_MODEL), BF16)],
        compiler_params=_cparams(("parallel",)),
        name="embed_ln",
    )(xp, g.reshape(1, -1), b.reshape(1, -1))


def _mm_kernel(a_ref, b_ref, o_ref):
    o_ref[...] = jnp.dot(a_ref[...], b_ref[...], preferred_element_type=F32).astype(o_ref.dtype)


def in_proj_big(hb, w):
    lp = hb.shape[0]
    tm = _pick(lp, (1536, 1280, 1024, 768, 512))
    tn = 1536
    return pl.pallas_call(
        _mm_kernel,
        grid=(BIG_W // tn, lp // tm),
        in_specs=[pl.BlockSpec((tm, D_MODEL), lambda j, i: (i, 0)),
                  pl.BlockSpec((D_MODEL, tn), lambda j, i: (0, j))],
        out_specs=pl.BlockSpec((tm, tn), lambda j, i: (i, j)),
        out_shape=jax.ShapeDtypeStruct((lp, BIG_W), BF16),
        compiler_params=_cparams(("parallel", "parallel")),
        name="in_proj_big",
    )(hb, w)


def _mm_hi_kernel(a_ref, b_ref, o_ref):
    o_ref[...] = jnp.dot(a_ref[...], b_ref[...], preferred_element_type=F32, precision=HI)


def in_proj_small(h, w):
    lp = h.shape[0]
    tm = ROW_TILE
    return pl.pallas_call(
        _mm_hi_kernel,
        grid=(lp // tm,),
        in_specs=[pl.BlockSpec((tm, D_MODEL), lambda i: (i, 0)),
                  pl.BlockSpec((D_MODEL, SM_W), lambda i: (0, 0))],
        out_specs=pl.BlockSpec((tm, SM_W), lambda i: (i, 0)),
        out_shape=jax.ShapeDtypeStruct((lp, SM_W), F32),
        compiler_params=_cparams(("parallel",)),
        name="in_proj_small",
    )(h, w)


def _log_sigmoid(x):
    return jnp.minimum(x, 0.0) - jnp.log1p(jnp.exp(-jnp.abs(x)))


def _mlstm_kernel(qk_ref, v_ref, o_ref, sm_ref, smt_ref, cw_ref, cb_ref, gb_ref, gbt_ref, mhg_ref,
                  y_ref, ubuf, c_ref, n_ref, m_ref):
    t = ROW_TILE
    step = pl.program_id(0)

    @pl.when(step == 0)
    def _():
        ubuf[0:8, :] = jnp.zeros((8, 2 * M_QK_W), F32)
        c_ref[...] = jnp.zeros_like(c_ref)
        n_ref[...] = jnp.zeros_like(n_ref)
        m_ref[...] = jnp.zeros_like(m_ref)

    ubuf[8:8 + t, :] = qk_ref[...].astype(F32)
    conv = cb_ref[...]
    for j in range(CONV_W):
        d = CONV_W - 1 - j
        conv = conv + cw_ref[j:j + 1, :] * ubuf[8 - d:8 - d + t, :]
    ubuf[0:8, :] = ubuf[t:t + 8, :]
    qk = conv * _sigmoid(conv)

    gcol = sm_ref[...] + gb_ref[...]
    grow = smt_ref[0:8, :] + gbt_ref[...]
    lane = lax.broadcasted_iota(jnp.int32, (t, SM_W), 1)
    gcol = jnp.where(lane < M_HEADS, gcol, _log_sigmoid(gcol))
    srow = lax.broadcasted_iota(jnp.int32, (8, t), 0)
    grow = jnp.where(srow < M_HEADS, grow, _log_sigmoid(grow))
    ri = lax.broadcasted_iota(jnp.int32, (t, t), 0)
    ci = lax.broadcasted_iota(jnp.int32, (t, t), 1)
    causal = ci <= ri
    tri = causal.astype(F32)
    bcol = jnp.dot(tri, gcol, preferred_element_type=F32, precision=HI)
    brow = lax.dot_general(grow, tri, (((1,), (1,)), ((), ())), preferred_element_type=F32,
                           precision=HI)

    v_all = v_ref[...]
    sig_o = _sigmoid(o_ref[...].astype(F32))
    for h in range(M_HEADS):
        q = qk[:, h * M_QK:(h + 1) * M_QK] * (M_QK ** -0.5)
        k = qk[:, M_QK_W + h * M_QK:M_QK_W + (h + 1) * M_QK]
        v = v_all[:, h * M_V:(h + 1) * M_V]
        qb = q.astype(BF16)
        kb = k.astype(BF16)
        b_c = bcol[:, M_HEADS + h:M_HEADS + h + 1]
        ig_c = gcol[:, h:h + 1]
        b_r = brow[M_HEADS + h:M_HEADS + h + 1, :]
        ig_r = grow[h:h + 1, :]
        m_prev = m_ref[h, 0:1, 0:1]
        dmat = jnp.where(causal, b_c + (ig_r - b_r), NEG)
        inter = b_c + m_prev
        m_t = jnp.maximum(jnp.max(dmat, axis=-1, keepdims=True), inter)
        s = lax.dot_general(qb, kb, (((1,), (1,)), ((), ())), preferred_element_type=F32)
        w_intra = jnp.exp(dmat - m_t) * s
        w_inter = jnp.exp(inter - m_t)
        c_old = c_ref[h]
        n_old = n_ref[h, 0:1, :]
        num = (jnp.dot(w_intra.astype(BF16), v, preferred_element_type=F32)
               + w_inter * jnp.dot(qb, c_old.astype(BF16), preferred_element_type=F32))
        den = (jnp.sum(w_intra, axis=-1, keepdims=True)
               + w_inter * jnp.sum(q * n_old, axis=-1, keepdims=True))
        hh = num / jnp.maximum(jnp.abs(den), jnp.exp(-m_t))
        mu = jnp.mean(hh, axis=-1, keepdims=True)
        hc = hh - mu
        var = jnp.mean(hc * hc, axis=-1, keepdims=True)
        hn = hc * lax.rsqrt(var + LN_EPS) * mhg_ref[:, h * M_V:(h + 1) * M_V]
        y_ref[:, h * M_V:(h + 1) * M_V] = (sig_o[:, h * M_V:(h + 1) * M_V] * hn).astype(y_ref.dtype)
        b_last = b_c[t - 1:t, :]
        dec = b_last - b_c + ig_c
        m_new = jnp.maximum(b_last + m_prev, jnp.max(dec, axis=0, keepdims=True))
        w_src = jnp.exp(dec - m_new)
        cs = jnp.exp(b_last + m_prev - m_new)
        wv = (w_src * v.astype(F32)).astype(BF16)
        c_ref[h] = cs * c_old + jnp.dot(k.T.astype(BF16), wv, preferred_element_type=F32)
        n_new = cs * n_old + jnp.sum(w_src * k, axis=0, keepdims=True)
        n_ref[h] = jnp.broadcast_to(n_new, (8, M_QK))
        m_ref[h] = jnp.broadcast_to(m_new, (8, 128))


def mlstm(proj, small, small_t, conv_w, conv_b, b_ig, b_fg, mh_g):
    lp = proj.shape[0]
    t = ROW_TILE
    gb = jnp.zeros((1, SM_W), F32).at[0, 0:M_HEADS].set(b_ig).at[0, M_HEADS:2 * M_HEADS].set(b_fg)
    gbt = gb[0, 0:8].reshape(8, 1)
    cq = COL_MQK // (2 * M_QK_W)
    cv = COL_MV // M_V_W
    co = COL_MO // M_V_W
    return pl.pallas_call(
        _mlstm_kernel,
        grid=(lp // t,),
        in_specs=[pl.BlockSpec((t, 2 * M_QK_W), lambda i: (i, cq)),
                  pl.BlockSpec((t, M_V_W), lambda i: (i, cv)),
                  pl.BlockSpec((t, M_V_W), lambda i: (i, co)),
                  pl.BlockSpec((t, SM_W), lambda i: (i, 0)),
                  pl.BlockSpec((16, t), lambda i: (0, i)),
                  pl.BlockSpec((CONV_W, 2 * M_QK_W), lambda i: (0, 0)),
                  pl.BlockSpec((1, 2 * M_QK_W), lambda i: (0, 0)),
                  pl.BlockSpec((1, SM_W), lambda i: (0, 0)),
                  pl.BlockSpec((8, 1), lambda i: (0, 0)),
                  pl.BlockSpec((1, M_V_W), lambda i: (0, 0))],
        out_specs=pl.BlockSpec((t, M_V_W), lambda i: (i, 0)),
        out_shape=jax.ShapeDtypeStruct((lp, M_V_W), BF16),
        scratch_shapes=[pltpu.VMEM((t + 8, 2 * M_QK_W), F32),
                        pltpu.VMEM((M_HEADS, M_QK, M_V), F32),
                        pltpu.VMEM((M_HEADS, 8, M_QK), F32),
                        pltpu.VMEM((M_HEADS, 8, 128), F32)],
        compiler_params=_cparams(("arbitrary",)),
        name="mlstm",
    )(proj, proj, proj, small, small_t, conv_w, conv_b.reshape(1, -1), gb, gbt, mh_g.reshape(1, -1))


def _f2key(f):
    b = pltpu.bitcast(f, jnp.int32)
    return jnp.where(b >= 0, b, b ^ jnp.int32(0x7FFFFFFF))


def _key2f(k):
    return pltpu.bitcast(jnp.where(k >= 0, k, k ^ jnp.int32(0x7FFFFFFF)), F32)


IDX_MAXIT = 128
IDX_VALUE_BISECT_ITERS = 48
KEY_MIN_NORMAL = 0x00800000
IDX_TQ = 256


def _col_reduce(x, op):
    n = x.shape[0] // 8
    parts = [x[g * 8:(g + 1) * 8, :] for g in range(n)]
    while len(parts) > 1:
        nxt = [op(parts[a], parts[a + 1]) for a in range(0, len(parts) - 1, 2)]
        if len(parts) % 2:
            nxt.append(parts[-1])
        parts = nxt
    return parts[0]


def _index_mask_kernel(iqt_ref, wt_ref, ki_ref, mask_ref, s_ref, *, topk):
    tq = IDX_TQ
    tk = ROW_TILE
    i = pl.program_id(0)
    nk = i + 1
    nkt = pl.num_programs(0)
    q_pos = i * tq + lax.broadcasted_iota(jnp.int32, (1, tq), 1)
    k_iota = lax.broadcasted_iota(jnp.int32, (tk, 1), 0)

    def raw_scores(kj):
        kt = ki_ref[pl.ds(pl.multiple_of(kj * tk, tk), tk), :]
        acc = jnp.zeros((tk, tq), F32)
        for h in range(IDX_HEADS):
            x = jnp.dot(kt, iqt_ref[h * IDX_DIM:(h + 1) * IDX_DIM, :], preferred_element_type=F32)
            acc = acc + wt_ref[8 + h:9 + h, :] * jnp.maximum(x, 0.0)
        return acc

    def edge_tile(kj, vmin, vmax):
        acc = raw_scores(kj)
        key_pos = kj * tk + k_iota
        visible = key_pos <= q_pos
        ordinary = jnp.logical_and(visible, key_pos >= N_META)
        s_ref[pl.ds(pl.multiple_of(kj * tk, tk), tk), :] = jnp.where(
            visible, jnp.where(key_pos < N_META, BIG, acc), NEG)
        vmin = jnp.minimum(vmin, _col_reduce(jnp.where(ordinary, acc, BIG), jnp.minimum))
        vmax = jnp.maximum(vmax, _col_reduce(jnp.where(ordinary, acc, NEG), jnp.maximum))
        return vmin, vmax

    def inner_tiles(kjs, vmin, vmax):
        accs = [raw_scores(kj) for kj in kjs]
        for kj, acc in zip(kjs, accs):
            s_ref[pl.ds(pl.multiple_of(kj * tk, tk), tk), :] = acc
            vmin = jnp.minimum(vmin, _col_reduce(acc, jnp.minimum))
            vmax = jnp.maximum(vmax, _col_reduce(acc, jnp.maximum))
        return vmin, vmax

    n_inner = jnp.maximum(i - 1, 0)
    vmin, vmax = edge_tile(0, jnp.full((8, tq), BIG, F32), jnp.full((8, tq), NEG, F32))
    vmin, vmax = lax.fori_loop(0, n_inner // 2, lambda p, c: inner_tiles([1 + 2 * p, 2 + 2 * p], *c),
                               (vmin, vmax))
    vmin, vmax = lax.cond(n_inner % 2 == 1, lambda a, b: inner_tiles([i - 1], a, b), lambda a, b: (a, b),
                          vmin, vmax)
    vmin, vmax = lax.cond(i > 0, lambda a, b: edge_tile(i, a, b), lambda a, b: (a, b), vmin, vmax)
    vmin = jnp.min(vmin, axis=0, keepdims=True)
    vmax = jnp.max(vmax, axis=0, keepdims=True)

    def count(x, strict=False):
        xb = jnp.broadcast_to(x, (8, tq))

        def body(kj, acc):
            tile = s_ref[pl.ds(pl.multiple_of(kj * tk, tk), tk), :]
            n = tk // 8
            hits = []
            for g in range(n):
                row = tile[g * 8:(g + 1) * 8, :]
                hit = (row > xb) if strict else (row >= xb)
                hits.append(jnp.where(hit, 1.0, 0.0))
            while len(hits) > 1:
                hits = [hits[a] + hits[a + 1] for a in range(0, len(hits), 2)]
            return acc + hits[0]

        acc = lax.fori_loop(0, nk, body, jnp.zeros((8, tq), F32))
        return jnp.sum(acc, axis=0, keepdims=True)

    kf = float(topk)
    n_vis = (q_pos + 1).astype(F32)
    n_meta = jnp.minimum(n_vis, float(N_META))
    few = n_vis <= kf
    lo_k0 = _f2key(vmin)
    hi_k0 = _f2key(vmax) + 1
    done0 = jnp.logical_or(few, n_vis - n_meta < 0.5)
    thr0 = jnp.full((1, tq), 0.5 * NEG, F32)

    def cond(st):
        it, lo_k, hi_k, clo, chi, thr, done_i, cthr = st
        return jnp.logical_and(it < IDX_MAXIT, jnp.min(done_i) < 1)

    def step(st):
        it, lo_k, hi_k, clo, chi, thr, done_i, cthr = st
        done = done_i > 0
        lo_v = _key2f(lo_k)
        hi_v = _key2f(hi_k)
        frac = jnp.log(clo / (kf + 0.5)) / jnp.log(clo / jnp.maximum(chi, 0.5))
        frac = jnp.clip(frac, 0.0, 1.0)
        pk_i = _f2key(lo_v + (hi_v - lo_v) * frac)
        pk_m = jnp.where(it < IDX_VALUE_BISECT_ITERS, _f2key(lo_v + 0.5 * (hi_v - lo_v)),
                         (lo_k >> 1) + (hi_k >> 1) + (lo_k & hi_k & 1))
        pk_m = jnp.where(jnp.logical_and(lo_k == 0, hi_k > KEY_MIN_NORMAL), KEY_MIN_NORMAL, pk_m)
        pk_m = jnp.where(jnp.logical_and(lo_k < 0, hi_k > 0), 0, pk_m)
        pk = jnp.where(it % 2 == 0, pk_i, pk_m)
        pk = jnp.minimum(jnp.maximum(pk, lo_k + 1), hi_k - 1)
        pk = jnp.where(done, lo_k, pk)
        x = _key2f(pk)
        c = count(x)
        ge = c >= kf
        act = jnp.logical_not(done)
        up = jnp.logical_and(act, ge)
        dn = jnp.logical_and(act, jnp.logical_not(ge))
        lo_k = jnp.where(up, pk, lo_k)
        clo = jnp.where(up, c, clo)
        hi_k = jnp.where(dn, pk, hi_k)
        chi = jnp.where(dn, c, chi)
        exact = jnp.logical_and(act, c == kf)
        collapsed = jnp.logical_and(act, jnp.logical_or(
            hi_k <= lo_k + 1, jnp.logical_and(lo_k == 0, hi_k <= KEY_MIN_NORMAL)))
        fin = jnp.logical_or(exact, collapsed)
        thr = jnp.where(exact, x, jnp.where(collapsed, _key2f(lo_k), thr))
        cthr = jnp.where(exact, c, jnp.where(collapsed, clo, cthr))
        return it + 1, lo_k, hi_k, clo, chi, thr, jnp.where(jnp.logical_or(done, fin), 1, 0), cthr

    col0 = jnp.logical_and(jnp.logical_not(done0), hi_k0 <= lo_k0 + 1)
    thr0 = jnp.where(col0, vmin, thr0)
    st = (jnp.int32(0), lo_k0, hi_k0, n_vis, n_meta, thr0, jnp.where(jnp.logical_or(done0, col0), 1, 0),
          jnp.where(col0, n_vis, kf))
    _, _, _, _, _, thr, _, cthr = lax.while_loop(cond, step, st)
    thr = jnp.where(few, 0.5 * NEG, thr)
    any_tie = jnp.max(jnp.where(jnp.logical_and(jnp.logical_not(few), cthr > kf), 1.0, 0.0)) > 0.5

    @pl.when(jnp.logical_not(any_tie))
    def _():
        def emit(kj, _):
            s = s_ref[pl.ds(pl.multiple_of(kj * tk, tk), tk), :]
            mask_ref[pl.ds(pl.multiple_of(kj * tk, tk), tk), :] = jnp.where(s >= thr, 1, 0).astype(jnp.int8)
            return 0
        lax.fori_loop(0, nk, emit, 0)

    @pl.when(any_tie)
    def _():
        allowed = kf - count(thr, strict=True)
        ri = lax.broadcasted_iota(jnp.int32, (tk, tk), 0)
        ci = lax.broadcasted_iota(jnp.int32, (tk, tk), 1)
        before = (ci < ri).astype(BF16)

        def emit(kjs, run):
            tiles = [s_ref[pl.ds(pl.multiple_of(kj * tk, tk), tk), :] for kj in kjs]
            eqfs = [jnp.where(s == thr, 1.0, 0.0) for s in tiles]
            prefix = [jnp.dot(before, e.astype(BF16), preferred_element_type=F32) for e in eqfs]
            for kj, s, e, pre in zip(kjs, tiles, eqfs, prefix):
                sel = jnp.logical_or(s > thr, jnp.logical_and(e > 0.5, run + pre < allowed))
                mask_ref[pl.ds(pl.multiple_of(kj * tk, tk), tk), :] = jnp.where(sel, 1, 0).astype(jnp.int8)
                run = run + jnp.sum(e, axis=0, keepdims=True)
            return run

        run = lax.fori_loop(0, nk // 2, lambda p, r: emit([2 * p, 2 * p + 1], r), jnp.zeros((1, tq), F32))

        @pl.when(nk % 2 == 1)
        def _():
            emit([nk - 1], run)

    def clear(kj, _):
        mask_ref[pl.ds(pl.multiple_of(kj * tk, tk), tk), :] = jnp.zeros((tk, tq), jnp.int8)
        return 0
    lax.fori_loop(nk, nkt, clear, 0)


def index_mask(iq_t, small_t, ki, topk):
    lp = iq_t.shape[1]
    tq = IDX_TQ
    assert IDX_TQ == ROW_TILE
    return pl.pallas_call(
        functools.partial(_index_mask_kernel, topk=topk),
        grid=(lp // tq,),
        in_specs=[pl.BlockSpec((IDX_Q_W, tq), lambda i: (0, i)),
                  pl.BlockSpec((16, tq), lambda i: (0, i)),
                  pl.BlockSpec((lp, IDX_DIM), lambda i: (0, 0), pipeline_mode=pl.Buffered(1))],
        out_specs=pl.BlockSpec((lp, tq), lambda i: (0, i)),
        out_shape=jax.ShapeDtypeStruct((lp, lp), jnp.int8),
        scratch_shapes=[pltpu.VMEM((lp, tq), F32)],
        compiler_params=_cparams(("arbitrary",)),
        name="index_mask",
    )(iq_t, small_t, ki)


ATTN_AHEAD = 2
ONES_ROWS = 16


def _attn_kernel(qi_ref, kj_ref, qt_ref, k_ref, vt_ref, mask_ref, bp_ref, b0_ref, bm_ref, o_ref,
                 m_ref, l_ref, acc_ref, neg_ref):
    tq, tk = Q_TILE, ROW_TILE
    sidx = pl.program_id(0)
    qi = qi_ref[sidx]
    kj = kj_ref[sidx]
    delta = kj - 2 * qi

    @pl.when(kj == 0)
    def _():
        m_ref[...] = jnp.full(m_ref.shape, NEG, F32)
        l_ref[...] = jnp.zeros(l_ref.shape, F32)
        acc_ref[...] = jnp.zeros(acc_ref.shape, F32)

    def tile(bias_ref):
        neg_ref[...] = jnp.where(mask_ref[...].astype(F32) > 0.5, 0.0, NEG)
        ones = jnp.ones((ONES_ROWS, tk), BF16)

        def logits(h):
            hs = slice(h * A_DIM, (h + 1) * A_DIM)
            s = jnp.dot(k_ref[:, hs], qt_ref[hs, :], preferred_element_type=F32) + neg_ref[...]
            if bias_ref is not None:
                s = s + bias_ref[h]
            return s

        ahead = [logits(h) for h in range(min(ATTN_AHEAD, A_HEADS))]
        for h in range(A_HEADS):
            hs = slice(h * A_DIM, (h + 1) * A_DIM)
            if h + ATTN_AHEAD < A_HEADS:
                ahead.append(logits(h + ATTN_AHEAD))
            s = ahead.pop(0)
            m_old = m_ref[h]
            m_new = jnp.maximum(m_old, jnp.max(s, axis=0, keepdims=True))
            alpha = jnp.exp2(m_old - m_new)
            p = jnp.exp2(s - m_new).astype(BF16)
            pv = jnp.dot(jnp.concatenate([vt_ref[hs, :], ones], axis=0), p, preferred_element_type=F32)
            acc_ref[hs, :] = alpha * acc_ref[hs, :] + pv[0:A_DIM, :]
            l_ref[h] = alpha * l_ref[h] + pv[A_DIM:A_DIM + 1, :]
            m_ref[h] = m_new

    pl.when(delta == 1)(lambda: tile(bp_ref))
    pl.when(delta == 0)(lambda: tile(b0_ref))
    pl.when(delta == -1)(lambda: tile(bm_ref))
    pl.when(delta < -1)(lambda: tile(None))

    @pl.when(delta == 1)
    def _():
        for h in range(A_HEADS):
            hs = slice(h * A_DIM, (h + 1) * A_DIM)
            o_ref[:, hs] = (acc_ref[hs, :] / l_ref[h]).T.astype(o_ref.dtype)


def _t5_bucket_np(rel):
    rel = np.maximum(rel, 0)
    max_exact = REL_BUCKETS // 2
    rel_f = np.maximum(rel, 1).astype(np.float32)
    large = max_exact + (np.log(rel_f / np.float32(max_exact)) / np.float32(math.log(REL_MAX_DIST / max_exact))
                         * np.float32(REL_BUCKETS - max_exact)).astype(np.int32)
    large = np.minimum(large, REL_BUCKETS - 1)
    return np.where(rel < max_exact, rel, large)


def attention(q_t, proj, v_t, mask_t, rel_bias):
    lp = proj.shape[0]
    tq, tk = Q_TILE, ROW_TILE
    assert tq == 2 * tk and tk >= REL_MAX_DIST
    nq = lp // tq
    qi_l, kj_l = [], []
    for a in range(nq):
        for b in range(2 * a + 2):
            qi_l.append(a)
            kj_l.append(b)
    qi_arr = jnp.asarray(np.array(qi_l, np.int32))
    kj_arr = jnp.asarray(np.array(kj_l, np.int32))
    table = (rel_bias.astype(F32) - rel_bias[REL_BUCKETS - 1].astype(F32)) * LOG2E

    def btile(d):
        n = tq + tk
        j = np.arange(n)
        diag = np.where(j < tq, j, j - n)
        vec = jnp.take(table, jnp.asarray(_t5_bucket_np(diag - tk * d).astype(np.int32)), axis=0).T
        return jnp.tile(vec, (1, tk))[:, :tk * (n - 1)].reshape(A_HEADS, tk, n - 1)[:, :, :tq]

    ck = COL_AK // A_W
    const3 = lambda s, qi, kj: (0, 0, 0)
    grid_spec = pltpu.PrefetchScalarGridSpec(
        num_scalar_prefetch=2,
        grid=(len(qi_l),),
        in_specs=[pl.BlockSpec((A_W, tq), lambda s, qi, kj: (0, qi[s])),
                  pl.BlockSpec((tk, A_W), lambda s, qi, kj: (kj[s], ck)),
                  pl.BlockSpec((A_W, tk), lambda s, qi, kj: (0, kj[s])),
                  pl.BlockSpec((tk, tq), lambda s, qi, kj: (kj[s], qi[s])),
                  pl.BlockSpec((A_HEADS, tk, tq), const3, pipeline_mode=pl.Buffered(1)),
                  pl.BlockSpec((A_HEADS, tk, tq), const3, pipeline_mode=pl.Buffered(1)),
                  pl.BlockSpec((A_HEADS, tk, tq), const3, pipeline_mode=pl.Buffered(1))],
        out_specs=pl.BlockSpec((tq, A_W), lambda s, qi, kj: (qi[s], 0)),
        scratch_shapes=[pltpu.VMEM((A_HEADS, 1, tq), F32),
                        pltpu.VMEM((A_HEADS, 1, tq), F32),
                        pltpu.VMEM((A_W, tq), F32),
                        pltpu.VMEM((tk, tq), F32)],
    )
    return pl.pallas_call(
        _attn_kernel,
        grid_spec=grid_spec,
        out_shape=jax.ShapeDtypeStruct((lp, A_W), BF16),
        compiler_params=_cparams(("arbitrary",)),
        name="attention",
    )(qi_arr, kj_arr, q_t, proj, v_t, mask_t, btile(1), btile(0), btile(-1))


def _merge_kernel(ym_ref, ya_ref, gm_ref, ga_ref, h_ref, wpm_ref, wpa_ref, wo_ref, g_ref, b_ref,
                  hn_ref, hb_ref, *, alpha):
    pm = jnp.dot(ym_ref[...], wpm_ref[...], preferred_element_type=F32)
    pa = jnp.dot(ya_ref[...], wpa_ref[...], preferred_element_type=F32)
    merged = _sigmoid(gm_ref[...].astype(F32)) * pm + _sigmoid(ga_ref[...].astype(F32)) * pa
    mix = jnp.dot(merged.astype(BF16), wo_ref[...], preferred_element_type=F32)
    y = _ln(alpha * h_ref[...] + mix, g_ref[...], b_ref[...])
    hn_ref[...] = y
    hb_ref[...] = y.astype(BF16)


def merge_out_ln(ym, ya, proj, h, wpm, wpa, wo, g, b, alpha):
    lp = h.shape[0]
    tm = ROW_TILE
    cgm, cga = COL_GM // D_MODEL, COL_GA // D_MODEL
    const = lambda i: (0, 0)
    return pl.pallas_call(
        functools.partial(_merge_kernel, alpha=alpha),
        grid=(lp // tm,),
        in_specs=[pl.BlockSpec((tm, M_V_W), lambda i: (i, 0)),
                  pl.BlockSpec((tm, A_W), lambda i: (i, 0)),
                  pl.BlockSpec((tm, D_MODEL), lambda i: (i, cgm)),
                  pl.BlockSpec((tm, D_MODEL), lambda i: (i, cga)),
                  pl.BlockSpec((tm, D_MODEL), lambda i: (i, 0)),
                  pl.BlockSpec((M_V_W, D_MODEL), const, pipeline_mode=pl.Buffered(1)),
                  pl.BlockSpec((A_W, D_MODEL), const, pipeline_mode=pl.Buffered(1)),
                  pl.BlockSpec((D_MODEL, D_MODEL), const, pipeline_mode=pl.Buffered(1)),
                  pl.BlockSpec((1, D_MODEL), const),
                  pl.BlockSpec((1, D_MODEL), const)],
        out_specs=[pl.BlockSpec((tm, D_MODEL), lambda i: (i, 0)),
                   pl.BlockSpec((tm, D_MODEL), lambda i: (i, 0))],
        out_shape=[jax.ShapeDtypeStruct((lp, D_MODEL), F32),
                   jax.ShapeDtypeStruct((lp, D_MODEL), BF16)],
        compiler_params=_cparams(("parallel",)),
        name="merge_out_ln",
    )(ym, ya, proj, proj, h, wpm, wpa, wo, g.reshape(1, -1), b.reshape(1, -1))


def _router_kernel(h_ref, w_ref, b_ref, e_ref, g_ref):
    lg = jnp.dot(h_ref[...], w_ref[...], preferred_element_type=F32, precision=HI) + b_ref[...]
    tm = lg.shape[0]
    col = lax.broadcasted_iota(jnp.int32, (tm, SM_W), 1).astype(F32)
    far = 1e9

    def first_argmax(x):
        mx = jnp.max(x, axis=-1, keepdims=True)
        return mx, jnp.min(jnp.where(x == mx, col, far), axis=-1, keepdims=True)

    gl = jnp.where(col < N_GROUPS, lg, -jnp.inf)
    gmax, gsel = first_argmax(gl)
    gprob = 1.0 / jnp.sum(jnp.exp(gl - gmax), axis=-1, keepdims=True)
    c0 = N_GROUPS + EXP_PER_GROUP * gsel
    el = jnp.where(jnp.logical_and(col >= c0, col < c0 + EXP_PER_GROUP), lg, -jnp.inf)
    v1, i1 = first_argmax(el)
    v2, i2 = first_argmax(jnp.where(col == i1, -jnp.inf, el))
    e21 = jnp.exp(v2 - v1)
    g1 = gprob / (1.0 + e21)
    g2 = gprob * e21 / (1.0 + e21)
    e_ref[...] = jnp.where(col == 0, i1 - N_GROUPS, jnp.where(col == 1, i2 - N_GROUPS, 0.0)).astype(jnp.int32)
    g_ref[...] = jnp.where(col == 0, g1, jnp.where(col == 1, g2, 0.0))


def router(h, w, b):
    lp = h.shape[0]
    tm = ROW_TILE
    return pl.pallas_call(
        _router_kernel,
        grid=(lp // tm,),
        in_specs=[pl.BlockSpec((tm, D_MODEL), lambda i: (i, 0)),
                  pl.BlockSpec((D_MODEL, SM_W), lambda i: (0, 0)),
                  pl.BlockSpec((1, SM_W), lambda i: (0, 0))],
        out_specs=[pl.BlockSpec((tm, SM_W), lambda i: (i, 0)),
                   pl.BlockSpec((tm, SM_W), lambda i: (i, 0))],
        out_shape=[jax.ShapeDtypeStruct((lp, SM_W), jnp.int32),
                   jax.ShapeDtypeStruct((lp, SM_W), F32)],
        compiler_params=_cparams(("parallel",)),
        name="router",
    )(h, w, b)


def _expert_kernel(blk_e_ref, tok_ref, orow_ref, nused_ref, x_hbm, w_ref, wgu_ref, wd_ref, y_hbm,
                   xbuf, ybuf, wgu_b, wd_b, gsem, ssem):
    b = pl.program_id(0)
    nb = pl.num_programs(0)
    nused = nused_ref[0]
    blk = MOE_BLOCK

    def start_gather(block, slot):
        for r in range(blk):
            tok = tok_ref[block * blk + r]
            pltpu.make_async_copy(x_hbm.at[pl.ds(tok, 1), :], xbuf.at[slot, pl.ds(r, 1), :], gsem.at[slot]).start()

    def wait_gather(slot):
        pltpu.make_async_copy(x_hbm.at[pl.ds(0, blk), :], xbuf.at[slot], gsem.at[slot]).wait()

    def start_scatter(block, slot):
        for r in range(blk):
            dst = orow_ref[block * blk + r]
            pltpu.make_async_copy(ybuf.at[slot, pl.ds(r, 1), :], y_hbm.at[pl.ds(dst, 1), :], ssem.at[slot]).start()

    def wait_scatter(slot):
        pltpu.make_async_copy(ybuf.at[slot], y_hbm.at[pl.ds(0, blk), :], ssem.at[slot]).wait()

    @pl.when(jnp.logical_and(b == 0, nused > 0))
    def _():
        start_gather(0, 0)

    @pl.when(jnp.logical_and(b >= 2, b - 2 < nused))
    def _():
        wait_scatter(b % 2)

    @pl.when(b < nused)
    def _():
        changed = jnp.logical_or(b == 0, blk_e_ref[b] != blk_e_ref[jnp.maximum(b - 1, 0)])

        @pl.when(changed)
        def _():
            wgu_b[...] = wgu_ref[0].astype(BF16)
            wd_b[...] = wd_ref[0].astype(BF16)

    @pl.when(b < nused)
    def _():
        slot = b % 2
        wait_gather(slot)
        xb = xbuf[slot].astype(BF16)
        start_gather(b + 1, 1 - slot)
        gu = jnp.dot(xb, wgu_b[...], preferred_element_type=F32)
        g = gu[:, :D_FF]
        hb = (g * _sigmoid(g)) * gu[:, D_FF:]
        ybuf[slot] = jnp.dot(hb.astype(BF16), wd_b[...], preferred_element_type=F32) * w_ref[...]
        start_scatter(b, slot)

    @pl.when(jnp.logical_and(b == nused, nused > 0))
    def _():
        wait_gather(b % 2)

    @pl.when(b == nb - 1)
    def _():
        @pl.when(nused == nb)
        def _():
            wait_gather(nb % 2)

        @pl.when(nb - 2 < nused)
        def _():
            wait_scatter(nb % 2)

        @pl.when(nb - 1 < nused)
        def _():
            wait_scatter((nb - 1) % 2)


def expert_ffn(h, blk_e, row_tok, out_row, nused, row_w, wgu, wd):
    lp = h.shape[0]
    n_blocks = out_row.shape[0] // MOE_BLOCK
    assert row_tok.shape[0] == (n_blocks + 1) * MOE_BLOCK and n_blocks >= 2
    grid_spec = pltpu.PrefetchScalarGridSpec(
        num_scalar_prefetch=4,
        grid=(n_blocks,),
        in_specs=[pl.BlockSpec(memory_space=pl.ANY),
                  pl.BlockSpec((MOE_BLOCK, 1), lambda b, e, t, s, n: (b, 0)),
                  pl.BlockSpec((1, D_MODEL, 2 * D_FF), lambda b, e, t, s, n: (e[b], 0, 0)),
                  pl.BlockSpec((1, D_FF, D_MODEL), lambda b, e, t, s, n: (e[b], 0, 0))],
        out_specs=pl.BlockSpec(memory_space=pl.ANY),
        scratch_shapes=[pltpu.VMEM((2, MOE_BLOCK, D_MODEL), F32),
                        pltpu.VMEM((2, MOE_BLOCK, D_MODEL), F32),
                        pltpu.VMEM((D_MODEL, 2 * D_FF), BF16),
                        pltpu.VMEM((D_FF, D_MODEL), BF16),
                        pltpu.SemaphoreType.DMA((2,)),
                        pltpu.SemaphoreType.DMA((2,))],
    )
    return pl.pallas_call(
        _expert_kernel,
        grid_spec=grid_spec,
        out_shape=jax.ShapeDtypeStruct((2 * lp + MOE_BLOCK, D_MODEL), F32),
        compiler_params=_cparams(("arbitrary",)),
        name="expert_ffn",
    )(blk_e, row_tok, out_row, nused, h, row_w, wgu, wd)


def moe_dispatch(e_ids, gates):
    n_assign = e_ids.shape[0] * 2
    e_flat = e_ids.reshape(n_assign)
    w_flat = gates.reshape(n_assign)
    cb = ROW_TILE
    onehot = (e_flat[:, None] == jnp.arange(N_EXPERTS, dtype=jnp.int32)[None, :]).astype(F32)
    onehot = onehot.reshape(n_assign // cb, cb, N_EXPERTS)
    tri = jnp.asarray(np.tril(np.ones((cb, cb), np.float32)))
    within = jnp.einsum('ij,bjk->bik', tri, onehot, precision=HI)
    block_tot = jnp.sum(onehot, axis=1)
    block_off = jnp.cumsum(block_tot, axis=0) - block_tot
    csum = within + block_off[:, None, :]
    rank = (jnp.sum(onehot * csum, axis=2) - 1.0).astype(jnp.int32).reshape(n_assign)
    counts = jnp.sum(block_tot, axis=0).astype(jnp.int32)
    padded = (counts + MOE_BLOCK - 1) // MOE_BLOCK * MOE_BLOCK
    pend = jnp.cumsum(padded)
    pstart = pend - padded
    dest = pstart[e_flat] + rank
    n_blocks = n_assign // MOE_BLOCK + N_EXPERTS
    n_rows = n_blocks * MOE_BLOCK
    row_slot = jnp.full((n_rows,), -1, jnp.int32).at[dest].set(jnp.arange(n_assign, dtype=jnp.int32))
    valid = row_slot >= 0
    slot = jnp.maximum(row_slot, 0)
    n_tok = n_assign // 2
    row_tok = jnp.concatenate([slot >> 1, jnp.zeros((MOE_BLOCK,), jnp.int32)])
    out_row = jnp.where(valid, (slot & 1) * n_tok + (slot >> 1),
                        n_assign + jnp.arange(n_rows, dtype=jnp.int32) % MOE_BLOCK)
    row_w = jnp.where(valid, w_flat[slot], 0.0).reshape(n_rows, 1)
    blk_e = jnp.minimum(jnp.searchsorted(pend, jnp.arange(n_blocks, dtype=jnp.int32) * MOE_BLOCK, side='right'),
                        N_EXPERTS - 1).astype(jnp.int32)
    nused = (pend[-1] // MOE_BLOCK).astype(jnp.int32).reshape(1)
    return blk_e, row_tok, out_row, nused, row_w


def _combine_kernel(h_ref, y0_ref, y1_ref, g_ref, b_ref, hn_ref, hb_ref, *, alpha):
    ffn = y0_ref[...] + y1_ref[...]
    y = _ln(alpha * h_ref[...] + ffn, g_ref[...], b_ref[...])
    hn_ref[...] = y
    hb_ref[...] = y.astype(BF16)


def combine_ln(h, y2, g, b, alpha):
    lp = h.shape[0]
    tm = ROW_TILE
    half = lp // tm
    return pl.pallas_call(
        functools.partial(_combine_kernel, alpha=alpha),
        grid=(lp // tm,),
        in_specs=[pl.BlockSpec((tm, D_MODEL), lambda i: (i, 0)),
                  pl.BlockSpec((tm, D_MODEL), lambda i: (i, 0)),
                  pl.BlockSpec((tm, D_MODEL), lambda i: (i + half, 0)),
                  pl.BlockSpec((1, D_MODEL), lambda i: (0, 0)),
                  pl.BlockSpec((1, D_MODEL), lambda i: (0, 0))],
        out_specs=[pl.BlockSpec((tm, D_MODEL), lambda i: (i, 0)),
                   pl.BlockSpec((tm, D_MODEL), lambda i: (i, 0))],
        out_shape=[jax.ShapeDtypeStruct((lp, D_MODEL), F32),
                   jax.ShapeDtypeStruct((lp, D_MODEL), BF16)],
        compiler_params=_cparams(("parallel",)),
        name="combine_ln",
    )(h, y2, y2, g.reshape(1, -1), b.reshape(1, -1))


def _split_w_in(w):
    off = {}
    o = 0
    for name, width in (("mq", M_QK_W), ("mk", M_QK_W), ("mv", M_V_W), ("mo", M_V_W), ("mi", M_HEADS),
                        ("mf", M_HEADS), ("aq", A_W), ("ak", A_W), ("av", A_W), ("iq", IDX_Q_W),
                        ("ik", IDX_DIM), ("iw", IDX_HEADS), ("gm", D_MODEL), ("ga", D_MODEL)):
        off[name] = (o, o + width)
        o += width
    seg = lambda n: w[:, off[n][0]:off[n][1]]
    big = jnp.concatenate([seg("gm"), seg("ga"), seg("mv"), seg("mo"), seg("aq") * (A_DIM ** -0.5 * LOG2E),
                           seg("ak"), seg("av"), seg("mq"), seg("mk"), seg("iq") * (IDX_DIM ** -0.5)],
                          axis=1).astype(BF16)
    small = jnp.concatenate([seg("mi"), seg("mf"), seg("iw") * (IDX_HEADS ** -0.5),
                             jnp.zeros((w.shape[0], SM_W - 2 * M_HEADS - IDX_HEADS - IDX_DIM), F32),
                             seg("ik")], axis=1)
    return big, small


def _layer(h, hb, p, rel_bias, topk, alpha):
    w_big, w_small = _split_w_in(p["w_in"])
    proj = in_proj_big(hb, w_big)
    small = in_proj_small(h, w_small)
    small_t = small[:, 0:16].T
    ki = small[:, SM_W - IDX_DIM:].astype(BF16)
    q_t = proj[:, COL_AQ:COL_AQ + A_W].T
    v_t = proj[:, COL_AV:COL_AV + A_W].T
    iq_t = proj[:, COL_IQ:COL_IQ + IDX_Q_W].T
    y_m = mlstm(proj, small, small_t, p["conv_w"], p["conv_b"], p["b_igate"], p["b_fgate"], p["mh_norm_g"])
    mask_t = index_mask(iq_t, small_t, ki, topk)
    y_a = attention(q_t, proj, v_t, mask_t, rel_bias)
    h, hb = merge_out_ln(y_m, y_a, proj, h, p["w_proj_m"].astype(BF16), p["w_proj_a"].astype(BF16),
                         p["w_out"].astype(BF16), p["ln1_g"], p["ln1_b"], alpha)
    w_r = jnp.concatenate([p["w_group"], p["w_router"],
                           jnp.zeros((D_MODEL, SM_W - N_GROUPS - N_EXPERTS), F32)], axis=1)
    b_r = jnp.concatenate([p["b_group"], p["b_router"],
                           jnp.zeros((SM_W - N_GROUPS - N_EXPERTS,), F32)]).reshape(1, SM_W)
    e_out, g_out = router(h, w_r, b_r)
    blk_e, row_tok, out_row, nused, row_w = moe_dispatch(e_out[:, 0:2], g_out[:, 0:2])
    y2 = expert_ffn(h, blk_e, row_tok, out_row, nused, row_w, p["w_gate_up"], p["w_down"])
    return combine_ln(h, y2, p["ln2_g"], p["ln2_b"], alpha)


def _trunk(x2, meta_tokens, ln_emb_g, ln_emb_b, rel_bias, layers):
    depth = len(layers)
    alpha = (2 * depth) ** 0.25
    seq = x2.shape[0]
    length = N_META + seq
    lp = -(-length // PAD_TO) * PAD_TO
    topk = min(TOPK_MAX, length // 4)
    xp = jnp.concatenate([meta_tokens.astype(x2.dtype), x2, jnp.zeros((lp - length, D_MODEL), x2.dtype)], axis=0)
    h, hb = embed_ln(xp, ln_emb_g, ln_emb_b)
    for p in layers:
        h, hb = _layer(h, hb, p, rel_bias, topk, alpha)
    return h[N_META:length]


def kernel(x, meta_tokens, ln_emb_g, ln_emb_b, rel_bias, w_in, conv_w, conv_b, b_igate, b_fgate, mh_norm_g,
           w_proj_m, w_proj_a, w_out, ln1_g, ln1_b, w_group, b_group, w_router, b_router, w_gate_up, w_down,
           ln2_g, ln2_b):
    depth = w_in.shape[0]
    layers = [dict(w_in=w_in[l], conv_w=conv_w[l], conv_b=conv_b[l], b_igate=b_igate[l], b_fgate=b_fgate[l],
                   mh_norm_g=mh_norm_g[l], w_proj_m=w_proj_m[l], w_proj_a=w_proj_a[l], w_out=w_out[l],
                   ln1_g=ln1_g[l], ln1_b=ln1_b[l], w_group=w_group[l], b_group=b_group[l],
                   w_router=w_router[l], b_router=b_router[l], w_gate_up=w_gate_up[l], w_down=w_down[l],
                   ln2_g=ln2_g[l], ln2_b=ln2_b[l]) for l in range(depth)]
    outs = [_trunk(x[b], meta_tokens, ln_emb_g, ln_emb_b, rel_bias, layers) for b in range(x.shape[0])]
    return jnp.stack(outs, axis=0)
```

```python
import functools
import math

import jax
import jax.numpy as jnp
import numpy as np
from jax import lax
from jax.experimental import pallas as pl
from jax.experimental.pallas import tpu as pltpu

D_MODEL = 2048
N_META = 16
M_HEADS = 4
M_QK = 128
M_V = 256
CONV_W = 4
A_HEADS = 8
A_DIM = 128
TOPK_MAX = 256
IDX_HEADS = 8
IDX_DIM = 64
REL_BUCKETS = 32
REL_MAX_DIST = 128
N_GROUPS = 4
EXP_PER_GROUP = 8
N_EXPERTS = 32
D_FF = 512
MOE_BLOCK = 128
LN_EPS = 1e-5
LOG2E = math.log2(math.e)
NEG = -1e30
BIG = 1e30

M_QK_W = M_HEADS * M_QK
M_V_W = M_HEADS * M_V
A_W = A_HEADS * A_DIM
IDX_Q_W = IDX_HEADS * IDX_DIM

COL_GM = 0
COL_GA = 2048
COL_MV = 4096
COL_MO = 5120
COL_AQ = 6144
COL_AK = 7168
COL_AV = 8192
COL_MQK = 9216
COL_IQ = 10240
BIG_W = 10752
SM_W = 128

ROW_TILE = 256
Q_TILE = 512
PAD_TO = 512
F32 = jnp.float32
BF16 = jnp.bfloat16
VMEM_LIMIT = 56 * 1024 * 1024
HI = lax.Precision.HIGHEST


def _cparams(sem):
    return pltpu.CompilerParams(dimension_semantics=sem, vmem_limit_bytes=VMEM_LIMIT)


def _pick(n, cands):
    for c in cands:
        if n % c == 0:
            return c
    raise ValueError(f"no tile for {n}")


def _ln(x, g, b):
    mu = jnp.mean(x, axis=-1, keepdims=True)
    xc = x - mu
    var = jnp.mean(xc * xc, axis=-1, keepdims=True)
    return xc * lax.rsqrt(var + LN_EPS) * g + b


def _sigmoid(x):
    return 1.0 / (1.0 + jnp.exp(-x))


def _embed_ln_kernel(x_ref, g_ref, b_ref, h_ref, hb_ref):
    y = _ln(x_ref[...], g_ref[...], b_ref[...])
    h_ref[...] = y
    hb_ref[...] = y.astype(BF16)


def embed_ln(xp, g, b):
    lp = xp.shape[0]
    tm = ROW_TILE
    return pl.pallas_call(
        _embed_ln_kernel,
        grid=(lp // tm,),
        in_specs=[pl.BlockSpec((tm, D_MODEL), lambda i: (i, 0)),
                  pl.BlockSpec((1, D_MODEL), lambda i: (0, 0)),
                  pl.BlockSpec((1, D_MODEL), lambda i: (0, 0))],
        out_specs=[pl.BlockSpec((tm, D_MODEL), lambda i: (i, 0)),
                   pl.BlockSpec((tm, D_MODEL), lambda i: (i, 0))],
        out_shape=[jax.ShapeDtypeStruct((lp, D_MODEL), F32),
                   jax.ShapeDtypeStruct((lp, D_MODEL), BF16)],
        compiler_params=_cparams(("parallel",)),
        name="embed_ln",
    )(xp, g.reshape(1, -1), b.reshape(1, -1))


def _mm_kernel(a_ref, b_ref, o_ref):
    o_ref[...] = jnp.dot(a_ref[...], b_ref[...], preferred_element_type=F32).astype(o_ref.dtype)


def in_proj_big(hb, w):
    lp = hb.shape[0]
    tm = _pick(lp, (1536, 1280, 1024, 768, 512))
    tn = 1536
    return pl.pallas_call(
        _mm_kernel,
        grid=(BIG_W // tn, lp // tm),
        in_specs=[pl.BlockSpec((tm, D_MODEL), lambda j, i: (i, 0)),
                  pl.BlockSpec((D_MODEL, tn), lambda j, i: (0, j))],
        out_specs=pl.BlockSpec((tm, tn), lambda j, i: (i, j)),
        out_shape=jax.ShapeDtypeStruct((lp, BIG_W), BF16),
        compiler_params=_cparams(("parallel", "parallel")),
        name="in_proj_big",
    )(hb, w)


def _mm_hi_kernel(a_ref, b_ref, o_ref):
    o_ref[...] = jnp.dot(a_ref[...], b_ref[...], preferred_element_type=F32, precision=HI)


def in_proj_small(h, w):
    lp = h.shape[0]
    tm = ROW_TILE
    return pl.pallas_call(
        _mm_hi_kernel,
        grid=(lp // tm,),
        in_specs=[pl.BlockSpec((tm, D_MODEL), lambda i: (i, 0)),
                  pl.BlockSpec((D_MODEL, SM_W), lambda i: (0, 0))],
        out_specs=pl.BlockSpec((tm, SM_W), lambda i: (i, 0)),
        out_shape=jax.ShapeDtypeStruct((lp, SM_W), F32),
        compiler_params=_cparams(("parallel",)),
        name="in_proj_small",
    )(h, w)


def _log_sigmoid(x):
    return jnp.minimum(x, 0.0) - jnp.log1p(jnp.exp(-jnp.abs(x)))


def _mlstm_kernel(qk_ref, v_ref, o_ref, sm_ref, smt_ref, cw_ref, cb_ref, gb_ref, gbt_ref, mhg_ref,
                  y_ref, ubuf, c_ref, n_ref, m_ref):
    t = ROW_TILE
    step = pl.program_id(0)

    @pl.when(step == 0)
    def _():
        ubuf[0:8, :] = jnp.zeros((8, 2 * M_QK_W), F32)
        c_ref[...] = jnp.zeros_like(c_ref)
        n_ref[...] = jnp.zeros_like(n_ref)
        m_ref[...] = jnp.zeros_like(m_ref)

    ubuf[8:8 + t, :] = qk_ref[...].astype(F32)
    conv = cb_ref[...]
    for j in range(CONV_W):
        d = CONV_W - 1 - j
        conv = conv + cw_ref[j:j + 1, :] * ubuf[8 - d:8 - d + t, :]
    ubuf[0:8, :] = ubuf[t:t + 8, :]
    qk = conv * _sigmoid(conv)

    gcol = sm_ref[...] + gb_ref[...]
    grow = smt_ref[0:8, :] + gbt_ref[...]
    lane = lax.broadcasted_iota(jnp.int32, (t, SM_W), 1)
    gcol = jnp.where(lane < M_HEADS, gcol, _log_sigmoid(gcol))
    srow = lax.broadcasted_iota(jnp.int32, (8, t), 0)
    grow = jnp.where(srow < M_HEADS, grow, _log_sigmoid(grow))
    ri = lax.broadcasted_iota(jnp.int32, (t, t), 0)
    ci = lax.broadcasted_iota(jnp.int32, (t, t), 1)
    causal = ci <= ri
    tri = causal.astype(F32)
    bcol = jnp.dot(tri, gcol, preferred_element_type=F32, precision=HI)
    brow = lax.dot_general(grow, tri, (((1,), (1,)), ((), ())), preferred_element_type=F32,
                           precision=HI)

    v_all = v_ref[...]
    sig_o = _sigmoid(o_ref[...].astype(F32))
    qs = [qk[:, h * M_QK:(h + 1) * M_QK] * (M_QK ** -0.5) for h in range(M_HEADS)]
    ks = [qk[:, M_QK_W + h * M_QK:M_QK_W + (h + 1) * M_QK] for h in range(M_HEADS)]
    qbs = [q.astype(BF16) for q in qs]
    kbs = [k.astype(BF16) for k in ks]
    c_olds = [c_ref[h] for h in range(M_HEADS)]
    s_all = [lax.dot_general(qbs[h], kbs[h], (((1,), (1,)), ((), ())), preferred_element_type=F32)
             for h in range(M_HEADS)]
    qc_all = [jnp.dot(qbs[h], c_olds[h].astype(BF16), preferred_element_type=F32) for h in range(M_HEADS)]
    for h in range(M_HEADS):
        q, k, qb, kb = qs[h], ks[h], qbs[h], kbs[h]
        v = v_all[:, h * M_V:(h + 1) * M_V]
        b_c = bcol[:, M_HEADS + h:M_HEADS + h + 1]
        ig_c = gcol[:, h:h + 1]
        b_r = brow[M_HEADS + h:M_HEADS + h + 1, :]
        ig_r = grow[h:h + 1, :]
        m_prev = m_ref[h, 0:1, 0:1]
        dmat = jnp.where(causal, b_c + (ig_r - b_r), NEG)
        inter = b_c + m_prev
        m_t = jnp.maximum(jnp.max(dmat, axis=-1, keepdims=True), inter)
        w_intra = jnp.exp(dmat - m_t) * s_all[h]
        w_inter = jnp.exp(inter - m_t)
        c_old = c_olds[h]
        n_old = n_ref[h, 0:1, :]
        num = jnp.dot(w_intra.astype(BF16), v, preferred_element_type=F32) + w_inter * qc_all[h]
        den = (jnp.sum(w_intra, axis=-1, keepdims=True)
               + w_inter * jnp.sum(q * n_old, axis=-1, keepdims=True))
        hh = num / jnp.maximum(jnp.abs(den), jnp.exp(-m_t))
        mu = jnp.mean(hh, axis=-1, keepdims=True)
        hc = hh - mu
        var = jnp.mean(hc * hc, axis=-1, keepdims=True)
        hn = hc * lax.rsqrt(var + LN_EPS) * mhg_ref[:, h * M_V:(h + 1) * M_V]
        y_ref[:, h * M_V:(h + 1) * M_V] = (sig_o[:, h * M_V:(h + 1) * M_V] * hn).astype(y_ref.dtype)
        b_last = b_c[t - 1:t, :]
        dec = b_last - b_c + ig_c
        m_new = jnp.maximum(b_last + m_prev, jnp.max(dec, axis=0, keepdims=True))
        w_src = jnp.exp(dec - m_new)
        cs = jnp.exp(b_last + m_prev - m_new)
        wv = (w_src * v.astype(F32)).astype(BF16)
        c_ref[h] = cs * c_old + jnp.dot(k.T.astype(BF16), wv, preferred_element_type=F32)
        n_new = cs * n_old + jnp.sum(w_src * k, axis=0, keepdims=True)
        n_ref[h] = jnp.broadcast_to(n_new, (8, M_QK))
        m_ref[h] = jnp.broadcast_to(m_new, (8, 128))


def mlstm(proj, small, small_t, conv_w, conv_b, b_ig, b_fg, mh_g):
    lp = proj.shape[0]
    t = ROW_TILE
    gb = jnp.zeros((1, SM_W), F32).at[0, 0:M_HEADS].set(b_ig).at[0, M_HEADS:2 * M_HEADS].set(b_fg)
    gbt = gb[0, 0:8].reshape(8, 1)
    cq = COL_MQK // (2 * M_QK_W)
    cv = COL_MV // M_V_W
    co = COL_MO // M_V_W
    return pl.pallas_call(
        _mlstm_kernel,
        grid=(lp // t,),
        in_specs=[pl.BlockSpec((t, 2 * M_QK_W), lambda i: (i, cq)),
                  pl.BlockSpec((t, M_V_W), lambda i: (i, cv)),
                  pl.BlockSpec((t, M_V_W), lambda i: (i, co)),
                  pl.BlockSpec((t, SM_W), lambda i: (i, 0)),
                  pl.BlockSpec((16, t), lambda i: (0, i)),
                  pl.BlockSpec((CONV_W, 2 * M_QK_W), lambda i: (0, 0)),
                  pl.BlockSpec((1, 2 * M_QK_W), lambda i: (0, 0)),
                  pl.BlockSpec((1, SM_W), lambda i: (0, 0)),
                  pl.BlockSpec((8, 1), lambda i: (0, 0)),
                  pl.BlockSpec((1, M_V_W), lambda i: (0, 0))],
        out_specs=pl.BlockSpec((t, M_V_W), lambda i: (i, 0)),
        out_shape=jax.ShapeDtypeStruct((lp, M_V_W), BF16),
        scratch_shapes=[pltpu.VMEM((t + 8, 2 * M_QK_W), F32),
                        pltpu.VMEM((M_HEADS, M_QK, M_V), F32),
                        pltpu.VMEM((M_HEADS, 8, M_QK), F32),
                        pltpu.VMEM((M_HEADS, 8, 128), F32)],
        compiler_params=_cparams(("arbitrary",)),
        name="mlstm",
    )(proj, proj, proj, small, small_t, conv_w, conv_b.reshape(1, -1), gb, gbt, mh_g.reshape(1, -1))


def _f2key(f):
    b = pltpu.bitcast(f, jnp.int32)
    return jnp.where(b >= 0, b, b ^ jnp.int32(0x7FFFFFFF))


def _key2f(k):
    return pltpu.bitcast(jnp.where(k >= 0, k, k ^ jnp.int32(0x7FFFFFFF)), F32)


IDX_MAXIT = 128
IDX_VALUE_BISECT_ITERS = 48
KEY_MIN_NORMAL = 0x00800000
IDX_TQ = 256
IDX_LANES = 256


def _col_reduce(x, op):
    n = x.shape[0] // 8
    parts = [x[g * 8:(g + 1) * 8, :] for g in range(n)]
    while len(parts) > 1:
        nxt = [op(parts[a], parts[a + 1]) for a in range(0, len(parts) - 1, 2)]
        if len(parts) % 2:
            nxt.append(parts[-1])
        parts = nxt
    return parts[0]


def _index_mask_kernel(iqt_ref, wt_ref, ki_ref, mask_ref, s_ref, *, topk):
    tq = IDX_TQ
    tk = ROW_TILE
    i = pl.program_id(0)
    nk = i + 1
    nkt = pl.num_programs(0)
    q_pos = i * tq + lax.broadcasted_iota(jnp.int32, (1, tq), 1)
    k_iota = lax.broadcasted_iota(jnp.int32, (tk, 1), 0)

    def raw_scores(kj):
        kt = ki_ref[pl.ds(pl.multiple_of(kj * tk, tk), tk), :]
        acc = jnp.zeros((tk, tq), F32)
        for h in range(IDX_HEADS):
            x = jnp.dot(kt, iqt_ref[h * IDX_DIM:(h + 1) * IDX_DIM, :], preferred_element_type=F32)
            acc = acc + wt_ref[8 + h:9 + h, :] * jnp.maximum(x, 0.0)
        return acc

    def edge_tile(kj, vmin, vmax):
        acc = raw_scores(kj)
        key_pos = kj * tk + k_iota
        visible = key_pos <= q_pos
        ordinary = jnp.logical_and(visible, key_pos >= N_META)
        s_ref[pl.ds(pl.multiple_of(kj * tk, tk), tk), :] = jnp.where(
            visible, jnp.where(key_pos < N_META, BIG, acc), NEG)
        vmin = jnp.minimum(vmin, _col_reduce(jnp.where(ordinary, acc, BIG), jnp.minimum))
        vmax = jnp.maximum(vmax, _col_reduce(jnp.where(ordinary, acc, NEG), jnp.maximum))
        return vmin, vmax

    def inner_tiles(kjs, vmin, vmax):
        accs = [raw_scores(kj) for kj in kjs]
        for kj, acc in zip(kjs, accs):
            s_ref[pl.ds(pl.multiple_of(kj * tk, tk), tk), :] = acc
            vmin = jnp.minimum(vmin, _col_reduce(acc, jnp.minimum))
            vmax = jnp.maximum(vmax, _col_reduce(acc, jnp.maximum))
        return vmin, vmax

    n_inner = jnp.maximum(i - 1, 0)
    vmin, vmax = edge_tile(0, jnp.full((8, tq), BIG, F32), jnp.full((8, tq), NEG, F32))
    vmin, vmax = lax.fori_loop(0, n_inner // 2, lambda p, c: inner_tiles([1 + 2 * p, 2 + 2 * p], *c),
                               (vmin, vmax))
    vmin, vmax = lax.cond(n_inner % 2 == 1, lambda a, b: inner_tiles([i - 1], a, b), lambda a, b: (a, b),
                          vmin, vmax)
    vmin, vmax = lax.cond(i > 0, lambda a, b: edge_tile(i, a, b), lambda a, b: (a, b), vmin, vmax)
    vmin = jnp.min(vmin, axis=0, keepdims=True)
    vmax = jnp.max(vmax, axis=0, keepdims=True)

    def count(x, ls, strict=False):
        width = x.shape[1]
        xb = jnp.broadcast_to(x, (8, width))

        def body(kjs, acc):
            hits = []
            for kj in kjs:
                tile = s_ref[pl.ds(pl.multiple_of(kj * tk, tk), tk), ls]
                for g in range(tk // 8):
                    row = tile[g * 8:(g + 1) * 8, :]
                    hit = (row > xb) if strict else (row >= xb)
                    hits.append(jnp.where(hit, 1.0, 0.0))
            while len(hits) > 1:
                hits = [hits[a] + hits[a + 1] for a in range(0, len(hits), 2)]
            return acc + hits[0]

        acc = lax.fori_loop(0, nk // 2, lambda p, a: body([2 * p, 2 * p + 1], a), jnp.zeros((8, width), F32))
        acc = lax.cond(nk % 2 == 1, lambda a: body([nk - 1], a), lambda a: a, acc)
        return jnp.sum(acc, axis=0, keepdims=True)

    kf = float(topk)
    n_vis_all = (q_pos + 1).astype(F32)
    few_all = n_vis_all <= kf

    def cond(st):
        it, lo_k, hi_k, clo, chi, thr, done_i, cthr = st
        return jnp.logical_and(it < IDX_MAXIT, jnp.min(done_i) < 1)

    def step(ls, st):
        it, lo_k, hi_k, clo, chi, thr, done_i, cthr = st
        done = done_i > 0
        lo_v = _key2f(lo_k)
        hi_v = _key2f(hi_k)
        frac = jnp.log(clo / (kf + 0.5)) / jnp.log(clo / jnp.maximum(chi, 0.5))
        frac = jnp.clip(frac, 0.0, 1.0)
        pk_i = _f2key(lo_v + (hi_v - lo_v) * frac)
        pk_m = jnp.where(it < IDX_VALUE_BISECT_ITERS, _f2key(lo_v + 0.5 * (hi_v - lo_v)),
                         (lo_k >> 1) + (hi_k >> 1) + (lo_k & hi_k & 1))
        pk_m = jnp.where(jnp.logical_and(lo_k == 0, hi_k > KEY_MIN_NORMAL), KEY_MIN_NORMAL, pk_m)
        pk_m = jnp.where(jnp.logical_and(lo_k < 0, hi_k > 0), 0, pk_m)
        pk = jnp.where(it % 2 == 0, pk_i, pk_m)
        pk = jnp.minimum(jnp.maximum(pk, lo_k + 1), hi_k - 1)
        pk = jnp.where(done, lo_k, pk)
        x = _key2f(pk)
        c = count(x, ls)
        ge = c >= kf
        act = jnp.logical_not(done)
        up = jnp.logical_and(act, ge)
        dn = jnp.logical_and(act, jnp.logical_not(ge))
        lo_k = jnp.where(up, pk, lo_k)
        clo = jnp.where(up, c, clo)
        hi_k = jnp.where(dn, pk, hi_k)
        chi = jnp.where(dn, c, chi)
        exact = jnp.logical_and(act, c == kf)
        collapsed = jnp.logical_and(act, jnp.logical_or(
            hi_k <= lo_k + 1, jnp.logical_and(lo_k == 0, hi_k <= KEY_MIN_NORMAL)))
        fin = jnp.logical_or(exact, collapsed)
        thr = jnp.where(exact, x, jnp.where(collapsed, _key2f(lo_k), thr))
        cthr = jnp.where(exact, c, jnp.where(collapsed, clo, cthr))
        return it + 1, lo_k, hi_k, clo, chi, thr, jnp.where(jnp.logical_or(done, fin), 1, 0), cthr

    def search(ls):
        n_vis, few, lo_v0, hi_v0 = n_vis_all[:, ls], few_all[:, ls], vmin[:, ls], vmax[:, ls]
        n_meta = jnp.minimum(n_vis, float(N_META))
        lo_k0 = _f2key(lo_v0)
        hi_k0 = _f2key(hi_v0) + 1
        done0 = jnp.logical_or(few, n_vis - n_meta < 0.5)
        thr0 = jnp.full(n_vis.shape, 0.5 * NEG, F32)
        col0 = jnp.logical_and(jnp.logical_not(done0), hi_k0 <= lo_k0 + 1)
        thr0 = jnp.where(col0, lo_v0, thr0)
        st = (jnp.int32(0), lo_k0, hi_k0, n_vis, n_meta, thr0, jnp.where(jnp.logical_or(done0, col0), 1, 0),
              jnp.where(col0, n_vis, kf))
        _, _, _, _, _, thr, _, cthr = lax.while_loop(cond, functools.partial(step, ls), st)
        return jnp.where(few, 0.5 * NEG, thr), cthr

    found = [search(slice(a, a + IDX_LANES)) for a in range(0, tq, IDX_LANES)]
    thr = jnp.concatenate([f[0] for f in found], axis=1)
    cthr = jnp.concatenate([f[1] for f in found], axis=1)
    any_tie = jnp.max(jnp.where(jnp.logical_and(jnp.logical_not(few_all), cthr > kf), 1.0, 0.0)) > 0.5

    @pl.when(jnp.logical_not(any_tie))
    def _():
        def emit(kj, _):
            s = s_ref[pl.ds(pl.multiple_of(kj * tk, tk), tk), :]
            mask_ref[pl.ds(pl.multiple_of(kj * tk, tk), tk), :] = jnp.where(s >= thr, 1, 0).astype(jnp.int8)
            return 0
        lax.fori_loop(0, nk, emit, 0)

    @pl.when(any_tie)
    def _():
        allowed = kf - count(thr, slice(0, tq), strict=True)
        ri = lax.broadcasted_iota(jnp.int32, (tk, tk), 0)
        ci = lax.broadcasted_iota(jnp.int32, (tk, tk), 1)
        before = (ci < ri).astype(BF16)

        def emit(kjs, run):
            tiles = [s_ref[pl.ds(pl.multiple_of(kj * tk, tk), tk), :] for kj in kjs]
            eqfs = [jnp.where(s == thr, 1.0, 0.0) for s in tiles]
            prefix = [jnp.dot(before, e.astype(BF16), preferred_element_type=F32) for e in eqfs]
            for kj, s, e, pre in zip(kjs, tiles, eqfs, prefix):
                sel = jnp.logical_or(s > thr, jnp.logical_and(e > 0.5, run + pre < allowed))
                mask_ref[pl.ds(pl.multiple_of(kj * tk, tk), tk), :] = jnp.where(sel, 1, 0).astype(jnp.int8)
                run = run + jnp.sum(e, axis=0, keepdims=True)
            return run

        run = lax.fori_loop(0, nk // 2, lambda p, r: emit([2 * p, 2 * p + 1], r), jnp.zeros((1, tq), F32))

        @pl.when(nk % 2 == 1)
        def _():
            emit([nk - 1], run)

    def clear(kj, _):
        mask_ref[pl.ds(pl.multiple_of(kj * tk, tk), tk), :] = jnp.zeros((tk, tq), jnp.int8)
        return 0
    lax.fori_loop(nk, nkt, clear, 0)


def index_mask(iq_t, small_t, ki, topk):
    lp = iq_t.shape[1]
    tq = IDX_TQ
    assert IDX_TQ == ROW_TILE
    return pl.pallas_call(
        functools.partial(_index_mask_kernel, topk=topk),
        grid=(lp // tq,),
        in_specs=[pl.BlockSpec((IDX_Q_W, tq), lambda i: (0, i)),
                  pl.BlockSpec((16, tq), lambda i: (0, i)),
                  pl.BlockSpec((lp, IDX_DIM), lambda i: (0, 0), pipeline_mode=pl.Buffered(1))],
        out_specs=pl.BlockSpec((lp, tq), lambda i: (0, i)),
        out_shape=jax.ShapeDtypeStruct((lp, lp), jnp.int8),
        scratch_shapes=[pltpu.VMEM((lp, tq), F32)],
        compiler_params=_cparams(("arbitrary",)),
        name="index_mask",
    )(iq_t, small_t, ki)


ATTN_AHEAD = 2
ONES_ROWS = 16


def _attn_kernel(qi_ref, kj_ref, qt_ref, k_ref, vt_ref, mask_ref, bp_ref, b0_ref, bm_ref, o_ref,
                 m_ref, l_ref, acc_ref, neg_ref):
    tq, tk = Q_TILE, ROW_TILE
    sidx = pl.program_id(0)
    qi = qi_ref[sidx]
    kj = kj_ref[sidx]
    delta = kj - 2 * qi

    @pl.when(kj == 0)
    def _():
        m_ref[...] = jnp.full(m_ref.shape, NEG, F32)
        l_ref[...] = jnp.zeros(l_ref.shape, F32)
        acc_ref[...] = jnp.zeros(acc_ref.shape, F32)

    def tile(bias_ref):
        neg_ref[...] = jnp.where(mask_ref[...].astype(F32) > 0.5, 0.0, NEG)
        ones = jnp.ones((ONES_ROWS, tk), BF16)

        def logits(h):
            hs = slice(h * A_DIM, (h + 1) * A_DIM)
            s = jnp.dot(k_ref[:, hs], qt_ref[hs, :], preferred_element_type=F32) + neg_ref[...]
            if bias_ref is not None:
                s = s + bias_ref[h]
            return s

        ahead = [logits(h) for h in range(min(ATTN_AHEAD, A_HEADS))]
        for h in range(A_HEADS):
            hs = slice(h * A_DIM, (h + 1) * A_DIM)
            if h + ATTN_AHEAD < A_HEADS:
                ahead.append(logits(h + ATTN_AHEAD))
            s = ahead.pop(0)
            m_old = m_ref[h]
            m_new = jnp.maximum(m_old, jnp.max(s, axis=0, keepdims=True))
            alpha = jnp.exp2(m_old - m_new)
            p = jnp.exp2(s - m_new).astype(BF16)
            pv = jnp.dot(jnp.concatenate([vt_ref[hs, :], ones], axis=0), p, preferred_element_type=F32)
            acc_ref[hs, :] = alpha * acc_ref[hs, :] + pv[0:A_DIM, :]
            l_ref[h] = alpha * l_ref[h] + pv[A_DIM:A_DIM + 1, :]
            m_ref[h] = m_new

    pl.when(delta == 1)(lambda: tile(bp_ref))
    pl.when(delta == 0)(lambda: tile(b0_ref))
    pl.when(delta == -1)(lambda: tile(bm_ref))
    pl.when(delta < -1)(lambda: tile(None))

    @pl.when(delta == 1)
    def _():
        for h in range(A_HEADS):
            hs = slice(h * A_DIM, (h + 1) * A_DIM)
            o_ref[:, hs] = (acc_ref[hs, :] / l_ref[h]).T.astype(o_ref.dtype)


def _t5_bucket_np(rel):
    rel = np.maximum(rel, 0)
    max_exact = REL_BUCKETS // 2
    rel_f = np.maximum(rel, 1).astype(np.float32)
    large = max_exact + (np.log(rel_f / np.float32(max_exact)) / np.float32(math.log(REL_MAX_DIST / max_exact))
                         * np.float32(REL_BUCKETS - max_exact)).astype(np.int32)
    large = np.minimum(large, REL_BUCKETS - 1)
    return np.where(rel < max_exact, rel, large)


def attention(q_t, proj, v_t, mask_t, rel_bias):
    lp = proj.shape[0]
    tq, tk = Q_TILE, ROW_TILE
    assert tq == 2 * tk and tk >= REL_MAX_DIST
    nq = lp // tq
    qi_l, kj_l = [], []
    for a in range(nq):
        for b in range(2 * a + 2):
            qi_l.append(a)
            kj_l.append(b)
    qi_arr = jnp.asarray(np.array(qi_l, np.int32))
    kj_arr = jnp.asarray(np.array(kj_l, np.int32))
    table = (rel_bias.astype(F32) - rel_bias[REL_BUCKETS - 1].astype(F32)) * LOG2E

    def btile(d):
        n = tq + tk
        j = np.arange(n)
        diag = np.where(j < tq, j, j - n)
        vec = jnp.take(table, jnp.asarray(_t5_bucket_np(diag - tk * d).astype(np.int32)), axis=0).T
        return jnp.tile(vec, (1, tk))[:, :tk * (n - 1)].reshape(A_HEADS, tk, n - 1)[:, :, :tq]

    ck = COL_AK // A_W
    const3 = lambda s, qi, kj: (0, 0, 0)
    grid_spec = pltpu.PrefetchScalarGridSpec(
        num_scalar_prefetch=2,
        grid=(len(qi_l),),
        in_specs=[pl.BlockSpec((A_W, tq), lambda s, qi, kj: (0, qi[s])),
                  pl.BlockSpec((tk, A_W), lambda s, qi, kj: (kj[s], ck)),
                  pl.BlockSpec((A_W, tk), lambda s, qi, kj: (0, kj[s])),
                  pl.BlockSpec((tk, tq), lambda s, qi, kj: (kj[s], qi[s])),
                  pl.BlockSpec((A_HEADS, tk, tq), const3, pipeline_mode=pl.Buffered(1)),
                  pl.BlockSpec((A_HEADS, tk, tq), const3, pipeline_mode=pl.Buffered(1)),
                  pl.BlockSpec((A_HEADS, tk, tq), const3, pipeline_mode=pl.Buffered(1))],
        out_specs=pl.BlockSpec((tq, A_W), lambda s, qi, kj: (qi[s], 0)),
        scratch_shapes=[pltpu.VMEM((A_HEADS, 1, tq), F32),
                        pltpu.VMEM((A_HEADS, 1, tq), F32),
                        pltpu.VMEM((A_W, tq), F32),
                        pltpu.VMEM((tk, tq), F32)],
    )
    return pl.pallas_call(
        _attn_kernel,
        grid_spec=grid_spec,
        out_shape=jax.ShapeDtypeStruct((lp, A_W), BF16),
        compiler_params=_cparams(("arbitrary",)),
        name="attention",
    )(qi_arr, kj_arr, q_t, proj, v_t, mask_t, btile(1), btile(0), btile(-1))


def _merge_kernel(ym_ref, ya_ref, gm_ref, ga_ref, h_ref, wpm_ref, wpa_ref, wo_ref, g_ref, b_ref,
                  hn_ref, hb_ref, *, alpha):
    pm = jnp.dot(ym_ref[...], wpm_ref[...], preferred_element_type=F32)
    pa = jnp.dot(ya_ref[...], wpa_ref[...], preferred_element_type=F32)
    merged = _sigmoid(gm_ref[...].astype(F32)) * pm + _sigmoid(ga_ref[...].astype(F32)) * pa
    mix = jnp.dot(merged.astype(BF16), wo_ref[...], preferred_element_type=F32)
    y = _ln(alpha * h_ref[...] + mix, g_ref[...], b_ref[...])
    hn_ref[...] = y
    hb_ref[...] = y.astype(BF16)


def merge_out_ln(ym, ya, proj, h, wpm, wpa, wo, g, b, alpha):
    lp = h.shape[0]
    tm = ROW_TILE
    cgm, cga = COL_GM // D_MODEL, COL_GA // D_MODEL
    const = lambda i: (0, 0)
    return pl.pallas_call(
        functools.partial(_merge_kernel, alpha=alpha),
        grid=(lp // tm,),
        in_specs=[pl.BlockSpec((tm, M_V_W), lambda i: (i, 0)),
                  pl.BlockSpec((tm, A_W), lambda i: (i, 0)),
                  pl.BlockSpec((tm, D_MODEL), lambda i: (i, cgm)),
                  pl.BlockSpec((tm, D_MODEL), lambda i: (i, cga)),
                  pl.BlockSpec((tm, D_MODEL), lambda i: (i, 0)),
                  pl.BlockSpec((M_V_W, D_MODEL), const, pipeline_mode=pl.Buffered(1)),
                  pl.BlockSpec((A_W, D_MODEL), const, pipeline_mode=pl.Buffered(1)),
                  pl.BlockSpec((D_MODEL, D_MODEL), const, pipeline_mode=pl.Buffered(1)),
                  pl.BlockSpec((1, D_MODEL), const),
                  pl.BlockSpec((1, D_MODEL), const)],
        out_specs=[pl.BlockSpec((tm, D_MODEL), lambda i: (i, 0)),
                   pl.BlockSpec((tm, D_MODEL), lambda i: (i, 0))],
        out_shape=[jax.ShapeDtypeStruct((lp, D_MODEL), F32),
                   jax.ShapeDtypeStruct((lp, D_MODEL), BF16)],
        compiler_params=_cparams(("parallel",)),
        name="merge_out_ln",
    )(ym, ya, proj, proj, h, wpm, wpa, wo, g.reshape(1, -1), b.reshape(1, -1))


def _router_kernel(h_ref, w_ref, b_ref, e_ref, g_ref):
    lg = jnp.dot(h_ref[...], w_ref[...], preferred_element_type=F32, precision=HI) + b_ref[...]
    tm = lg.shape[0]
    col = lax.broadcasted_iota(jnp.int32, (tm, SM_W), 1).astype(F32)
    far = 1e9

    def first_argmax(x):
        mx = jnp.max(x, axis=-1, keepdims=True)
        return mx, jnp.min(jnp.where(x == mx, col, far), axis=-1, keepdims=True)

    gl = jnp.where(col < N_GROUPS, lg, -jnp.inf)
    gmax, gsel = first_argmax(gl)
    gprob = 1.0 / jnp.sum(jnp.exp(gl - gmax), axis=-1, keepdims=True)
    c0 = N_GROUPS + EXP_PER_GROUP * gsel
    el = jnp.where(jnp.logical_and(col >= c0, col < c0 + EXP_PER_GROUP), lg, -jnp.inf)
    v1, i1 = first_argmax(el)
    v2, i2 = first_argmax(jnp.where(col == i1, -jnp.inf, el))
    e21 = jnp.exp(v2 - v1)
    g1 = gprob / (1.0 + e21)
    g2 = gprob * e21 / (1.0 + e21)
    e_ref[...] = jnp.where(col == 0, i1 - N_GROUPS, jnp.where(col == 1, i2 - N_GROUPS, 0.0)).astype(jnp.int32)
    g_ref[...] = jnp.where(col == 0, g1, jnp.where(col == 1, g2, 0.0))


def router(h, w, b):
    lp = h.shape[0]
    tm = ROW_TILE
    return pl.pallas_call(
        _router_kernel,
        grid=(lp // tm,),
        in_specs=[pl.BlockSpec((tm, D_MODEL), lambda i: (i, 0)),
                  pl.BlockSpec((D_MODEL, SM_W), lambda i: (0, 0)),
                  pl.BlockSpec((1, SM_W), lambda i: (0, 0))],
        out_specs=[pl.BlockSpec((tm, SM_W), lambda i: (i, 0)),
                   pl.BlockSpec((tm, SM_W), lambda i: (i, 0))],
        out_shape=[jax.ShapeDtypeStruct((lp, SM_W), jnp.int32),
                   jax.ShapeDtypeStruct((lp, SM_W), F32)],
        compiler_params=_cparams(("parallel",)),
        name="router",
    )(h, w, b)


def _expert_kernel(blk_e_ref, tok_ref, orow_ref, nused_ref, x_hbm, w_ref, wgu_ref, wd_ref, y_hbm,
                   xbuf, ybuf, wgu_b, wd_b, gsem, ssem):
    b = pl.program_id(0)
    nb = pl.num_programs(0)
    nused = nused_ref[0]
    blk = MOE_BLOCK

    def start_gather(block, slot):
        for r in range(blk):
            tok = tok_ref[block * blk + r]
            pltpu.make_async_copy(x_hbm.at[pl.ds(tok, 1), :], xbuf.at[slot, pl.ds(r, 1), :], gsem.at[slot]).start()

    def wait_gather(slot):
        pltpu.make_async_copy(x_hbm.at[pl.ds(0, blk), :], xbuf.at[slot], gsem.at[slot]).wait()

    def start_scatter(block, slot):
        for r in range(blk):
            dst = orow_ref[block * blk + r]
            pltpu.make_async_copy(ybuf.at[slot, pl.ds(r, 1), :], y_hbm.at[pl.ds(dst, 1), :], ssem.at[slot]).start()

    def wait_scatter(slot):
        pltpu.make_async_copy(ybuf.at[slot], y_hbm.at[pl.ds(0, blk), :], ssem.at[slot]).wait()

    @pl.when(b == 0)
    def _():
        ybuf[1] = jnp.zeros((blk, D_MODEL), F32)
        spare = pltpu.make_async_copy(ybuf.at[1], y_hbm.at[pl.ds(y_hbm.shape[0] - blk, blk), :], ssem.at[1])
        spare.start()
        spare.wait()

    @pl.when(jnp.logical_and(b == 0, nused > 0))
    def _():
        start_gather(0, 0)

    @pl.when(jnp.logical_and(b >= 2, b - 2 < nused))
    def _():
        wait_scatter(b % 2)

    @pl.when(b < nused)
    def _():
        changed = jnp.logical_or(b == 0, blk_e_ref[b] != blk_e_ref[jnp.maximum(b - 1, 0)])

        @pl.when(changed)
        def _():
            wgu_b[...] = wgu_ref[0].astype(BF16)
            wd_b[...] = wd_ref[0].astype(BF16)

    @pl.when(b < nused)
    def _():
        slot = b % 2
        wait_gather(slot)
        xb = xbuf[slot].astype(BF16)
        start_gather(b + 1, 1 - slot)
        gu = jnp.dot(xb, wgu_b[...], preferred_element_type=F32)
        g = gu[:, :D_FF]
        hb = (g * _sigmoid(g)) * gu[:, D_FF:]
        ybuf[slot] = jnp.dot(hb.astype(BF16), wd_b[...], preferred_element_type=F32) * w_ref[...]
        start_scatter(b, slot)

    @pl.when(jnp.logical_and(b == nused, nused > 0))
    def _():
        wait_gather(b % 2)

    @pl.when(b == nb - 1)
    def _():
        @pl.when(nused == nb)
        def _():
            wait_gather(nb % 2)

        @pl.when(nb - 2 < nused)
        def _():
            wait_scatter(nb % 2)

        @pl.when(nb - 1 < nused)
        def _():
            wait_scatter((nb - 1) % 2)


def expert_ffn(h, blk_e, row_tok, out_row, nused, row_w, wgu, wd):
    lp = h.shape[0]
    n_blocks = out_row.shape[0] // MOE_BLOCK
    assert row_tok.shape[0] == (n_blocks + 1) * MOE_BLOCK and n_blocks >= 2
    grid_spec = pltpu.PrefetchScalarGridSpec(
        num_scalar_prefetch=4,
        grid=(n_blocks,),
        in_specs=[pl.BlockSpec(memory_space=pl.ANY),
                  pl.BlockSpec((MOE_BLOCK, 1), lambda b, e, t, s, n: (b, 0)),
                  pl.BlockSpec((1, D_MODEL, 2 * D_FF), lambda b, e, t, s, n: (e[b], 0, 0)),
                  pl.BlockSpec((1, D_FF, D_MODEL), lambda b, e, t, s, n: (e[b], 0, 0))],
        out_specs=pl.BlockSpec(memory_space=pl.ANY),
        scratch_shapes=[pltpu.VMEM((2, MOE_BLOCK, D_MODEL), F32),
                        pltpu.VMEM((2, MOE_BLOCK, D_MODEL), F32),
                        pltpu.VMEM((D_MODEL, 2 * D_FF), BF16),
                        pltpu.VMEM((D_FF, D_MODEL), BF16),
                        pltpu.SemaphoreType.DMA((2,)),
                        pltpu.SemaphoreType.DMA((2,))],
    )
    return pl.pallas_call(
        _expert_kernel,
        grid_spec=grid_spec,
        out_shape=jax.ShapeDtypeStruct((2 * lp + MOE_BLOCK, D_MODEL), F32),
        compiler_params=_cparams(("arbitrary",)),
        name="expert_ffn",
    )(blk_e, row_tok, out_row, nused, h, row_w, wgu, wd)


def moe_dispatch(e_ids, gates):
    n_assign = e_ids.shape[0] * 2
    e_flat = e_ids.reshape(n_assign)
    w_flat = gates.reshape(n_assign)
    cb = ROW_TILE
    onehot = (e_flat[:, None] == jnp.arange(N_EXPERTS, dtype=jnp.int32)[None, :]).astype(F32)
    onehot = onehot.reshape(n_assign // cb, cb, N_EXPERTS)
    tri = jnp.asarray(np.tril(np.ones((cb, cb), np.float32)))
    within = jnp.einsum('ij,bjk->bik', tri, onehot, precision=HI)
    block_tot = jnp.sum(onehot, axis=1)
    block_off = jnp.cumsum(block_tot, axis=0) - block_tot
    csum = within + block_off[:, None, :]
    rank = (jnp.sum(onehot * csum, axis=2) - 1.0).astype(jnp.int32).reshape(n_assign)
    counts = jnp.sum(block_tot, axis=0).astype(jnp.int32)
    padded = (counts + MOE_BLOCK - 1) // MOE_BLOCK * MOE_BLOCK
    pend = jnp.cumsum(padded)
    pstart = pend - padded
    dest = pstart[e_flat] + rank
    n_blocks = n_assign // MOE_BLOCK + N_EXPERTS
    n_rows = n_blocks * MOE_BLOCK
    row_slot = jnp.full((n_rows,), -1, jnp.int32).at[dest].set(jnp.arange(n_assign, dtype=jnp.int32))
    valid = row_slot >= 0
    slot = jnp.maximum(row_slot, 0)
    n_tok = n_assign // 2
    row_tok = jnp.concatenate([slot >> 1, jnp.zeros((MOE_BLOCK,), jnp.int32)])
    out_row = jnp.where(valid, (slot & 1) * n_tok + (slot >> 1),
                        n_assign + jnp.arange(n_rows, dtype=jnp.int32) % MOE_BLOCK)
    row_w = jnp.where(valid, w_flat[slot], 0.0).reshape(n_rows, 1)
    blk_start = jnp.arange(n_blocks, dtype=jnp.int32) * MOE_BLOCK
    blk_e = jnp.minimum(jnp.sum((pend[None, :] <= blk_start[:, None]).astype(jnp.int32), axis=1), N_EXPERTS - 1)
    nused = (pend[-1] // MOE_BLOCK).astype(jnp.int32).reshape(1)
    return blk_e, row_tok, out_row, nused, row_w


def _combine_kernel(h_ref, y0_ref, y1_ref, g_ref, b_ref, hn_ref, hb_ref, *, alpha):
    ffn = y0_ref[...] + y1_ref[...]
    y = _ln(alpha * h_ref[...] + ffn, g_ref[...], b_ref[...])
    hn_ref[...] = y
    hb_ref[...] = y.astype(BF16)


def combine_ln(h, y2, g, b, alpha):
    lp = h.shape[0]
    tm = ROW_TILE
    half = lp // tm
    return pl.pallas_call(
        functools.partial(_combine_kernel, alpha=alpha),
        grid=(lp // tm,),
        in_specs=[pl.BlockSpec((tm, D_MODEL), lambda i: (i, 0)),
                  pl.BlockSpec((tm, D_MODEL), lambda i: (i, 0)),
                  pl.BlockSpec((tm, D_MODEL), lambda i: (i + half, 0)),
                  pl.BlockSpec((1, D_MODEL), lambda i: (0, 0)),
                  pl.BlockSpec((1, D_MODEL), lambda i: (0, 0))],
        out_specs=[pl.BlockSpec((tm, D_MODEL), lambda i: (i, 0)),
                   pl.BlockSpec((tm, D_MODEL), lambda i: (i, 0))],
        out_shape=[jax.ShapeDtypeStruct((lp, D_MODEL), F32),
                   jax.ShapeDtypeStruct((lp, D_MODEL), BF16)],
        compiler_params=_cparams(("parallel",)),
        name="combine_ln",
    )(h, y2, y2, g.reshape(1, -1), b.reshape(1, -1))


def _split_w_in(w):
    off = {}
    o = 0
    for name, width in (("mq", M_QK_W), ("mk", M_QK_W), ("mv", M_V_W), ("mo", M_V_W), ("mi", M_HEADS),
                        ("mf", M_HEADS), ("aq", A_W), ("ak", A_W), ("av", A_W), ("iq", IDX_Q_W),
                        ("ik", IDX_DIM), ("iw", IDX_HEADS), ("gm", D_MODEL), ("ga", D_MODEL)):
        off[name] = (o, o + width)
        o += width
    seg = lambda n: w[:, off[n][0]:off[n][1]]
    big = jnp.concatenate([seg("gm"), seg("ga"), seg("mv"), seg("mo"), seg("aq") * (A_DIM ** -0.5 * LOG2E),
                           seg("ak"), seg("av"), seg("mq"), seg("mk"), seg("iq") * (IDX_DIM ** -0.5)],
                          axis=1).astype(BF16)
    small = jnp.concatenate([seg("mi"), seg("mf"), seg("iw") * (IDX_HEADS ** -0.5),
                             jnp.zeros((w.shape[0], SM_W - 2 * M_HEADS - IDX_HEADS - IDX_DIM), F32),
                             seg("ik")], axis=1)
    return big, small


def _layer(h, hb, p, rel_bias, topk, alpha):
    w_big, w_small = _split_w_in(p["w_in"])
    proj = in_proj_big(hb, w_big)
    small = in_proj_small(h, w_small)
    small_t = small[:, 0:16].T
    ki = small[:, SM_W - IDX_DIM:].astype(BF16)
    q_t = proj[:, COL_AQ:COL_AQ + A_W].T
    v_t = proj[:, COL_AV:COL_AV + A_W].T
    iq_t = proj[:, COL_IQ:COL_IQ + IDX_Q_W].T
    y_m = mlstm(proj, small, small_t, p["conv_w"], p["conv_b"], p["b_igate"], p["b_fgate"], p["mh_norm_g"])
    mask_t = index_mask(iq_t, small_t, ki, topk)
    y_a = attention(q_t, proj, v_t, mask_t, rel_bias)
    h, hb = merge_out_ln(y_m, y_a, proj, h, p["w_proj_m"].astype(BF16), p["w_proj_a"].astype(BF16),
                         p["w_out"].astype(BF16), p["ln1_g"], p["ln1_b"], alpha)
    w_r = jnp.concatenate([p["w_group"], p["w_router"],
                           jnp.zeros((D_MODEL, SM_W - N_GROUPS - N_EXPERTS), F32)], axis=1)
    b_r = jnp.concatenate([p["b_group"], p["b_router"],
                           jnp.zeros((SM_W - N_GROUPS - N_EXPERTS,), F32)]).reshape(1, SM_W)
    e_out, g_out = router(h, w_r, b_r)
    blk_e, row_tok, out_row, nused, row_w = moe_dispatch(e_out[:, 0:2], g_out[:, 0:2])
    y2 = expert_ffn(h, blk_e, row_tok, out_row, nused, row_w, p["w_gate_up"], p["w_down"])
    return combine_ln(h, y2, p["ln2_g"], p["ln2_b"], alpha)


def _trunk(x2, meta_tokens, ln_emb_g, ln_emb_b, rel_bias, layers):
    depth = len(layers)
    alpha = (2 * depth) ** 0.25
    seq = x2.shape[0]
    length = N_META + seq
    lp = -(-length // PAD_TO) * PAD_TO
    topk = min(TOPK_MAX, length // 4)
    xp = jnp.concatenate([meta_tokens.astype(x2.dtype), x2, jnp.zeros((lp - length, D_MODEL), x2.dtype)], axis=0)
    h, hb = embed_ln(xp, ln_emb_g, ln_emb_b)
    for p in layers:
        h, hb = _layer(h, hb, p, rel_bias, topk, alpha)
    return h[N_META:length]


def kernel(x, meta_tokens, ln_emb_g, ln_emb_b, rel_bias, w_in, conv_w, conv_b, b_igate, b_fgate, mh_norm_g,
           w_proj_m, w_proj_a, w_out, ln1_g, ln1_b, w_group, b_group, w_router, b_router, w_gate_up, w_down,
           ln2_g, ln2_b):
    depth = w_in.shape[0]
    layers = [dict(w_in=w_in[l], conv_w=conv_w[l], conv_b=conv_b[l], b_igate=b_igate[l], b_fgate=b_fgate[l],
                   mh_norm_g=mh_norm_g[l], w_proj_m=w_proj_m[l], w_proj_a=w_proj_a[l], w_out=w_out[l],
                   ln1_g=ln1_g[l], ln1_b=ln1_b[l], w_group=w_group[l], b_group=b_group[l],
                   w_router=w_router[l], b_router=b_router[l], w_gate_up=w_gate_up[l], w_down=w_down[l],
                   ln2_g=ln2_g[l], ln2_b=ln2_b[l]) for l in range(depth)]
    outs = [_trunk(x[b], meta_tokens, ln_emb_g, ln_emb_b, rel_bias, layers) for b in range(x.shape[0])]
    return jnp.stack(outs, axis=0)
```

```python
import functools
import math

import jax
import jax.numpy as jnp
import numpy as np
from jax import lax
from jax.experimental import pallas as pl
from jax.experimental.pallas import tpu as pltpu

D_MODEL = 2048
N_META = 16
M_HEADS = 4
M_QK = 128
M_V = 256
CONV_W = 4
A_HEADS = 8
A_DIM = 128
TOPK_MAX = 256
IDX_HEADS = 8
IDX_DIM = 64
REL_BUCKETS = 32
REL_MAX_DIST = 128
N_GROUPS = 4
EXP_PER_GROUP = 8
N_EXPERTS = 32
D_FF = 512
MOE_BLOCK = 128
LN_EPS = 1e-5
LOG2E = math.log2(math.e)
NEG = -1e30
BIG = 1e30

M_QK_W = M_HEADS * M_QK
M_V_W = M_HEADS * M_V
A_W = A_HEADS * A_DIM
IDX_Q_W = IDX_HEADS * IDX_DIM

COL_GM = 0
COL_GA = 2048
COL_MV = 4096
COL_MO = 5120
COL_AQ = 6144
COL_AK = 7168
COL_AV = 8192
COL_MQK = 9216
COL_IQ = 10240
BIG_W = 10752
SM_W = 128

ROW_TILE = 256
Q_TILE = 512
PAD_TO = 512
F32 = jnp.float32
BF16 = jnp.bfloat16
VMEM_LIMIT = 56 * 1024 * 1024
HI = lax.Precision.HIGHEST


def _cparams(sem):
    return pltpu.CompilerParams(dimension_semantics=sem, vmem_limit_bytes=VMEM_LIMIT)


def _pick(n, cands):
    for c in cands:
        if n % c == 0:
            return c
    raise ValueError(f"no tile for {n}")


def _ln(x, g, b):
    mu = jnp.mean(x, axis=-1, keepdims=True)
    xc = x - mu
    var = jnp.mean(xc * xc, axis=-1, keepdims=True)
    return xc * lax.rsqrt(var + LN_EPS) * g + b


def _sigmoid(x):
    return 1.0 / (1.0 + jnp.exp(-x))


def _embed_ln_kernel(x_ref, g_ref, b_ref, h_ref, hb_ref):
    y = _ln(x_ref[...], g_ref[...], b_ref[...])
    h_ref[...] = y
    hb_ref[...] = y.astype(BF16)


def embed_ln(xp, g, b):
    lp = xp.shape[0]
    tm = ROW_TILE
    return pl.pallas_call(
        _embed_ln_kernel,
        grid=(lp // tm,),
        in_specs=[pl.BlockSpec((tm, D_MODEL), lambda i: (i, 0)),
                  pl.BlockSpec((1, D_MODEL), lambda i: (0, 0)),
                  pl.BlockSpec((1, D_MODEL), lambda i: (0, 0))],
        out_specs=[pl.BlockSpec((tm, D_MODEL), lambda i: (i, 0)),
                   pl.BlockSpec((tm, D_MODEL), lambda i: (i, 0))],
        out_shape=[jax.ShapeDtypeStruct((lp, D_MODEL), F32),
                   jax.ShapeDtypeStruct((lp, D_MODEL), BF16)],
        compiler_params=_cparams(("parallel",)),
        name="embed_ln",
    )(xp, g.reshape(1, -1), b.reshape(1, -1))


def _mm_kernel(a_ref, b_ref, o_ref):
    o_ref[...] = jnp.dot(a_ref[...], b_ref[...], preferred_element_type=F32).astype(o_ref.dtype)


def in_proj_big(hb, w):
    lp = hb.shape[0]
    tm = _pick(lp, (1536, 1280, 1024, 768, 512))
    tn = 1536
    return pl.pallas_call(
        _mm_kernel,
        grid=(BIG_W // tn, lp // tm),
        in_specs=[pl.BlockSpec((tm, D_MODEL), lambda j, i: (i, 0)),
                  pl.BlockSpec((D_MODEL, tn), lambda j, i: (0, j))],
        out_specs=pl.BlockSpec((tm, tn), lambda j, i: (i, j)),
        out_shape=jax.ShapeDtypeStruct((lp, BIG_W), BF16),
        compiler_params=_cparams(("parallel", "parallel")),
        name="in_proj_big",
    )(hb, w)


def _mm_hi_kernel(a_ref, b_ref, o_ref):
    o_ref[...] = jnp.dot(a_ref[...], b_ref[...], preferred_element_type=F32, precision=HI)


def in_proj_small(h, w):
    lp = h.shape[0]
    tm = ROW_TILE
    return pl.pallas_call(
        _mm_hi_kernel,
        grid=(lp // tm,),
        in_specs=[pl.BlockSpec((tm, D_MODEL), lambda i: (i, 0)),
                  pl.BlockSpec((D_MODEL, SM_W), lambda i: (0, 0))],
        out_specs=pl.BlockSpec((tm, SM_W), lambda i: (i, 0)),
        out_shape=jax.ShapeDtypeStruct((lp, SM_W), F32),
        compiler_params=_cparams(("parallel",)),
        name="in_proj_small",
    )(h, w)


def _log_sigmoid(x):
    return jnp.minimum(x, 0.0) - jnp.log1p(jnp.exp(-jnp.abs(x)))


def _mlstm_kernel(qk_ref, v_ref, o_ref, sm_ref, smt_ref, cw_ref, cb_ref, gb_ref, gbt_ref, mhg_ref,
                  y_ref, ubuf, c_ref, n_ref, m_ref):
    t = ROW_TILE
    step = pl.program_id(0)

    @pl.when(step == 0)
    def _():
        ubuf[0:8, :] = jnp.zeros((8, 2 * M_QK_W), F32)
        c_ref[...] = jnp.zeros_like(c_ref)
        n_ref[...] = jnp.zeros_like(n_ref)
        m_ref[...] = jnp.zeros_like(m_ref)

    ubuf[8:8 + t, :] = qk_ref[...].astype(F32)
    conv = cb_ref[...]
    for j in range(CONV_W):
        d = CONV_W - 1 - j
        conv = conv + cw_ref[j:j + 1, :] * ubuf[8 - d:8 - d + t, :]
    ubuf[0:8, :] = ubuf[t:t + 8, :]
    qk = conv * _sigmoid(conv)

    gcol = sm_ref[...] + gb_ref[...]
    grow = smt_ref[0:8, :] + gbt_ref[...]
    lane = lax.broadcasted_iota(jnp.int32, (t, SM_W), 1)
    gcol = jnp.where(lane < M_HEADS, gcol, _log_sigmoid(gcol))
    srow = lax.broadcasted_iota(jnp.int32, (8, t), 0)
    grow = jnp.where(srow < M_HEADS, grow, _log_sigmoid(grow))
    ri = lax.broadcasted_iota(jnp.int32, (t, t), 0)
    ci = lax.broadcasted_iota(jnp.int32, (t, t), 1)
    causal = ci <= ri
    tri = causal.astype(F32)
    bcol = jnp.dot(tri, gcol, preferred_element_type=F32, precision=HI)
    brow = lax.dot_general(grow, tri, (((1,), (1,)), ((), ())), preferred_element_type=F32,
                           precision=HI)

    v_all = v_ref[...]
    sig_o = _sigmoid(o_ref[...].astype(F32))
    qs = [qk[:, h * M_QK:(h + 1) * M_QK] * (M_QK ** -0.5) for h in range(M_HEADS)]
    ks = [qk[:, M_QK_W + h * M_QK:M_QK_W + (h + 1) * M_QK] for h in range(M_HEADS)]
    qbs = [q.astype(BF16) for q in qs]
    kbs = [k.astype(BF16) for k in ks]
    c_olds = [c_ref[h] for h in range(M_HEADS)]
    s_all = [lax.dot_general(qbs[h], kbs[h], (((1,), (1,)), ((), ())), preferred_element_type=F32)
             for h in range(M_HEADS)]
    qc_all = [jnp.dot(qbs[h], c_olds[h].astype(BF16), preferred_element_type=F32) for h in range(M_HEADS)]
    for h in range(M_HEADS):
        q, k, qb, kb = qs[h], ks[h], qbs[h], kbs[h]
        v = v_all[:, h * M_V:(h + 1) * M_V]
        b_c = bcol[:, M_HEADS + h:M_HEADS + h + 1]
        ig_c = gcol[:, h:h + 1]
        b_r = brow[M_HEADS + h:M_HEADS + h + 1, :]
        ig_r = grow[h:h + 1, :]
        m_prev = m_ref[h, 0:1, 0:1]
        dmat = jnp.where(causal, b_c + (ig_r - b_r), NEG)
        inter = b_c + m_prev
        m_t = jnp.maximum(jnp.max(dmat, axis=-1, keepdims=True), inter)
        w_intra = jnp.exp(dmat - m_t) * s_all[h]
        w_inter = jnp.exp(inter - m_t)
        c_old = c_olds[h]
        n_old = n_ref[h, 0:1, :]
        num = jnp.dot(w_intra.astype(BF16), v, preferred_element_type=F32) + w_inter * qc_all[h]
        den = (jnp.sum(w_intra, axis=-1, keepdims=True)
               + w_inter * jnp.sum(q * n_old, axis=-1, keepdims=True))
        hh = num / jnp.maximum(jnp.abs(den), jnp.exp(-m_t))
        mu = jnp.mean(hh, axis=-1, keepdims=True)
        hc = hh - mu
        var = jnp.mean(hc * hc, axis=-1, keepdims=True)
        hn = hc * lax.rsqrt(var + LN_EPS) * mhg_ref[:, h * M_V:(h + 1) * M_V]
        y_ref[:, h * M_V:(h + 1) * M_V] = (sig_o[:, h * M_V:(h + 1) * M_V] * hn).astype(y_ref.dtype)
        b_last = b_c[t - 1:t, :]
        dec = b_last - b_c + ig_c
        m_new = jnp.maximum(b_last + m_prev, jnp.max(dec, axis=0, keepdims=True))
        w_src = jnp.exp(dec - m_new)
        cs = jnp.exp(b_last + m_prev - m_new)
        wv = (w_src * v.astype(F32)).astype(BF16)
        c_ref[h] = cs * c_old + jnp.dot(k.T.astype(BF16), wv, preferred_element_type=F32)
        n_new = cs * n_old + jnp.sum(w_src * k, axis=0, keepdims=True)
        n_ref[h] = jnp.broadcast_to(n_new, (8, M_QK))
        m_ref[h] = jnp.broadcast_to(m_new, (8, 128))


def mlstm(proj, small, small_t, conv_w, conv_b, b_ig, b_fg, mh_g):
    lp = proj.shape[0]
    t = ROW_TILE
    gb = jnp.zeros((1, SM_W), F32).at[0, 0:M_HEADS].set(b_ig).at[0, M_HEADS:2 * M_HEADS].set(b_fg)
    gbt = gb[0, 0:8].reshape(8, 1)
    cq = COL_MQK // (2 * M_QK_W)
    cv = COL_MV // M_V_W
    co = COL_MO // M_V_W
    return pl.pallas_call(
        _mlstm_kernel,
        grid=(lp // t,),
        in_specs=[pl.BlockSpec((t, 2 * M_QK_W), lambda i: (i, cq)),
                  pl.BlockSpec((t, M_V_W), lambda i: (i, cv)),
                  pl.BlockSpec((t, M_V_W), lambda i: (i, co)),
                  pl.BlockSpec((t, SM_W), lambda i: (i, 0)),
                  pl.BlockSpec((16, t), lambda i: (0, i)),
                  pl.BlockSpec((CONV_W, 2 * M_QK_W), lambda i: (0, 0)),
                  pl.BlockSpec((1, 2 * M_QK_W), lambda i: (0, 0)),
                  pl.BlockSpec((1, SM_W), lambda i: (0, 0)),
                  pl.BlockSpec((8, 1), lambda i: (0, 0)),
                  pl.BlockSpec((1, M_V_W), lambda i: (0, 0))],
        out_specs=pl.BlockSpec((t, M_V_W), lambda i: (i, 0)),
        out_shape=jax.ShapeDtypeStruct((lp, M_V_W), BF16),
        scratch_shapes=[pltpu.VMEM((t + 8, 2 * M_QK_W), F32),
                        pltpu.VMEM((M_HEADS, M_QK, M_V), F32),
                        pltpu.VMEM((M_HEADS, 8, M_QK), F32),
                        pltpu.VMEM((M_HEADS, 8, 128), F32)],
        compiler_params=_cparams(("arbitrary",)),
        name="mlstm",
    )(proj, proj, proj, small, small_t, conv_w, conv_b.reshape(1, -1), gb, gbt, mh_g.reshape(1, -1))


def _f2key(f):
    b = pltpu.bitcast(f, jnp.int32)
    return jnp.where(b >= 0, b, b ^ jnp.int32(0x7FFFFFFF))


def _key2f(k):
    return pltpu.bitcast(jnp.where(k >= 0, k, k ^ jnp.int32(0x7FFFFFFF)), F32)


IDX_MAXIT = 128
IDX_VALUE_BISECT_ITERS = 48
KEY_MIN_NORMAL = 0x00800000
IDX_TQ = 256
IDX_LANES = 256


def _col_reduce(x, op):
    n = x.shape[0] // 8
    parts = [x[g * 8:(g + 1) * 8, :] for g in range(n)]
    while len(parts) > 1:
        nxt = [op(parts[a], parts[a + 1]) for a in range(0, len(parts) - 1, 2)]
        if len(parts) % 2:
            nxt.append(parts[-1])
        parts = nxt
    return parts[0]


def _index_mask_kernel(iqt_ref, wt_ref, ki_ref, mask_ref, s_ref, *, topk):
    tq = IDX_TQ
    tk = ROW_TILE
    i = pl.program_id(0)
    nk = i + 1
    nkt = pl.num_programs(0)
    q_pos = i * tq + lax.broadcasted_iota(jnp.int32, (1, tq), 1)
    k_iota = lax.broadcasted_iota(jnp.int32, (tk, 1), 0)

    def raw_scores(kj):
        kt = ki_ref[pl.ds(pl.multiple_of(kj * tk, tk), tk), :]
        acc = jnp.zeros((tk, tq), F32)
        for h in range(IDX_HEADS):
            x = jnp.dot(kt, iqt_ref[h * IDX_DIM:(h + 1) * IDX_DIM, :], preferred_element_type=F32)
            acc = acc + wt_ref[8 + h:9 + h, :] * jnp.maximum(x, 0.0)
        return acc

    def edge_tile(kj, vmin, vmax):
        acc = raw_scores(kj)
        key_pos = kj * tk + k_iota
        visible = key_pos <= q_pos
        ordinary = jnp.logical_and(visible, key_pos >= N_META)
        s_ref[pl.ds(pl.multiple_of(kj * tk, tk), tk), :] = jnp.where(
            visible, jnp.where(key_pos < N_META, BIG, acc), NEG)
        vmin = jnp.minimum(vmin, _col_reduce(jnp.where(ordinary, acc, BIG), jnp.minimum))
        vmax = jnp.maximum(vmax, _col_reduce(jnp.where(ordinary, acc, NEG), jnp.maximum))
        return vmin, vmax

    def inner_tiles(kjs, vmin, vmax):
        accs = [raw_scores(kj) for kj in kjs]
        for kj, acc in zip(kjs, accs):
            s_ref[pl.ds(pl.multiple_of(kj * tk, tk), tk), :] = acc
            vmin = jnp.minimum(vmin, _col_reduce(acc, jnp.minimum))
            vmax = jnp.maximum(vmax, _col_reduce(acc, jnp.maximum))
        return vmin, vmax

    n_inner = jnp.maximum(i - 1, 0)
    vmin, vmax = edge_tile(0, jnp.full((8, tq), BIG, F32), jnp.full((8, tq), NEG, F32))
    vmin, vmax = lax.fori_loop(0, n_inner // 2, lambda p, c: inner_tiles([1 + 2 * p, 2 + 2 * p], *c),
                               (vmin, vmax))
    vmin, vmax = lax.cond(n_inner % 2 == 1, lambda a, b: inner_tiles([i - 1], a, b), lambda a, b: (a, b),
                          vmin, vmax)
    vmin, vmax = lax.cond(i > 0, lambda a, b: edge_tile(i, a, b), lambda a, b: (a, b), vmin, vmax)
    vmin = jnp.min(vmin, axis=0, keepdims=True)
    vmax = jnp.max(vmax, axis=0, keepdims=True)

    def count(x, ls, strict=False):
        width = x.shape[1]
        xb = jnp.broadcast_to(x, (8, width))

        def body(kjs, acc):
            hits = []
            for kj in kjs:
                tile = s_ref[pl.ds(pl.multiple_of(kj * tk, tk), tk), ls]
                for g in range(tk // 8):
                    row = tile[g * 8:(g + 1) * 8, :]
                    hit = (row > xb) if strict else (row >= xb)
                    hits.append(jnp.where(hit, 1.0, 0.0))
            while len(hits) > 1:
                hits = [hits[a] + hits[a + 1] for a in range(0, len(hits), 2)]
            return acc + hits[0]

        acc = lax.fori_loop(0, nk // 2, lambda p, a: body([2 * p, 2 * p + 1], a), jnp.zeros((8, width), F32))
        acc = lax.cond(nk % 2 == 1, lambda a: body([nk - 1], a), lambda a: a, acc)
        return jnp.sum(acc, axis=0, keepdims=True)

    kf = float(topk)
    n_vis_all = (q_pos + 1).astype(F32)
    few_all = n_vis_all <= kf

    def cond(st):
        it, lo_k, hi_k, clo, chi, thr, done_i, cthr = st
        return jnp.logical_and(it < IDX_MAXIT, jnp.min(done_i) < 1)

    def step(ls, st):
        it, lo_k, hi_k, clo, chi, thr, done_i, cthr = st
        done = done_i > 0
        lo_v = _key2f(lo_k)
        hi_v = _key2f(hi_k)
        frac = jnp.log(clo / (kf + 0.5)) / jnp.log(clo / jnp.maximum(chi, 0.5))
        frac = jnp.clip(frac, 0.0, 1.0)
        pk_i = _f2key(lo_v + (hi_v - lo_v) * frac)
        pk_m = jnp.where(it < IDX_VALUE_BISECT_ITERS, _f2key(lo_v + 0.5 * (hi_v - lo_v)),
                         (lo_k >> 1) + (hi_k >> 1) + (lo_k & hi_k & 1))
        pk_m = jnp.where(jnp.logical_and(lo_k == 0, hi_k > KEY_MIN_NORMAL), KEY_MIN_NORMAL, pk_m)
        pk_m = jnp.where(jnp.logical_and(lo_k < 0, hi_k > 0), 0, pk_m)
        pk = jnp.where(it % 2 == 0, pk_i, pk_m)
        pk = jnp.minimum(jnp.maximum(pk, lo_k + 1), hi_k - 1)
        pk = jnp.where(done, lo_k, pk)
        x = _key2f(pk)
        c = count(x, ls)
        ge = c >= kf
        act = jnp.logical_not(done)
        up = jnp.logical_and(act, ge)
        dn = jnp.logical_and(act, jnp.logical_not(ge))
        lo_k = jnp.where(up, pk, lo_k)
        clo = jnp.where(up, c, clo)
        hi_k = jnp.where(dn, pk, hi_k)
        chi = jnp.where(dn, c, chi)
        exact = jnp.logical_and(act, c == kf)
        collapsed = jnp.logical_and(act, jnp.logical_or(
            hi_k <= lo_k + 1, jnp.logical_and(lo_k == 0, hi_k <= KEY_MIN_NORMAL)))
        fin = jnp.logical_or(exact, collapsed)
        thr = jnp.where(exact, x, jnp.where(collapsed, _key2f(lo_k), thr))
        cthr = jnp.where(exact, c, jnp.where(collapsed, clo, cthr))
        return it + 1, lo_k, hi_k, clo, chi, thr, jnp.where(jnp.logical_or(done, fin), 1, 0), cthr

    def search(ls):
        n_vis, few, lo_v0, hi_v0 = n_vis_all[:, ls], few_all[:, ls], vmin[:, ls], vmax[:, ls]
        n_meta = jnp.minimum(n_vis, float(N_META))
        lo_k0 = _f2key(lo_v0)
        hi_k0 = _f2key(hi_v0) + 1
        done0 = jnp.logical_or(few, n_vis - n_meta < 0.5)
        thr0 = jnp.full(n_vis.shape, 0.5 * NEG, F32)
        col0 = jnp.logical_and(jnp.logical_not(done0), hi_k0 <= lo_k0 + 1)
        thr0 = jnp.where(col0, lo_v0, thr0)
        st = (jnp.int32(0), lo_k0, hi_k0, n_vis, n_meta, thr0, jnp.where(jnp.logical_or(done0, col0), 1, 0),
              jnp.where(col0, n_vis, kf))
        _, _, _, _, _, thr, _, cthr = lax.while_loop(cond, functools.partial(step, ls), st)
        return jnp.where(few, 0.5 * NEG, thr), cthr

    found = [search(slice(a, a + IDX_LANES)) for a in range(0, tq, IDX_LANES)]
    thr = jnp.concatenate([f[0] for f in found], axis=1)
    cthr = jnp.concatenate([f[1] for f in found], axis=1)
    any_tie = jnp.max(jnp.where(jnp.logical_and(jnp.logical_not(few_all), cthr > kf), 1.0, 0.0)) > 0.5

    @pl.when(jnp.logical_not(any_tie))
    def _():
        def emit(kj, _):
            s = s_ref[pl.ds(pl.multiple_of(kj * tk, tk), tk), :]
            mask_ref[pl.ds(pl.multiple_of(kj * tk, tk), tk), :] = jnp.where(s >= thr, 1, 0).astype(jnp.int8)
            return 0
        lax.fori_loop(0, nk, emit, 0)

    @pl.when(any_tie)
    def _():
        allowed = kf - count(thr, slice(0, tq), strict=True)
        ri = lax.broadcasted_iota(jnp.int32, (tk, tk), 0)
        ci = lax.broadcasted_iota(jnp.int32, (tk, tk), 1)
        before = (ci < ri).astype(BF16)

        def emit(kjs, run):
            tiles = [s_ref[pl.ds(pl.multiple_of(kj * tk, tk), tk), :] for kj in kjs]
            eqfs = [jnp.where(s == thr, 1.0, 0.0) for s in tiles]
            prefix = [jnp.dot(before, e.astype(BF16), preferred_element_type=F32) for e in eqfs]
            for kj, s, e, pre in zip(kjs, tiles, eqfs, prefix):
                sel = jnp.logical_or(s > thr, jnp.logical_and(e > 0.5, run + pre < allowed))
                mask_ref[pl.ds(pl.multiple_of(kj * tk, tk), tk), :] = jnp.where(sel, 1, 0).astype(jnp.int8)
                run = run + jnp.sum(e, axis=0, keepdims=True)
            return run

        run = lax.fori_loop(0, nk // 2, lambda p, r: emit([2 * p, 2 * p + 1], r), jnp.zeros((1, tq), F32))

        @pl.when(nk % 2 == 1)
        def _():
            emit([nk - 1], run)

    def clear(kj, _):
        mask_ref[pl.ds(pl.multiple_of(kj * tk, tk), tk), :] = jnp.zeros((tk, tq), jnp.int8)
        return 0
    lax.fori_loop(nk, nkt, clear, 0)


def index_mask(iq_t, small_t, ki, topk):
    lp = iq_t.shape[1]
    tq = IDX_TQ
    assert IDX_TQ == ROW_TILE
    return pl.pallas_call(
        functools.partial(_index_mask_kernel, topk=topk),
        grid=(lp // tq,),
        in_specs=[pl.BlockSpec((IDX_Q_W, tq), lambda i: (0, i)),
                  pl.BlockSpec((16, tq), lambda i: (0, i)),
                  pl.BlockSpec((lp, IDX_DIM), lambda i: (0, 0), pipeline_mode=pl.Buffered(1))],
        out_specs=pl.BlockSpec((lp, tq), lambda i: (0, i)),
        out_shape=jax.ShapeDtypeStruct((lp, lp), jnp.int8),
        scratch_shapes=[pltpu.VMEM((lp, tq), F32)],
        compiler_params=_cparams(("arbitrary",)),
        name="index_mask",
    )(iq_t, small_t, ki)


ATTN_AHEAD = 2
ONES_ROWS = 16


def _attn_kernel(qi_ref, kj_ref, qt_ref, k_ref, vt_ref, mask_ref, bp_ref, b0_ref, bm_ref, o_ref,
                 m_ref, l_ref, acc_ref, neg_ref):
    tq, tk = Q_TILE, ROW_TILE
    sidx = pl.program_id(0)
    qi = qi_ref[sidx]
    kj = kj_ref[sidx]
    delta = kj - 2 * qi

    @pl.when(kj == 0)
    def _():
        m_ref[...] = jnp.full(m_ref.shape, NEG, F32)
        l_ref[...] = jnp.zeros(l_ref.shape, F32)
        acc_ref[...] = jnp.zeros(acc_ref.shape, F32)

    def tile(bias_ref):
        neg_ref[...] = jnp.where(mask_ref[...].astype(F32) > 0.5, 0.0, NEG)
        ones = jnp.ones((ONES_ROWS, tk), BF16)

        def logits(h):
            hs = slice(h * A_DIM, (h + 1) * A_DIM)
            s = jnp.dot(k_ref[:, hs], qt_ref[hs, :], preferred_element_type=F32) + neg_ref[...]
            if bias_ref is not None:
                s = s + bias_ref[h]
            return s

        ahead = [logits(h) for h in range(min(ATTN_AHEAD, A_HEADS))]
        for h in range(A_HEADS):
            hs = slice(h * A_DIM, (h + 1) * A_DIM)
            if h + ATTN_AHEAD < A_HEADS:
                ahead.append(logits(h + ATTN_AHEAD))
            s = ahead.pop(0)
            m_old = m_ref[h]
            m_new = jnp.maximum(m_old, jnp.max(s, axis=0, keepdims=True))
            alpha = jnp.exp2(m_old - m_new)
            p = jnp.exp2(s - m_new).astype(BF16)
            pv = jnp.dot(jnp.concatenate([vt_ref[hs, :], ones], axis=0), p, preferred_element_type=F32)
            acc_ref[hs, :] = alpha * acc_ref[hs, :] + pv[0:A_DIM, :]
            l_ref[h] = alpha * l_ref[h] + pv[A_DIM:A_DIM + 1, :]
            m_ref[h] = m_new

    pl.when(delta == 1)(lambda: tile(bp_ref))
    pl.when(delta == 0)(lambda: tile(b0_ref))
    pl.when(delta == -1)(lambda: tile(bm_ref))
    pl.when(delta < -1)(lambda: tile(None))

    @pl.when(delta == 1)
    def _():
        for h in range(A_HEADS):
            hs = slice(h * A_DIM, (h + 1) * A_DIM)
            o_ref[:, hs] = (acc_ref[hs, :] / l_ref[h]).T.astype(o_ref.dtype)


def _t5_bucket_np(rel):
    rel = np.maximum(rel, 0)
    max_exact = REL_BUCKETS // 2
    rel_f = np.maximum(rel, 1).astype(np.float32)
    large = max_exact + (np.log(rel_f / np.float32(max_exact)) / np.float32(math.log(REL_MAX_DIST / max_exact))
                         * np.float32(REL_BUCKETS - max_exact)).astype(np.int32)
    large = np.minimum(large, REL_BUCKETS - 1)
    return np.where(rel < max_exact, rel, large)


def attention(q_t, proj, v_t, mask_t, rel_bias):
    lp = proj.shape[0]
    tq, tk = Q_TILE, ROW_TILE
    assert tq == 2 * tk and tk >= REL_MAX_DIST
    nq = lp // tq
    qi_l, kj_l = [], []
    for a in range(nq):
        for b in range(2 * a + 2):
            qi_l.append(a)
            kj_l.append(b)
    qi_arr = jnp.asarray(np.array(qi_l, np.int32))
    kj_arr = jnp.asarray(np.array(kj_l, np.int32))
    table = (rel_bias.astype(F32) - rel_bias[REL_BUCKETS - 1].astype(F32)) * LOG2E

    def btile(d):
        n = tq + tk
        j = np.arange(n)
        diag = np.where(j < tq, j, j - n)
        vec = jnp.take(table, jnp.asarray(_t5_bucket_np(diag - tk * d).astype(np.int32)), axis=0).T
        return jnp.tile(vec, (1, tk))[:, :tk * (n - 1)].reshape(A_HEADS, tk, n - 1)[:, :, :tq]

    ck = COL_AK // A_W
    const3 = lambda s, qi, kj: (0, 0, 0)
    grid_spec = pltpu.PrefetchScalarGridSpec(
        num_scalar_prefetch=2,
        grid=(len(qi_l),),
        in_specs=[pl.BlockSpec((A_W, tq), lambda s, qi, kj: (0, qi[s])),
                  pl.BlockSpec((tk, A_W), lambda s, qi, kj: (kj[s], ck)),
                  pl.BlockSpec((A_W, tk), lambda s, qi, kj: (0, kj[s])),
                  pl.BlockSpec((tk, tq), lambda s, qi, kj: (kj[s], qi[s])),
                  pl.BlockSpec((A_HEADS, tk, tq), const3, pipeline_mode=pl.Buffered(1)),
                  pl.BlockSpec((A_HEADS, tk, tq), const3, pipeline_mode=pl.Buffered(1)),
                  pl.BlockSpec((A_HEADS, tk, tq), const3, pipeline_mode=pl.Buffered(1))],
        out_specs=pl.BlockSpec((tq, A_W), lambda s, qi, kj: (qi[s], 0)),
        scratch_shapes=[pltpu.VMEM((A_HEADS, 1, tq), F32),
                        pltpu.VMEM((A_HEADS, 1, tq), F32),
                        pltpu.VMEM((A_W, tq), F32),
                        pltpu.VMEM((tk, tq), F32)],
    )
    return pl.pallas_call(
        _attn_kernel,
        grid_spec=grid_spec,
        out_shape=jax.ShapeDtypeStruct((lp, A_W), BF16),
        compiler_params=_cparams(("arbitrary",)),
        name="attention",
    )(qi_arr, kj_arr, q_t, proj, v_t, mask_t, btile(1), btile(0), btile(-1))


def _merge_kernel(ym_ref, ya_ref, gm_ref, ga_ref, h_ref, wpm_ref, wpa_ref, wo_ref, g_ref, b_ref,
                  hn_ref, hb_ref, *, alpha):
    pm = jnp.dot(ym_ref[...], wpm_ref[...], preferred_element_type=F32)
    pa = jnp.dot(ya_ref[...], wpa_ref[...], preferred_element_type=F32)
    merged = _sigmoid(gm_ref[...].astype(F32)) * pm + _sigmoid(ga_ref[...].astype(F32)) * pa
    mix = jnp.dot(merged.astype(BF16), wo_ref[...], preferred_element_type=F32)
    y = _ln(alpha * h_ref[...] + mix, g_ref[...], b_ref[...])
    hn_ref[...] = y
    hb_ref[...] = y.astype(BF16)


def merge_out_ln(ym, ya, proj, h, wpm, wpa, wo, g, b, alpha):
    lp = h.shape[0]
    tm = ROW_TILE
    cgm, cga = COL_GM // D_MODEL, COL_GA // D_MODEL
    const = lambda i: (0, 0)
    return pl.pallas_call(
        functools.partial(_merge_kernel, alpha=alpha),
        grid=(lp // tm,),
        in_specs=[pl.BlockSpec((tm, M_V_W), lambda i: (i, 0)),
                  pl.BlockSpec((tm, A_W), lambda i: (i, 0)),
                  pl.BlockSpec((tm, D_MODEL), lambda i: (i, cgm)),
                  pl.BlockSpec((tm, D_MODEL), lambda i: (i, cga)),
                  pl.BlockSpec((tm, D_MODEL), lambda i: (i, 0)),
                  pl.BlockSpec((M_V_W, D_MODEL), const, pipeline_mode=pl.Buffered(1)),
                  pl.BlockSpec((A_W, D_MODEL), const, pipeline_mode=pl.Buffered(1)),
                  pl.BlockSpec((D_MODEL, D_MODEL), const, pipeline_mode=pl.Buffered(1)),
                  pl.BlockSpec((1, D_MODEL), const),
                  pl.BlockSpec((1, D_MODEL), const)],
        out_specs=[pl.BlockSpec((tm, D_MODEL), lambda i: (i, 0)),
                   pl.BlockSpec((tm, D_MODEL), lambda i: (i, 0))],
        out_shape=[jax.ShapeDtypeStruct((lp, D_MODEL), F32),
                   jax.ShapeDtypeStruct((lp, D_MODEL), BF16)],
        compiler_params=_cparams(("parallel",)),
        name="merge_out_ln",
    )(ym, ya, proj, proj, h, wpm, wpa, wo, g.reshape(1, -1), b.reshape(1, -1))


def _router_kernel(h_ref, w_ref, b_ref, e_ref, g_ref):
    lg = jnp.dot(h_ref[...], w_ref[...], preferred_element_type=F32, precision=HI) + b_ref[...]
    tm = lg.shape[0]
    col = lax.broadcasted_iota(jnp.int32, (tm, SM_W), 1).astype(F32)
    far = 1e9

    def first_argmax(x):
        mx = jnp.max(x, axis=-1, keepdims=True)
        return mx, jnp.min(jnp.where(x == mx, col, far), axis=-1, keepdims=True)

    gl = jnp.where(col < N_GROUPS, lg, -jnp.inf)
    gmax, gsel = first_argmax(gl)
    gprob = 1.0 / jnp.sum(jnp.exp(gl - gmax), axis=-1, keepdims=True)
    c0 = N_GROUPS + EXP_PER_GROUP * gsel
    el = jnp.where(jnp.logical_and(col >= c0, col < c0 + EXP_PER_GROUP), lg, -jnp.inf)
    v1, i1 = first_argmax(el)
    v2, i2 = first_argmax(jnp.where(col == i1, -jnp.inf, el))
    e21 = jnp.exp(v2 - v1)
    g1 = gprob / (1.0 + e21)
    g2 = gprob * e21 / (1.0 + e21)
    e_ref[...] = jnp.where(col == 0, i1 - N_GROUPS, jnp.where(col == 1, i2 - N_GROUPS, 0.0)).astype(jnp.int32)
    g_ref[...] = jnp.where(col == 0, g1, jnp.where(col == 1, g2, 0.0))


def router(h, w, b):
    lp = h.shape[0]
    tm = ROW_TILE
    return pl.pallas_call(
        _router_kernel,
        grid=(lp // tm,),
        in_specs=[pl.BlockSpec((tm, D_MODEL), lambda i: (i, 0)),
                  pl.BlockSpec((D_MODEL, SM_W), lambda i: (0, 0)),
                  pl.BlockSpec((1, SM_W), lambda i: (0, 0))],
        out_specs=[pl.BlockSpec((tm, SM_W), lambda i: (i, 0)),
                   pl.BlockSpec((tm, SM_W), lambda i: (i, 0))],
        out_shape=[jax.ShapeDtypeStruct((lp, SM_W), jnp.int32),
                   jax.ShapeDtypeStruct((lp, SM_W), F32)],
        compiler_params=_cparams(("parallel",)),
        name="router",
    )(h, w, b)


def _expert_kernel(blk_e_ref, tok_ref, orow_ref, nused_ref, x_hbm, w_ref, wgu_ref, wd_ref, y_hbm,
                   xbuf, ybuf, wgu_b, wd_b, gsem, ssem):
    b = pl.program_id(0)
    nb = pl.num_programs(0)
    nused = nused_ref[0]
    blk = MOE_BLOCK

    def start_gather(block, slot):
        for r in range(blk):
            tok = tok_ref[block * blk + r]
            pltpu.make_async_copy(x_hbm.at[pl.ds(tok, 1), :], xbuf.at[slot, pl.ds(r, 1), :], gsem.at[slot]).start()

    def wait_gather(slot):
        pltpu.make_async_copy(x_hbm.at[pl.ds(0, blk), :], xbuf.at[slot], gsem.at[slot]).wait()

    def start_scatter(block, slot):
        for r in range(blk):
            dst = orow_ref[block * blk + r]
            pltpu.make_async_copy(ybuf.at[slot, pl.ds(r, 1), :], y_hbm.at[pl.ds(dst, 1), :], ssem.at[slot]).start()

    def wait_scatter(slot):
        pltpu.make_async_copy(ybuf.at[slot], y_hbm.at[pl.ds(0, blk), :], ssem.at[slot]).wait()

    @pl.when(b == 0)
    def _():
        ybuf[1] = jnp.zeros((blk, D_MODEL), F32)
        spare = pltpu.make_async_copy(ybuf.at[1], y_hbm.at[pl.ds(y_hbm.shape[0] - blk, blk), :], ssem.at[1])
        spare.start()
        spare.wait()

    @pl.when(jnp.logical_and(b == 0, nused > 0))
    def _():
        start_gather(0, 0)

    @pl.when(jnp.logical_and(b >= 2, b - 2 < nused))
    def _():
        wait_scatter(b % 2)

    @pl.when(b < nused)
    def _():
        changed = jnp.logical_or(b == 0, blk_e_ref[b] != blk_e_ref[jnp.maximum(b - 1, 0)])

        @pl.when(changed)
        def _():
            wgu_b[...] = wgu_ref[0].astype(BF16)
            wd_b[...] = wd_ref[0].astype(BF16)

    @pl.when(b < nused)
    def _():
        slot = b % 2
        wait_gather(slot)
        xb = xbuf[slot].astype(BF16)
        start_gather(b + 1, 1 - slot)
        gu = jnp.dot(xb, wgu_b[...], preferred_element_type=F32)
        g = gu[:, :D_FF]
        hb = (g * _sigmoid(g)) * gu[:, D_FF:]
        ybuf[slot] = jnp.dot(hb.astype(BF16), wd_b[...], preferred_element_type=F32) * w_ref[...]
        start_scatter(b, slot)

    @pl.when(jnp.logical_and(b == nused, nused > 0))
    def _():
        wait_gather(b % 2)

    @pl.when(b == nb - 1)
    def _():
        @pl.when(nused == nb)
        def _():
            wait_gather(nb % 2)

        @pl.when(nb - 2 < nused)
        def _():
            wait_scatter(nb % 2)

        @pl.when(nb - 1 < nused)
        def _():
            wait_scatter((nb - 1) % 2)


def expert_ffn(h, blk_e, row_tok, out_row, nused, row_w, wgu, wd):
    lp = h.shape[0]
    n_blocks = out_row.shape[0] // MOE_BLOCK
    assert row_tok.shape[0] == (n_blocks + 1) * MOE_BLOCK and n_blocks >= 2
    grid_spec = pltpu.PrefetchScalarGridSpec(
        num_scalar_prefetch=4,
        grid=(n_blocks,),
        in_specs=[pl.BlockSpec(memory_space=pl.ANY),
                  pl.BlockSpec((MOE_BLOCK, 1), lambda b, e, t, s, n: (b, 0)),
                  pl.BlockSpec((1, D_MODEL, 2 * D_FF), lambda b, e, t, s, n: (e[b], 0, 0)),
                  pl.BlockSpec((1, D_FF, D_MODEL), lambda b, e, t, s, n: (e[b], 0, 0))],
        out_specs=pl.BlockSpec(memory_space=pl.ANY),
        scratch_shapes=[pltpu.VMEM((2, MOE_BLOCK, D_MODEL), F32),
                        pltpu.VMEM((2, MOE_BLOCK, D_MODEL), F32),
                        pltpu.VMEM((D_MODEL, 2 * D_FF), BF16),
                        pltpu.VMEM((D_FF, D_MODEL), BF16),
                        pltpu.SemaphoreType.DMA((2,)),
                        pltpu.SemaphoreType.DMA((2,))],
    )
    return pl.pallas_call(
        _expert_kernel,
        grid_spec=grid_spec,
        out_shape=jax.ShapeDtypeStruct((2 * lp + MOE_BLOCK, D_MODEL), F32),
        compiler_params=_cparams(("arbitrary",)),
        name="expert_ffn",
    )(blk_e, row_tok, out_row, nused, h, row_w, wgu, wd)


def moe_dispatch(e_ids, gates):
    n_assign = e_ids.shape[0] * 2
    e_flat = e_ids.reshape(n_assign)
    w_flat = gates.reshape(n_assign)
    cb = ROW_TILE
    onehot = (e_flat[:, None] == jnp.arange(N_EXPERTS, dtype=jnp.int32)[None, :]).astype(F32)
    onehot = onehot.reshape(n_assign // cb, cb, N_EXPERTS)
    tri = jnp.asarray(np.tril(np.ones((cb, cb), np.float32)))
    within = jnp.einsum('ij,bjk->bik', tri, onehot, precision=HI)
    block_tot = jnp.sum(onehot, axis=1)
    block_off = jnp.cumsum(block_tot, axis=0) - block_tot
    csum = within + block_off[:, None, :]
    rank = (jnp.sum(onehot * csum, axis=2) - 1.0).astype(jnp.int32).reshape(n_assign)
    counts = jnp.sum(block_tot, axis=0).astype(jnp.int32)
    padded = (counts + MOE_BLOCK - 1) // MOE_BLOCK * MOE_BLOCK
    pend = jnp.cumsum(padded)
    pstart = pend - padded
    dest = pstart[e_flat] + rank
    n_blocks = n_assign // MOE_BLOCK + N_EXPERTS
    n_rows = n_blocks * MOE_BLOCK
    row_slot = jnp.full((n_rows,), -1, jnp.int32).at[dest].set(jnp.arange(n_assign, dtype=jnp.int32))
    valid = row_slot >= 0
    slot = jnp.maximum(row_slot, 0)
    n_tok = n_assign // 2
    row_tok = jnp.concatenate([slot >> 1, jnp.zeros((MOE_BLOCK,), jnp.int32)])
    out_row = jnp.where(valid, (slot & 1) * n_tok + (slot >> 1),
                        n_assign + jnp.arange(n_rows, dtype=jnp.int32) % MOE_BLOCK)
    row_w = jnp.where(valid, w_flat[slot], 0.0).reshape(n_rows, 1)
    blk_start = jnp.arange(n_blocks, dtype=jnp.int32) * MOE_BLOCK
    blk_e = jnp.minimum(jnp.sum((pend[None, :] <= blk_start[:, None]).astype(jnp.int32), axis=1), N_EXPERTS - 1)
    nused = (pend[-1] // MOE_BLOCK).astype(jnp.int32).reshape(1)
    return blk_e, row_tok, out_row, nused, row_w


def _combine_kernel(h_ref, y0_ref, y1_ref, g_ref, b_ref, hn_ref, hb_ref, *, alpha):
    ffn = y0_ref[...] + y1_ref[...]
    y = _ln(alpha * h_ref[...] + ffn, g_ref[...], b_ref[...])
    hn_ref[...] = y
    hb_ref[...] = y.astype(BF16)


def combine_ln(h, y2, g, b, alpha):
    lp = h.shape[0]
    tm = ROW_TILE
    half = lp // tm
    return pl.pallas_call(
        functools.partial(_combine_kernel, alpha=alpha),
        grid=(lp // tm,),
        in_specs=[pl.BlockSpec((tm, D_MODEL), lambda i: (i, 0)),
                  pl.BlockSpec((tm, D_MODEL), lambda i: (i, 0)),
                  pl.BlockSpec((tm, D_MODEL), lambda i: (i + half, 0)),
                  pl.BlockSpec((1, D_MODEL), lambda i: (0, 0)),
                  pl.BlockSpec((1, D_MODEL), lambda i: (0, 0))],
        out_specs=[pl.BlockSpec((tm, D_MODEL), lambda i: (i, 0)),
                   pl.BlockSpec((tm, D_MODEL), lambda i: (i, 0))],
        out_shape=[jax.ShapeDtypeStruct((lp, D_MODEL), F32),
                   jax.ShapeDtypeStruct((lp, D_MODEL), BF16)],
        compiler_params=_cparams(("parallel",)),
        name="combine_ln",
    )(h, y2, y2, g.reshape(1, -1), b.reshape(1, -1))


TOK_TILE = 256


def _rows_to_experts_kernel(dest_ref, h_ref, xs_init_ref, xs_hbm, sem):
    del xs_init_ref
    i = pl.program_id(0)
    for r in range(TOK_TILE):
        for j in range(2):
            d = dest_ref[(i * TOK_TILE + r) * 2 + j]
            pltpu.make_async_copy(h_ref.at[pl.ds(r, 1), :], xs_hbm.at[pl.ds(d, 1), :], sem.at[0]).start()
    for _ in range(2):
        pltpu.make_async_copy(h_ref, xs_hbm.at[pl.ds(0, TOK_TILE), :], sem.at[0]).wait()


def rows_to_experts(h, dest, n_rows):
    lp = h.shape[0]
    grid_spec = pltpu.PrefetchScalarGridSpec(
        num_scalar_prefetch=1,
        grid=(lp // TOK_TILE,),
        in_specs=[pl.BlockSpec((TOK_TILE, D_MODEL), lambda i, d: (i, 0)),
                  pl.BlockSpec(memory_space=pl.ANY)],
        out_specs=pl.BlockSpec(memory_space=pl.ANY),
        scratch_shapes=[pltpu.SemaphoreType.DMA((1,))],
    )
    return pl.pallas_call(
        _rows_to_experts_kernel,
        grid_spec=grid_spec,
        out_shape=jax.ShapeDtypeStruct((n_rows, D_MODEL), F32),
        input_output_aliases={2: 0},
        compiler_params=_cparams(("arbitrary",)),
        name="rows_to_experts",
    )(dest, h, jnp.zeros((n_rows, D_MODEL), F32))


def _expert_blocks_kernel(blk_e_ref, nused_ref, xs_ref, wgu_ref, wd_ref, ys_ref, wgu_b, wd_b):
    b = pl.program_id(0)
    nused = nused_ref[0]

    @pl.when(b < nused)
    def _():
        changed = jnp.logical_or(b == 0, blk_e_ref[b] != blk_e_ref[jnp.maximum(b - 1, 0)])

        @pl.when(changed)
        def _():
            wgu_b[...] = wgu_ref[0].astype(BF16)
            wd_b[...] = wd_ref[0].astype(BF16)

        gu = jnp.dot(xs_ref[...].astype(BF16), wgu_b[...], preferred_element_type=F32)
        g = gu[:, :D_FF]
        hb = (g * _sigmoid(g)) * gu[:, D_FF:]
        ys_ref[...] = jnp.dot(hb.astype(BF16), wd_b[...], preferred_element_type=F32)

    @pl.when(b >= nused)
    def _():
        ys_ref[...] = jnp.zeros(ys_ref.shape, F32)


def expert_blocks(xs, blk_e, nused, wgu, wd):
    n_blocks = xs.shape[0] // MOE_BLOCK
    last = lambda b, n: jnp.minimum(b, jnp.maximum(n[0] - 1, 0))
    grid_spec = pltpu.PrefetchScalarGridSpec(
        num_scalar_prefetch=2,
        grid=(n_blocks,),
        in_specs=[pl.BlockSpec((MOE_BLOCK, D_MODEL), lambda b, e, n: (last(b, n), 0)),
                  pl.BlockSpec((1, D_MODEL, 2 * D_FF), lambda b, e, n: (e[b], 0, 0)),
                  pl.BlockSpec((1, D_FF, D_MODEL), lambda b, e, n: (e[b], 0, 0))],
        out_specs=pl.BlockSpec((MOE_BLOCK, D_MODEL), lambda b, e, n: (b, 0)),
        scratch_shapes=[pltpu.VMEM((D_MODEL, 2 * D_FF), BF16),
                        pltpu.VMEM((D_FF, D_MODEL), BF16)],
    )
    return pl.pallas_call(
        _expert_blocks_kernel,
        grid_spec=grid_spec,
        out_shape=jax.ShapeDtypeStruct(xs.shape, F32),
        compiler_params=_cparams(("arbitrary",)),
        name="expert_blocks",
    )(blk_e, nused, xs, wgu, wd)


def _gather_combine_kernel(dest_ref, h_ref, gate_ref, g_ref, b_ref, ys_hbm, hn_ref, hb_ref, ybuf, sem, *, alpha):
    i = pl.program_id(0)
    n = pl.num_programs(0)

    def start_gather(tile, slot):
        for r in range(TOK_TILE):
            for j in range(2):
                d = dest_ref[(tile * TOK_TILE + r) * 2 + j]
                pltpu.make_async_copy(ys_hbm.at[pl.ds(d, 1), :], ybuf.at[slot, pl.ds(j * TOK_TILE + r, 1), :],
                                      sem.at[slot]).start()

    def wait_gather(slot):
        pltpu.make_async_copy(ys_hbm.at[pl.ds(0, 2 * TOK_TILE), :], ybuf.at[slot], sem.at[slot]).wait()

    @pl.when(i == 0)
    def _():
        start_gather(0, 0)

    slot = i % 2
    start_gather(i + 1, 1 - slot)
    wait_gather(slot)
    ffn = (gate_ref[:, 0:1] * ybuf[slot, 0:TOK_TILE, :] + gate_ref[:, 1:2] * ybuf[slot, TOK_TILE:2 * TOK_TILE, :])
    y = _ln(alpha * h_ref[...] + ffn, g_ref[...], b_ref[...])
    hn_ref[...] = y
    hb_ref[...] = y.astype(BF16)

    @pl.when(i == n - 1)
    def _():
        wait_gather(1 - slot)


def gather_combine_ln(h, ys, dest_ext, gates, g, b, alpha):
    lp = h.shape[0]
    assert dest_ext.shape[0] == 2 * (lp + TOK_TILE)
    grid_spec = pltpu.PrefetchScalarGridSpec(
        num_scalar_prefetch=1,
        grid=(lp // TOK_TILE,),
        in_specs=[pl.BlockSpec((TOK_TILE, D_MODEL), lambda i, d: (i, 0)),
                  pl.BlockSpec((TOK_TILE, SM_W), lambda i, d: (i, 0)),
                  pl.BlockSpec((1, D_MODEL), lambda i, d: (0, 0)),
                  pl.BlockSpec((1, D_MODEL), lambda i, d: (0, 0)),
                  pl.BlockSpec(memory_space=pl.ANY)],
        out_specs=[pl.BlockSpec((TOK_TILE, D_MODEL), lambda i, d: (i, 0)),
                   pl.BlockSpec((TOK_TILE, D_MODEL), lambda i, d: (i, 0))],
        scratch_shapes=[pltpu.VMEM((2, 2 * TOK_TILE, D_MODEL), F32),
                        pltpu.SemaphoreType.DMA((2,))],
    )
    return pl.pallas_call(
        functools.partial(_gather_combine_kernel, alpha=alpha),
        grid_spec=grid_spec,
        out_shape=[jax.ShapeDtypeStruct((lp, D_MODEL), F32),
                   jax.ShapeDtypeStruct((lp, D_MODEL), BF16)],
        compiler_params=_cparams(("arbitrary",)),
        name="gather_combine_ln",
    )(dest_ext, h, gates, g.reshape(1, -1), b.reshape(1, -1), ys)


def moe_plan(e_ids):
    n_assign = e_ids.shape[0] * 2
    e_flat = e_ids.reshape(n_assign)
    cb = ROW_TILE
    onehot = (e_flat[:, None] == jnp.arange(N_EXPERTS, dtype=jnp.int32)[None, :]).astype(F32)
    onehot = onehot.reshape(n_assign // cb, cb, N_EXPERTS)
    tri = jnp.asarray(np.tril(np.ones((cb, cb), np.float32)))
    within = jnp.einsum('ij,bjk->bik', tri, onehot, precision=HI)
    block_tot = jnp.sum(onehot, axis=1)
    block_off = jnp.cumsum(block_tot, axis=0) - block_tot
    csum = within + block_off[:, None, :]
    rank = (jnp.sum(onehot * csum, axis=2) - 1.0).astype(jnp.int32).reshape(n_assign)
    counts = jnp.sum(block_tot, axis=0).astype(jnp.int32)
    padded = (counts + MOE_BLOCK - 1) // MOE_BLOCK * MOE_BLOCK
    pend = jnp.cumsum(padded)
    pstart = pend - padded
    dest = jnp.sum(onehot.reshape(n_assign, N_EXPERTS) * pstart.astype(F32)[None, :], axis=1).astype(jnp.int32) + rank
    n_blocks = n_assign // MOE_BLOCK + N_EXPERTS
    blk_start = jnp.arange(n_blocks, dtype=jnp.int32) * MOE_BLOCK
    blk_e = jnp.minimum(jnp.sum((pend[None, :] <= blk_start[:, None]).astype(jnp.int32), axis=1), N_EXPERTS - 1)
    nused = (pend[-1] // MOE_BLOCK).astype(jnp.int32).reshape(1)
    return dest, blk_e, nused, n_blocks * MOE_BLOCK


def _split_w_in(w):
    off = {}
    o = 0
    for name, width in (("mq", M_QK_W), ("mk", M_QK_W), ("mv", M_V_W), ("mo", M_V_W), ("mi", M_HEADS),
                        ("mf", M_HEADS), ("aq", A_W), ("ak", A_W), ("av", A_W), ("iq", IDX_Q_W),
                        ("ik", IDX_DIM), ("iw", IDX_HEADS), ("gm", D_MODEL), ("ga", D_MODEL)):
        off[name] = (o, o + width)
        o += width
    seg = lambda n: w[:, off[n][0]:off[n][1]]
    big = jnp.concatenate([seg("gm"), seg("ga"), seg("mv"), seg("mo"), seg("aq") * (A_DIM ** -0.5 * LOG2E),
                           seg("ak"), seg("av"), seg("mq"), seg("mk"), seg("iq") * (IDX_DIM ** -0.5)],
                          axis=1).astype(BF16)
    small = jnp.concatenate([seg("mi"), seg("mf"), seg("iw") * (IDX_HEADS ** -0.5),
                             jnp.zeros((w.shape[0], SM_W - 2 * M_HEADS - IDX_HEADS - IDX_DIM), F32),
                             seg("ik")], axis=1)
    return big, small


def _layer(h, hb, p, rel_bias, topk, alpha):
    w_big, w_small = _split_w_in(p["w_in"])
    proj = in_proj_big(hb, w_big)
    small = in_proj_small(h, w_small)
    small_t = small[:, 0:16].T
    ki = small[:, SM_W - IDX_DIM:].astype(BF16)
    q_t = proj[:, COL_AQ:COL_AQ + A_W].T
    v_t = proj[:, COL_AV:COL_AV + A_W].T
    iq_t = proj[:, COL_IQ:COL_IQ + IDX_Q_W].T
    y_m = mlstm(proj, small, small_t, p["conv_w"], p["conv_b"], p["b_igate"], p["b_fgate"], p["mh_norm_g"])
    mask_t = index_mask(iq_t, small_t, ki, topk)
    y_a = attention(q_t, proj, v_t, mask_t, rel_bias)
    h, hb = merge_out_ln(y_m, y_a, proj, h, p["w_proj_m"].astype(BF16), p["w_proj_a"].astype(BF16),
                         p["w_out"].astype(BF16), p["ln1_g"], p["ln1_b"], alpha)
    w_r = jnp.concatenate([p["w_group"], p["w_router"],
                           jnp.zeros((D_MODEL, SM_W - N_GROUPS - N_EXPERTS), F32)], axis=1)
    b_r = jnp.concatenate([p["b_group"], p["b_router"],
                           jnp.zeros((SM_W - N_GROUPS - N_EXPERTS,), F32)]).reshape(1, SM_W)
    e_out, g_out = router(h, w_r, b_r)
    dest, blk_e, nused, n_rows = moe_plan(e_out[:, 0:2])
    xs = rows_to_experts(h, dest, n_rows)
    ys = expert_blocks(xs, blk_e, nused, p["w_gate_up"], p["w_down"])
    dest_ext = jnp.concatenate([dest, jnp.zeros((2 * TOK_TILE,), jnp.int32)])
    return gather_combine_ln(h, ys, dest_ext, g_out, p["ln2_g"], p["ln2_b"], alpha)


def _trunk(x2, meta_tokens, ln_emb_g, ln_emb_b, rel_bias, layers):
    depth = len(layers)
    alpha = (2 * depth) ** 0.25
    seq = x2.shape[0]
    length = N_META + seq
    lp = -(-length // PAD_TO) * PAD_TO
    topk = min(TOPK_MAX, length // 4)
    xp = jnp.concatenate([meta_tokens.astype(x2.dtype), x2, jnp.zeros((lp - length, D_MODEL), x2.dtype)], axis=0)
    h, hb = embed_ln(xp, ln_emb_g, ln_emb_b)
    for p in layers:
        h, hb = _layer(h, hb, p, rel_bias, topk, alpha)
    return h[N_META:length]


def kernel(x, meta_tokens, ln_emb_g, ln_emb_b, rel_bias, w_in, conv_w, conv_b, b_igate, b_fgate, mh_norm_g,
           w_proj_m, w_proj_a, w_out, ln1_g, ln1_b, w_group, b_group, w_router, b_router, w_gate_up, w_down,
           ln2_g, ln2_b):
    depth = w_in.shape[0]
    layers = [dict(w_in=w_in[l], conv_w=conv_w[l], conv_b=conv_b[l], b_igate=b_igate[l], b_fgate=b_fgate[l],
                   mh_norm_g=mh_norm_g[l], w_proj_m=w_proj_m[l], w_proj_a=w_proj_a[l], w_out=w_out[l],
                   ln1_g=ln1_g[l], ln1_b=ln1_b[l], w_group=w_group[l], b_group=b_group[l],
                   w_router=w_router[l], b_router=b_router[l], w_gate_up=w_gate_up[l], w_down=w_down[l],
                   ln2_g=ln2_g[l], ln2_b=ln2_b[l]) for l in range(depth)]
    outs = [_trunk(x[b], meta_tokens, ln_emb_g, ln_emb_b, rel_bias, layers) for b in range(x.shape[0])]
    return jnp.stack(outs, axis=0)
```

```python
import functools
import math

import jax
import jax.numpy as jnp
import numpy as np
from jax import lax
from jax.experimental import pallas as pl
from jax.experimental.pallas import tpu as pltpu

D_MODEL = 2048
N_META = 16
M_HEADS = 4
M_QK = 128
M_V = 256
CONV_W = 4
A_HEADS = 8
A_DIM = 128
TOPK_MAX = 256
IDX_HEADS = 8
IDX_DIM = 64
REL_BUCKETS = 32
REL_MAX_DIST = 128
N_GROUPS = 4
EXP_PER_GROUP = 8
N_EXPERTS = 32
D_FF = 512
MOE_BLOCK = 128
LN_EPS = 1e-5
LOG2E = math.log2(math.e)
NEG = -1e30
BIG = 1e30

M_QK_W = M_HEADS * M_QK
M_V_W = M_HEADS * M_V
A_W = A_HEADS * A_DIM
IDX_Q_W = IDX_HEADS * IDX_DIM

COL_GM = 0
COL_GA = 2048
COL_MV = 4096
COL_MO = 5120
COL_AQ = 6144
COL_AK = 7168
COL_AV = 8192
COL_MQK = 9216
COL_IQ = 10240
BIG_W = 10752
SM_W = 128

ROW_TILE = 256
Q_TILE = 512
PAD_TO = 512
F32 = jnp.float32
BF16 = jnp.bfloat16
VMEM_LIMIT = 56 * 1024 * 1024
HI = lax.Precision.HIGHEST


def _cparams(sem):
    return pltpu.CompilerParams(dimension_semantics=sem, vmem_limit_bytes=VMEM_LIMIT)


def _pick(n, cands):
    for c in cands:
        if n % c == 0:
            return c
    raise ValueError(f"no tile for {n}")


def _ln(x, g, b):
    mu = jnp.mean(x, axis=-1, keepdims=True)
    xc = x - mu
    var = jnp.mean(xc * xc, axis=-1, keepdims=True)
    return xc * lax.rsqrt(var + LN_EPS) * g + b


def _sigmoid(x):
    return 1.0 / (1.0 + jnp.exp(-x))


def _dot_split3(a, b):
    a_hi = a.astype(BF16)
    b_hi = b.astype(BF16)
    a_lo = (a - a_hi.astype(F32)).astype(BF16)
    b_lo = (b - b_hi.astype(F32)).astype(BF16)
    dot = functools.partial(jnp.dot, preferred_element_type=F32)
    return dot(a_hi, b_hi) + (dot(a_hi, b_lo) + dot(a_lo, b_hi))


def _embed_ln_kernel(x_ref, g_ref, b_ref, h_ref, hb_ref):
    y = _ln(x_ref[...], g_ref[...], b_ref[...])
    h_ref[...] = y
    hb_ref[...] = y.astype(BF16)


def embed_ln(xp, g, b):
    lp = xp.shape[0]
    tm = ROW_TILE
    return pl.pallas_call(
        _embed_ln_kernel,
        grid=(lp // tm,),
        in_specs=[pl.BlockSpec((tm, D_MODEL), lambda i: (i, 0)),
                  pl.BlockSpec((1, D_MODEL), lambda i: (0, 0)),
                  pl.BlockSpec((1, D_MODEL), lambda i: (0, 0))],
        out_specs=[pl.BlockSpec((tm, D_MODEL), lambda i: (i, 0)),
                   pl.BlockSpec((tm, D_MODEL), lambda i: (i, 0))],
        out_shape=[jax.ShapeDtypeStruct((lp, D_MODEL), F32),
                   jax.ShapeDtypeStruct((lp, D_MODEL), BF16)],
        compiler_params=_cparams(("parallel",)),
        name="embed_ln",
    )(xp, g.reshape(1, -1), b.reshape(1, -1))


def _mm_kernel(a_ref, b_ref, o_ref):
    o_ref[...] = jnp.dot(a_ref[...], b_ref[...], preferred_element_type=F32).astype(o_ref.dtype)


def in_proj_big(hb, w):
    lp = hb.shape[0]
    tm = _pick(lp, (1536, 1280, 1024, 768, 512))
    tn = 1536
    return pl.pallas_call(
        _mm_kernel,
        grid=(BIG_W // tn, lp // tm),
        in_specs=[pl.BlockSpec((tm, D_MODEL), lambda j, i: (i, 0)),
                  pl.BlockSpec((D_MODEL, tn), lambda j, i: (0, j))],
        out_specs=pl.BlockSpec((tm, tn), lambda j, i: (i, j)),
        out_shape=jax.ShapeDtypeStruct((lp, BIG_W), BF16),
        compiler_params=_cparams(("parallel", "parallel")),
        name="in_proj_big",
    )(hb, w)


def _mm_hi_kernel(a_ref, b_ref, o_ref):
    o_ref[...] = _dot_split3(a_ref[...], b_ref[...])


def in_proj_small(h, w):
    lp = h.shape[0]
    tm = ROW_TILE
    return pl.pallas_call(
        _mm_hi_kernel,
        grid=(lp // tm,),
        in_specs=[pl.BlockSpec((tm, D_MODEL), lambda i: (i, 0)),
                  pl.BlockSpec((D_MODEL, SM_W), lambda i: (0, 0))],
        out_specs=pl.BlockSpec((tm, SM_W), lambda i: (i, 0)),
        out_shape=jax.ShapeDtypeStruct((lp, SM_W), F32),
        compiler_params=_cparams(("parallel",)),
        name="in_proj_small",
    )(h, w)


def _log_sigmoid(x):
    return jnp.minimum(x, 0.0) - jnp.log1p(jnp.exp(-jnp.abs(x)))


def _mlstm_kernel(qk_ref, v_ref, o_ref, sm_ref, smt_ref, cw_ref, cb_ref, gb_ref, gbt_ref, mhg_ref,
                  y_ref, ubuf, c_ref, n_ref, m_ref):
    t = ROW_TILE
    step = pl.program_id(0)

    @pl.when(step == 0)
    def _():
        ubuf[0:8, :] = jnp.zeros((8, 2 * M_QK_W), F32)
        c_ref[...] = jnp.zeros_like(c_ref)
        n_ref[...] = jnp.zeros_like(n_ref)
        m_ref[...] = jnp.zeros_like(m_ref)

    ubuf[8:8 + t, :] = qk_ref[...].astype(F32)
    conv = cb_ref[...]
    for j in range(CONV_W):
        d = CONV_W - 1 - j
        conv = conv + cw_ref[j:j + 1, :] * ubuf[8 - d:8 - d + t, :]
    ubuf[0:8, :] = ubuf[t:t + 8, :]
    qk = conv * _sigmoid(conv)

    gcol = sm_ref[...] + gb_ref[...]
    grow = smt_ref[0:8, :] + gbt_ref[...]
    lane = lax.broadcasted_iota(jnp.int32, (t, SM_W), 1)
    gcol = jnp.where(lane < M_HEADS, gcol, _log_sigmoid(gcol))
    srow = lax.broadcasted_iota(jnp.int32, (8, t), 0)
    grow = jnp.where(srow < M_HEADS, grow, _log_sigmoid(grow))
    ri = lax.broadcasted_iota(jnp.int32, (t, t), 0)
    ci = lax.broadcasted_iota(jnp.int32, (t, t), 1)
    causal = ci <= ri
    tri = causal.astype(F32)
    bcol = jnp.dot(tri, gcol, preferred_element_type=F32, precision=HI)
    brow = lax.dot_general(grow, tri, (((1,), (1,)), ((), ())), preferred_element_type=F32,
                           precision=HI)

    v_all = v_ref[...]
    sig_o = _sigmoid(o_ref[...].astype(F32))
    qs = [qk[:, h * M_QK:(h + 1) * M_QK] * (M_QK ** -0.5) for h in range(M_HEADS)]
    ks = [qk[:, M_QK_W + h * M_QK:M_QK_W + (h + 1) * M_QK] for h in range(M_HEADS)]
    qbs = [q.astype(BF16) for q in qs]
    kbs = [k.astype(BF16) for k in ks]
    c_olds = [c_ref[h] for h in range(M_HEADS)]
    s_all = [lax.dot_general(qbs[h], kbs[h], (((1,), (1,)), ((), ())), preferred_element_type=F32)
             for h in range(M_HEADS)]
    qc_all = [jnp.dot(qbs[h], c_olds[h].astype(BF16), preferred_element_type=F32) for h in range(M_HEADS)]
    for h in range(M_HEADS):
        q, k, qb, kb = qs[h], ks[h], qbs[h], kbs[h]
        v = v_all[:, h * M_V:(h + 1) * M_V]
        b_c = bcol[:, M_HEADS + h:M_HEADS + h + 1]
        ig_c = gcol[:, h:h + 1]
        b_r = brow[M_HEADS + h:M_HEADS + h + 1, :]
        ig_r = grow[h:h + 1, :]
        m_prev = m_ref[h, 0:1, 0:1]
        dmat = jnp.where(causal, b_c + (ig_r - b_r), NEG)
        inter = b_c + m_prev
        m_t = jnp.maximum(jnp.max(dmat, axis=-1, keepdims=True), inter)
        w_intra = jnp.exp(dmat - m_t) * s_all[h]
        w_inter = jnp.exp(inter - m_t)
        c_old = c_olds[h]
        n_old = n_ref[h, 0:1, :]
        num = jnp.dot(w_intra.astype(BF16), v, preferred_element_type=F32) + w_inter * qc_all[h]
        den = (jnp.sum(w_intra, axis=-1, keepdims=True)
               + w_inter * jnp.sum(q * n_old, axis=-1, keepdims=True))
        hh = num / jnp.maximum(jnp.abs(den), jnp.exp(-m_t))
        mu = jnp.mean(hh, axis=-1, keepdims=True)
        hc = hh - mu
        var = jnp.mean(hc * hc, axis=-1, keepdims=True)
        hn = hc * lax.rsqrt(var + LN_EPS) * mhg_ref[:, h * M_V:(h + 1) * M_V]
        y_ref[:, h * M_V:(h + 1) * M_V] = (sig_o[:, h * M_V:(h + 1) * M_V] * hn).astype(y_ref.dtype)
        b_last = b_c[t - 1:t, :]
        dec = b_last - b_c + ig_c
        m_new = jnp.maximum(b_last + m_prev, jnp.max(dec, axis=0, keepdims=True))
        w_src = jnp.exp(dec - m_new)
        cs = jnp.exp(b_last + m_prev - m_new)
        wv = (w_src * v.astype(F32)).astype(BF16)
        c_ref[h] = cs * c_old + jnp.dot(k.T.astype(BF16), wv, preferred_element_type=F32)
        n_new = cs * n_old + jnp.sum(w_src * k, axis=0, keepdims=True)
        n_ref[h] = jnp.broadcast_to(n_new, (8, M_QK))
        m_ref[h] = jnp.broadcast_to(m_new, (8, 128))


def mlstm(proj, small, small_t, conv_w, conv_b, b_ig, b_fg, mh_g):
    lp = proj.shape[0]
    t = ROW_TILE
    gb = jnp.zeros((1, SM_W), F32).at[0, 0:M_HEADS].set(b_ig).at[0, M_HEADS:2 * M_HEADS].set(b_fg)
    gbt = gb[0, 0:8].reshape(8, 1)
    cq = COL_MQK // (2 * M_QK_W)
    cv = COL_MV // M_V_W
    co = COL_MO // M_V_W
    return pl.pallas_call(
        _mlstm_kernel,
        grid=(lp // t,),
        in_specs=[pl.BlockSpec((t, 2 * M_QK_W), lambda i: (i, cq)),
                  pl.BlockSpec((t, M_V_W), lambda i: (i, cv)),
                  pl.BlockSpec((t, M_V_W), lambda i: (i, co)),
                  pl.BlockSpec((t, SM_W), lambda i: (i, 0)),
                  pl.BlockSpec((16, t), lambda i: (0, i)),
                  pl.BlockSpec((CONV_W, 2 * M_QK_W), lambda i: (0, 0)),
                  pl.BlockSpec((1, 2 * M_QK_W), lambda i: (0, 0)),
                  pl.BlockSpec((1, SM_W), lambda i: (0, 0)),
                  pl.BlockSpec((8, 1), lambda i: (0, 0)),
                  pl.BlockSpec((1, M_V_W), lambda i: (0, 0))],
        out_specs=pl.BlockSpec((t, M_V_W), lambda i: (i, 0)),
        out_shape=jax.ShapeDtypeStruct((lp, M_V_W), BF16),
        scratch_shapes=[pltpu.VMEM((t + 8, 2 * M_QK_W), F32),
                        pltpu.VMEM((M_HEADS, M_QK, M_V), F32),
                        pltpu.VMEM((M_HEADS, 8, M_QK), F32),
                        pltpu.VMEM((M_HEADS, 8, 128), F32)],
        compiler_params=_cparams(("arbitrary",)),
        name="mlstm",
    )(proj, proj, proj, small, small_t, conv_w, conv_b.reshape(1, -1), gb, gbt, mh_g.reshape(1, -1))


def _f2key(f):
    b = pltpu.bitcast(f, jnp.int32)
    return jnp.where(b >= 0, b, b ^ jnp.int32(0x7FFFFFFF))


def _key16_to_f(k):
    return pltpu.bitcast(jnp.where(k >= 0, k, k ^ jnp.int32(0x7FFF)) << 16, F32)


def _key2f(k):
    return pltpu.bitcast(jnp.where(k >= 0, k, k ^ jnp.int32(0x7FFFFFFF)), F32)


IDX_MAXIT = 128
IDX_COARSE_MAXIT = 8
IDX_VALUE_BISECT_ITERS = 48
KEY_MIN_NORMAL = 0x00800000
IDX_TQ = 256
IDX_LANES = 256


def _col_reduce(x, op):
    n = x.shape[0] // 8
    parts = [x[g * 8:(g + 1) * 8, :] for g in range(n)]
    while len(parts) > 1:
        nxt = [op(parts[a], parts[a + 1]) for a in range(0, len(parts) - 1, 2)]
        if len(parts) % 2:
            nxt.append(parts[-1])
        parts = nxt
    return parts[0]


def _index_mask_kernel(iqt_ref, wt_ref, ki_ref, mask_ref, s_ref, sb_ref, *, topk):
    tq = IDX_TQ
    tk = ROW_TILE
    i = pl.program_id(0)
    nk = i + 1
    nkt = pl.num_programs(0)
    q_pos = i * tq + lax.broadcasted_iota(jnp.int32, (1, tq), 1)
    k_iota = lax.broadcasted_iota(jnp.int32, (tk, 1), 0)

    def raw_scores(kj):
        kt = ki_ref[pl.ds(pl.multiple_of(kj * tk, tk), tk), :]
        acc = jnp.zeros((tk, tq), F32)
        for h in range(IDX_HEADS):
            x = jnp.dot(kt, iqt_ref[h * IDX_DIM:(h + 1) * IDX_DIM, :], preferred_element_type=F32)
            acc = acc + wt_ref[8 + h:9 + h, :] * jnp.maximum(x, 0.0)
        return acc

    def edge_tile(kj, vmin, vmax):
        acc = raw_scores(kj)
        key_pos = kj * tk + k_iota
        visible = key_pos <= q_pos
        ordinary = jnp.logical_and(visible, key_pos >= N_META)
        s = jnp.where(visible, jnp.where(key_pos < N_META, BIG, acc), NEG)
        s_ref[pl.ds(pl.multiple_of(kj * tk, tk), tk), :] = s
        sb_ref[pl.ds(pl.multiple_of(kj * tk, tk), tk), :] = s.astype(BF16)
        vmin = jnp.minimum(vmin, _col_reduce(jnp.where(ordinary, acc, BIG), jnp.minimum))
        vmax = jnp.maximum(vmax, _col_reduce(jnp.where(ordinary, acc, NEG), jnp.maximum))
        return vmin, vmax

    def inner_tiles(kjs, vmin, vmax):
        accs = [raw_scores(kj) for kj in kjs]
        for kj, acc in zip(kjs, accs):
            s_ref[pl.ds(pl.multiple_of(kj * tk, tk), tk), :] = acc
            sb_ref[pl.ds(pl.multiple_of(kj * tk, tk), tk), :] = acc.astype(BF16)
            vmin = jnp.minimum(vmin, _col_reduce(acc, jnp.minimum))
            vmax = jnp.maximum(vmax, _col_reduce(acc, jnp.maximum))
        return vmin, vmax

    n_inner = jnp.maximum(i - 1, 0)
    vmin, vmax = edge_tile(0, jnp.full((8, tq), BIG, F32), jnp.full((8, tq), NEG, F32))
    vmin, vmax = lax.fori_loop(0, n_inner // 2, lambda p, c: inner_tiles([1 + 2 * p, 2 + 2 * p], *c),
                               (vmin, vmax))
    vmin, vmax = lax.cond(n_inner % 2 == 1, lambda a, b: inner_tiles([i - 1], a, b), lambda a, b: (a, b),
                          vmin, vmax)
    vmin, vmax = lax.cond(i > 0, lambda a, b: edge_tile(i, a, b), lambda a, b: (a, b), vmin, vmax)
    vmin = jnp.min(vmin, axis=0, keepdims=True)
    vmax = jnp.max(vmax, axis=0, keepdims=True)

    def count(x, ls, strict=False):
        width = x.shape[1]
        xb = jnp.broadcast_to(x, (8, width))

        def body(kjs, acc):
            hits = []
            for kj in kjs:
                tile = s_ref[pl.ds(pl.multiple_of(kj * tk, tk), tk), ls]
                for g in range(tk // 8):
                    row = tile[g * 8:(g + 1) * 8, :]
                    hit = (row > xb) if strict else (row >= xb)
                    hits.append(jnp.where(hit, 1.0, 0.0))
            while len(hits) > 1:
                hits = [hits[a] + hits[a + 1] for a in range(0, len(hits), 2)]
            return acc + hits[0]

        acc = lax.fori_loop(0, nk // 2, lambda p, a: body([2 * p, 2 * p + 1], a), jnp.zeros((8, width), F32))
        acc = lax.cond(nk % 2 == 1, lambda a: body([nk - 1], a), lambda a: a, acc)
        return jnp.sum(acc, axis=0, keepdims=True)

    def count_b(x, ls):
        width = x.shape[1]
        xb = jnp.broadcast_to(x, (16, width))
        one = jnp.ones((16, width), BF16)
        zero = jnp.zeros((16, width), BF16)

        def body(kjs, acc):
            hits = []
            for kj in kjs:
                tile = sb_ref[pl.ds(pl.multiple_of(kj * tk, tk), tk), ls]
                for g in range(tk // 16):
                    hits.append(jnp.where(tile[g * 16:(g + 1) * 16, :] >= xb, one, zero))
            while len(hits) > 1:
                hits = [hits[a] + hits[a + 1] for a in range(0, len(hits), 2)]
            return acc + hits[0].astype(F32)

        acc = lax.fori_loop(0, nk // 2, lambda p, a: body([2 * p, 2 * p + 1], a), jnp.zeros((16, width), F32))
        acc = lax.cond(nk % 2 == 1, lambda a: body([nk - 1], a), lambda a: a, acc)
        return jnp.sum(acc, axis=0, keepdims=True)

    kf = float(topk)
    n_vis_all = (q_pos + 1).astype(F32)
    few_all = n_vis_all <= kf

    def interp_frac(clo, chi):
        return jnp.clip(jnp.log(clo / (kf + 0.5)) / jnp.log(clo / jnp.maximum(chi, 0.5)), 0.0, 1.0)

    def cond_b(st):
        it, lo16, hi16, clo, chi, done_i = st
        return jnp.logical_and(it < IDX_COARSE_MAXIT, jnp.min(done_i) < 1)

    def step_b(ls, st):
        it, lo16, hi16, clo, chi, done_i = st
        done = done_i > 0
        lo_v = _key16_to_f(lo16)
        hi_v = _key16_to_f(hi16)
        pk_i = _f2key(lo_v + (hi_v - lo_v) * interp_frac(clo, chi)) >> 16
        pk_m = jnp.where(jnp.logical_and(lo16 < 0, hi16 > 0), 0, (lo16 + hi16) >> 1)
        pk = jnp.where(it % 2 == 0, pk_i, pk_m)
        pk = jnp.minimum(jnp.maximum(pk, lo16 + 1), hi16 - 1)
        pk = jnp.where(done, lo16, pk)
        c = count_b(_key16_to_f(pk).astype(BF16), ls)
        ge = c >= kf
        act = jnp.logical_not(done)
        up = jnp.logical_and(act, ge)
        dn = jnp.logical_and(act, jnp.logical_not(ge))
        lo16 = jnp.where(up, pk, lo16)
        clo = jnp.where(up, c, clo)
        hi16 = jnp.where(dn, pk, hi16)
        chi = jnp.where(dn, c, chi)
        fin = jnp.logical_and(act, hi16 <= lo16 + 1)
        return it + 1, lo16, hi16, clo, chi, jnp.where(jnp.logical_or(done, fin), 1, 0)

    def cond(st):
        it, lo_k, hi_k, clo, chi, thr, done_i, cthr = st
        return jnp.logical_and(it < IDX_MAXIT, jnp.min(done_i) < 1)

    def step(ls, st):
        it, lo_k, hi_k, clo, chi, thr, done_i, cthr = st
        done = done_i > 0
        lo_v = _key2f(lo_k)
        hi_v = _key2f(hi_k)
        pk_i = _f2key(lo_v + (hi_v - lo_v) * interp_frac(clo, chi))
        pk_m = jnp.where(it < IDX_VALUE_BISECT_ITERS, _f2key(lo_v + 0.5 * (hi_v - lo_v)),
                         (lo_k >> 1) + (hi_k >> 1) + (lo_k & hi_k & 1))
        pk_m = jnp.where(jnp.logical_and(lo_k == 0, hi_k > KEY_MIN_NORMAL), KEY_MIN_NORMAL, pk_m)
        pk_m = jnp.where(jnp.logical_and(lo_k < 0, hi_k > 0), 0, pk_m)
        pk = jnp.where(it % 2 == 0, pk_i, pk_m)
        pk = jnp.minimum(jnp.maximum(pk, lo_k + 1), hi_k - 1)
        pk = jnp.where(done, lo_k, pk)
        x = _key2f(pk)
        c = count(x, ls)
        ge = c >= kf
        act = jnp.logical_not(done)
        up = jnp.logical_and(act, ge)
        dn = jnp.logical_and(act, jnp.logical_not(ge))
        lo_k = jnp.where(up, pk, lo_k)
        clo = jnp.where(up, c, clo)
        hi_k = jnp.where(dn, pk, hi_k)
        chi = jnp.where(dn, c, chi)
        exact = jnp.logical_and(act, c == kf)
        collapsed = jnp.logical_and(act, jnp.logical_or(
            hi_k <= lo_k + 1, jnp.logical_and(lo_k == 0, hi_k <= KEY_MIN_NORMAL)))
        fin = jnp.logical_or(exact, collapsed)
        thr = jnp.where(exact, x, jnp.where(collapsed, _key2f(lo_k), thr))
        cthr = jnp.where(exact, c, jnp.where(collapsed, clo, cthr))
        return it + 1, lo_k, hi_k, clo, chi, thr, jnp.where(jnp.logical_or(done, fin), 1, 0), cthr

    def search(ls):
        n_vis, few, lo_v0, hi_v0 = n_vis_all[:, ls], few_all[:, ls], vmin[:, ls], vmax[:, ls]
        n_meta = jnp.minimum(n_vis, float(N_META))
        done0 = jnp.logical_or(few, n_vis - n_meta < 0.5)
        lo16 = (_f2key(lo_v0) >> 16) - 1
        hi16 = (_f2key(hi_v0) >> 16) + 2
        _, lo16, hi16, _, chi, _ = lax.while_loop(
            cond_b, functools.partial(step_b, ls), (jnp.int32(0), lo16, hi16, n_vis, n_meta, jnp.where(done0, 1, 0)))
        lo_k0 = _f2key(_key16_to_f(lo16 - 1))
        hi_k0 = _f2key(_key16_to_f(hi16))
        x0 = _key2f(lo_k0)
        clo = count(x0, ls)
        exact0 = jnp.logical_and(jnp.logical_not(done0), clo == kf)
        thr0 = jnp.where(exact0, x0, 0.5 * NEG)
        st = (jnp.int32(0), lo_k0, hi_k0, clo, chi, thr0, jnp.where(jnp.logical_or(done0, exact0), 1, 0),
              jnp.full(n_vis.shape, kf, F32))
        _, _, _, _, _, thr, _, cthr = lax.while_loop(cond, functools.partial(step, ls), st)
        return jnp.where(few, 0.5 * NEG, thr), cthr

    found = [search(slice(a, a + IDX_LANES)) for a in range(0, tq, IDX_LANES)]
    thr = jnp.concatenate([f[0] for f in found], axis=1)
    cthr = jnp.concatenate([f[1] for f in found], axis=1)
    any_tie = jnp.max(jnp.where(jnp.logical_and(jnp.logical_not(few_all), cthr > kf), 1.0, 0.0)) > 0.5

    @pl.when(jnp.logical_not(any_tie))
    def _():
        def emit(kj, _):
            s = s_ref[pl.ds(pl.multiple_of(kj * tk, tk), tk), :]
            mask_ref[pl.ds(pl.multiple_of(kj * tk, tk), tk), :] = jnp.where(s >= thr, 1, 0).astype(jnp.int8)
            return 0
        lax.fori_loop(0, nk, emit, 0)

    @pl.when(any_tie)
    def _():
        allowed = kf - count(thr, slice(0, tq), strict=True)
        ri = lax.broadcasted_iota(jnp.int32, (tk, tk), 0)
        ci = lax.broadcasted_iota(jnp.int32, (tk, tk), 1)
        before = (ci < ri).astype(BF16)

        def emit(kjs, run):
            tiles = [s_ref[pl.ds(pl.multiple_of(kj * tk, tk), tk), :] for kj in kjs]
            eqfs = [jnp.where(s == thr, 1.0, 0.0) for s in tiles]
            prefix = [jnp.dot(before, e.astype(BF16), preferred_element_type=F32) for e in eqfs]
            for kj, s, e, pre in zip(kjs, tiles, eqfs, prefix):
                sel = jnp.logical_or(s > thr, jnp.logical_and(e > 0.5, run + pre < allowed))
                mask_ref[pl.ds(pl.multiple_of(kj * tk, tk), tk), :] = jnp.where(sel, 1, 0).astype(jnp.int8)
                run = run + jnp.sum(e, axis=0, keepdims=True)
            return run

        run = lax.fori_loop(0, nk // 2, lambda p, r: emit([2 * p, 2 * p + 1], r), jnp.zeros((1, tq), F32))

        @pl.when(nk % 2 == 1)
        def _():
            emit([nk - 1], run)

    def clear(kj, _):
        mask_ref[pl.ds(pl.multiple_of(kj * tk, tk), tk), :] = jnp.zeros((tk, tq), jnp.int8)
        return 0
    lax.fori_loop(nk, nkt, clear, 0)


def index_mask(iq_t, small_t, ki, topk):
    lp = iq_t.shape[1]
    tq = IDX_TQ
    assert IDX_TQ == ROW_TILE
    return pl.pallas_call(
        functools.partial(_index_mask_kernel, topk=topk),
        grid=(lp // tq,),
        in_specs=[pl.BlockSpec((IDX_Q_W, tq), lambda i: (0, i)),
                  pl.BlockSpec((16, tq), lambda i: (0, i)),
                  pl.BlockSpec((lp, IDX_DIM), lambda i: (0, 0), pipeline_mode=pl.Buffered(1))],
        out_specs=pl.BlockSpec((lp, tq), lambda i: (0, i)),
        out_shape=jax.ShapeDtypeStruct((lp, lp), jnp.int8),
        scratch_shapes=[pltpu.VMEM((lp, tq), F32), pltpu.VMEM((lp, tq), BF16)],
        compiler_params=_cparams(("arbitrary",)),
        name="index_mask",
    )(iq_t, small_t, ki)


ATTN_AHEAD = 2
ONES_ROWS = 16


def _attn_kernel(qi_ref, kj_ref, qt_ref, k_ref, vt_ref, mask_ref, bp_ref, b0_ref, bm_ref, o_ref,
                 m_ref, l_ref, acc_ref, neg_ref):
    tq, tk = Q_TILE, ROW_TILE
    sidx = pl.program_id(0)
    qi = qi_ref[sidx]
    kj = kj_ref[sidx]
    delta = kj - 2 * qi

    @pl.when(kj == 0)
    def _():
        m_ref[...] = jnp.full(m_ref.shape, NEG, F32)
        l_ref[...] = jnp.zeros(l_ref.shape, F32)
        acc_ref[...] = jnp.zeros(acc_ref.shape, F32)

    def tile(bias_ref):
        neg_ref[...] = jnp.where(mask_ref[...].astype(F32) > 0.5, 0.0, NEG)
        ones = jnp.ones((ONES_ROWS, tk), BF16)

        def logits(h):
            hs = slice(h * A_DIM, (h + 1) * A_DIM)
            s = jnp.dot(k_ref[:, hs], qt_ref[hs, :], preferred_element_type=F32) + neg_ref[...]
            if bias_ref is not None:
                s = s + bias_ref[h]
            return s

        ahead = [logits(h) for h in range(min(ATTN_AHEAD, A_HEADS))]
        for h in range(A_HEADS):
            hs = slice(h * A_DIM, (h + 1) * A_DIM)
            if h + ATTN_AHEAD < A_HEADS:
                ahead.append(logits(h + ATTN_AHEAD))
            s = ahead.pop(0)
            m_old = m_ref[h]
            m_new = jnp.maximum(m_old, jnp.max(s, axis=0, keepdims=True))
            alpha = jnp.exp2(m_old - m_new)
            p = jnp.exp2(s - m_new).astype(BF16)
            pv = jnp.dot(jnp.concatenate([vt_ref[hs, :], ones], axis=0), p, preferred_element_type=F32)
            acc_ref[hs, :] = alpha * acc_ref[hs, :] + pv[0:A_DIM, :]
            l_ref[h] = alpha * l_ref[h] + pv[A_DIM:A_DIM + 1, :]
            m_ref[h] = m_new

    pl.when(delta == 1)(lambda: tile(bp_ref))
    pl.when(delta == 0)(lambda: tile(b0_ref))
    pl.when(delta == -1)(lambda: tile(bm_ref))
    pl.when(delta < -1)(lambda: tile(None))

    @pl.when(delta == 1)
    def _():
        for h in range(A_HEADS):
            hs = slice(h * A_DIM, (h + 1) * A_DIM)
            o_ref[:, hs] = (acc_ref[hs, :] / l_ref[h]).T.astype(o_ref.dtype)


def _t5_bucket_np(rel):
    rel = np.maximum(rel, 0)
    max_exact = REL_BUCKETS // 2
    rel_f = np.maximum(rel, 1).astype(np.float32)
    large = max_exact + (np.log(rel_f / np.float32(max_exact)) / np.float32(math.log(REL_MAX_DIST / max_exact))
                         * np.float32(REL_BUCKETS - max_exact)).astype(np.int32)
    large = np.minimum(large, REL_BUCKETS - 1)
    return np.where(rel < max_exact, rel, large)


def attention(q_t, proj, v_t, mask_t, rel_bias):
    lp = proj.shape[0]
    tq, tk = Q_TILE, ROW_TILE
    assert tq == 2 * tk and tk >= REL_MAX_DIST
    nq = lp // tq
    qi_l, kj_l = [], []
    for a in range(nq):
        for b in range(2 * a + 2):
            qi_l.append(a)
            kj_l.append(b)
    qi_arr = jnp.asarray(np.array(qi_l, np.int32))
    kj_arr = jnp.asarray(np.array(kj_l, np.int32))
    table = (rel_bias.astype(F32) - rel_bias[REL_BUCKETS - 1].astype(F32)) * LOG2E

    def btile(d):
        n = tq + tk
        j = np.arange(n)
        diag = np.where(j < tq, j, j - n)
        vec = jnp.take(table, jnp.asarray(_t5_bucket_np(diag - tk * d).astype(np.int32)), axis=0).T
        return jnp.tile(vec, (1, tk))[:, :tk * (n - 1)].reshape(A_HEADS, tk, n - 1)[:, :, :tq]

    ck = COL_AK // A_W
    const3 = lambda s, qi, kj: (0, 0, 0)
    grid_spec = pltpu.PrefetchScalarGridSpec(
        num_scalar_prefetch=2,
        grid=(len(qi_l),),
        in_specs=[pl.BlockSpec((A_W, tq), lambda s, qi, kj: (0, qi[s])),
                  pl.BlockSpec((tk, A_W), lambda s, qi, kj: (kj[s], ck)),
                  pl.BlockSpec((A_W, tk), lambda s, qi, kj: (0, kj[s])),
                  pl.BlockSpec((tk, tq), lambda s, qi, kj: (kj[s], qi[s])),
                  pl.BlockSpec((A_HEADS, tk, tq), const3, pipeline_mode=pl.Buffered(1)),
                  pl.BlockSpec((A_HEADS, tk, tq), const3, pipeline_mode=pl.Buffered(1)),
                  pl.BlockSpec((A_HEADS, tk, tq), const3, pipeline_mode=pl.Buffered(1))],
        out_specs=pl.BlockSpec((tq, A_W), lambda s, qi, kj: (qi[s], 0)),
        scratch_shapes=[pltpu.VMEM((A_HEADS, 1, tq), F32),
                        pltpu.VMEM((A_HEADS, 1, tq), F32),
                        pltpu.VMEM((A_W, tq), F32),
                        pltpu.VMEM((tk, tq), F32)],
    )
    return pl.pallas_call(
        _attn_kernel,
        grid_spec=grid_spec,
        out_shape=jax.ShapeDtypeStruct((lp, A_W), BF16),
        compiler_params=_cparams(("arbitrary",)),
        name="attention",
    )(qi_arr, kj_arr, q_t, proj, v_t, mask_t, btile(1), btile(0), btile(-1))


def _merge_kernel(ym_ref, ya_ref, gm_ref, ga_ref, h_ref, wpm_ref, wpa_ref, wo_ref, g_ref, b_ref,
                  hn_ref, hb_ref, *, alpha):
    pm = jnp.dot(ym_ref[...], wpm_ref[...], preferred_element_type=F32)
    pa = jnp.dot(ya_ref[...], wpa_ref[...], preferred_element_type=F32)
    merged = _sigmoid(gm_ref[...].astype(F32)) * pm + _sigmoid(ga_ref[...].astype(F32)) * pa
    mix = jnp.dot(merged.astype(BF16), wo_ref[...], preferred_element_type=F32)
    y = _ln(alpha * h_ref[...] + mix, g_ref[...], b_ref[...])
    hn_ref[...] = y
    hb_ref[...] = y.astype(BF16)


def merge_out_ln(ym, ya, proj, h, wpm, wpa, wo, g, b, alpha):
    lp = h.shape[0]
    tm = ROW_TILE
    cgm, cga = COL_GM // D_MODEL, COL_GA // D_MODEL
    const = lambda i: (0, 0)
    return pl.pallas_call(
        functools.partial(_merge_kernel, alpha=alpha),
        grid=(lp // tm,),
        in_specs=[pl.BlockSpec((tm, M_V_W), lambda i: (i, 0)),
                  pl.BlockSpec((tm, A_W), lambda i: (i, 0)),
                  pl.BlockSpec((tm, D_MODEL), lambda i: (i, cgm)),
                  pl.BlockSpec((tm, D_MODEL), lambda i: (i, cga)),
                  pl.BlockSpec((tm, D_MODEL), lambda i: (i, 0)),
                  pl.BlockSpec((M_V_W, D_MODEL), const, pipeline_mode=pl.Buffered(1)),
                  pl.BlockSpec((A_W, D_MODEL), const, pipeline_mode=pl.Buffered(1)),
                  pl.BlockSpec((D_MODEL, D_MODEL), const, pipeline_mode=pl.Buffered(1)),
                  pl.BlockSpec((1, D_MODEL), const),
                  pl.BlockSpec((1, D_MODEL), const)],
        out_specs=[pl.BlockSpec((tm, D_MODEL), lambda i: (i, 0)),
                   pl.BlockSpec((tm, D_MODEL), lambda i: (i, 0))],
        out_shape=[jax.ShapeDtypeStruct((lp, D_MODEL), F32),
                   jax.ShapeDtypeStruct((lp, D_MODEL), BF16)],
        compiler_params=_cparams(("parallel",)),
        name="merge_out_ln",
    )(ym, ya, proj, proj, h, wpm, wpa, wo, g.reshape(1, -1), b.reshape(1, -1))


def _router_kernel(h_ref, w_ref, b_ref, e_ref, g_ref):
    lg = _dot_split3(h_ref[...], w_ref[...]) + b_ref[...]
    tm = lg.shape[0]
    col = lax.broadcasted_iota(jnp.int32, (tm, SM_W), 1).astype(F32)
    far = 1e9

    def first_argmax(x):
        mx = jnp.max(x, axis=-1, keepdims=True)
        return mx, jnp.min(jnp.where(x == mx, col, far), axis=-1, keepdims=True)

    gl = jnp.where(col < N_GROUPS, lg, -jnp.inf)
    gmax, gsel = first_argmax(gl)
    gprob = 1.0 / jnp.sum(jnp.exp(gl - gmax), axis=-1, keepdims=True)
    c0 = N_GROUPS + EXP_PER_GROUP * gsel
    el = jnp.where(jnp.logical_and(col >= c0, col < c0 + EXP_PER_GROUP), lg, -jnp.inf)
    v1, i1 = first_argmax(el)
    v2, i2 = first_argmax(jnp.where(col == i1, -jnp.inf, el))
    e21 = jnp.exp(v2 - v1)
    g1 = gprob / (1.0 + e21)
    g2 = gprob * e21 / (1.0 + e21)
    e_ref[...] = jnp.where(col == 0, i1 - N_GROUPS, jnp.where(col == 1, i2 - N_GROUPS, 0.0)).astype(jnp.int32)
    g_ref[...] = jnp.where(col == 0, g1, jnp.where(col == 1, g2, 0.0))


def router(h, w, b):
    lp = h.shape[0]
    tm = ROW_TILE
    return pl.pallas_call(
        _router_kernel,
        grid=(lp // tm,),
        in_specs=[pl.BlockSpec((tm, D_MODEL), lambda i: (i, 0)),
                  pl.BlockSpec((D_MODEL, SM_W), lambda i: (0, 0)),
                  pl.BlockSpec((1, SM_W), lambda i: (0, 0))],
        out_specs=[pl.BlockSpec((tm, SM_W), lambda i: (i, 0)),
                   pl.BlockSpec((tm, SM_W), lambda i: (i, 0))],
        out_shape=[jax.ShapeDtypeStruct((lp, SM_W), jnp.int32),
                   jax.ShapeDtypeStruct((lp, SM_W), F32)],
        compiler_params=_cparams(("parallel",)),
        name="router",
    )(h, w, b)


TOK_TILE = 256


def _rows_to_experts_kernel(dest_ref, h_ref, xs_init_ref, xs_hbm, sem):
    del xs_init_ref
    i = pl.program_id(0)
    for r in range(TOK_TILE):
        for j in range(2):
            d = dest_ref[(i * TOK_TILE + r) * 2 + j]
            pltpu.make_async_copy(h_ref.at[pl.ds(r, 1), :], xs_hbm.at[pl.ds(d, 1), :], sem.at[0]).start()
    for _ in range(2):
        pltpu.make_async_copy(h_ref, xs_hbm.at[pl.ds(0, TOK_TILE), :], sem.at[0]).wait()


def rows_to_experts(h, dest, n_rows):
    lp = h.shape[0]
    grid_spec = pltpu.PrefetchScalarGridSpec(
        num_scalar_prefetch=1,
        grid=(lp // TOK_TILE,),
        in_specs=[pl.BlockSpec((TOK_TILE, D_MODEL), lambda i, d: (i, 0)),
                  pl.BlockSpec(memory_space=pl.ANY)],
        out_specs=pl.BlockSpec(memory_space=pl.ANY),
        scratch_shapes=[pltpu.SemaphoreType.DMA((1,))],
    )
    return pl.pallas_call(
        _rows_to_experts_kernel,
        grid_spec=grid_spec,
        out_shape=jax.ShapeDtypeStruct((n_rows, D_MODEL), F32),
        input_output_aliases={2: 0},
        compiler_params=_cparams(("arbitrary",)),
        name="rows_to_experts",
    )(dest, h, jnp.zeros((n_rows, D_MODEL), F32))


def _expert_blocks_kernel(blk_e_ref, nused_ref, xs_ref, wgu_ref, wd_ref, ys_ref, wgu_b, wd_b):
    b = pl.program_id(0)
    nused = nused_ref[0]

    @pl.when(b < nused)
    def _():
        changed = jnp.logical_or(b == 0, blk_e_ref[b] != blk_e_ref[jnp.maximum(b - 1, 0)])

        @pl.when(changed)
        def _():
            wgu_b[...] = wgu_ref[0].astype(BF16)
            wd_b[...] = wd_ref[0].astype(BF16)

        gu = jnp.dot(xs_ref[...].astype(BF16), wgu_b[...], preferred_element_type=F32)
        g = gu[:, :D_FF]
        hb = (g * _sigmoid(g)) * gu[:, D_FF:]
        ys_ref[...] = jnp.dot(hb.astype(BF16), wd_b[...], preferred_element_type=F32)

    @pl.when(b >= nused)
    def _():
        ys_ref[...] = jnp.zeros(ys_ref.shape, F32)


def expert_blocks(xs, blk_e, nused, wgu, wd):
    n_blocks = xs.shape[0] // MOE_BLOCK
    last = lambda b, n: jnp.minimum(b, jnp.maximum(n[0] - 1, 0))
    grid_spec = pltpu.PrefetchScalarGridSpec(
        num_scalar_prefetch=2,
        grid=(n_blocks,),
        in_specs=[pl.BlockSpec((MOE_BLOCK, D_MODEL), lambda b, e, n: (last(b, n), 0)),
                  pl.BlockSpec((1, D_MODEL, 2 * D_FF), lambda b, e, n: (e[b], 0, 0)),
                  pl.BlockSpec((1, D_FF, D_MODEL), lambda b, e, n: (e[b], 0, 0))],
        out_specs=pl.BlockSpec((MOE_BLOCK, D_MODEL), lambda b, e, n: (b, 0)),
        scratch_shapes=[pltpu.VMEM((D_MODEL, 2 * D_FF), BF16),
                        pltpu.VMEM((D_FF, D_MODEL), BF16)],
    )
    return pl.pallas_call(
        _expert_blocks_kernel,
        grid_spec=grid_spec,
        out_shape=jax.ShapeDtypeStruct(xs.shape, F32),
        compiler_params=_cparams(("arbitrary",)),
        name="expert_blocks",
    )(blk_e, nused, xs, wgu, wd)


def _gather_combine_kernel(dest_ref, h_ref, gate_ref, g_ref, b_ref, ys_hbm, hn_ref, hb_ref, ybuf, sem, *, alpha):
    i = pl.program_id(0)
    n = pl.num_programs(0)

    def start_gather(tile, slot):
        for r in range(TOK_TILE):
            for j in range(2):
                d = dest_ref[(tile * TOK_TILE + r) * 2 + j]
                pltpu.make_async_copy(ys_hbm.at[pl.ds(d, 1), :], ybuf.at[slot, pl.ds(j * TOK_TILE + r, 1), :],
                                      sem.at[slot]).start()

    def wait_gather(slot):
        pltpu.make_async_copy(ys_hbm.at[pl.ds(0, 2 * TOK_TILE), :], ybuf.at[slot], sem.at[slot]).wait()

    @pl.when(i == 0)
    def _():
        start_gather(0, 0)

    slot = i % 2
    start_gather(i + 1, 1 - slot)
    wait_gather(slot)
    ffn = (gate_ref[:, 0:1] * ybuf[slot, 0:TOK_TILE, :] + gate_ref[:, 1:2] * ybuf[slot, TOK_TILE:2 * TOK_TILE, :])
    y = _ln(alpha * h_ref[...] + ffn, g_ref[...], b_ref[...])
    hn_ref[...] = y
    hb_ref[...] = y.astype(BF16)

    @pl.when(i == n - 1)
    def _():
        wait_gather(1 - slot)


def gather_combine_ln(h, ys, dest_ext, gates, g, b, alpha):
    lp = h.shape[0]
    assert dest_ext.shape[0] == 2 * (lp + TOK_TILE)
    grid_spec = pltpu.PrefetchScalarGridSpec(
        num_scalar_prefetch=1,
        grid=(lp // TOK_TILE,),
        in_specs=[pl.BlockSpec((TOK_TILE, D_MODEL), lambda i, d: (i, 0)),
                  pl.BlockSpec((TOK_TILE, SM_W), lambda i, d: (i, 0)),
                  pl.BlockSpec((1, D_MODEL), lambda i, d: (0, 0)),
                  pl.BlockSpec((1, D_MODEL), lambda i, d: (0, 0)),
                  pl.BlockSpec(memory_space=pl.ANY)],
        out_specs=[pl.BlockSpec((TOK_TILE, D_MODEL), lambda i, d: (i, 0)),
                   pl.BlockSpec((TOK_TILE, D_MODEL), lambda i, d: (i, 0))],
        scratch_shapes=[pltpu.VMEM((2, 2 * TOK_TILE, D_MODEL), F32),
                        pltpu.SemaphoreType.DMA((2,))],
    )
    return pl.pallas_call(
        functools.partial(_gather_combine_kernel, alpha=alpha),
        grid_spec=grid_spec,
        out_shape=[jax.ShapeDtypeStruct((lp, D_MODEL), F32),
                   jax.ShapeDtypeStruct((lp, D_MODEL), BF16)],
        compiler_params=_cparams(("arbitrary",)),
        name="gather_combine_ln",
    )(dest_ext, h, gates, g.reshape(1, -1), b.reshape(1, -1), ys)


def moe_plan(e_ids):
    n_assign = e_ids.shape[0] * 2
    e_flat = e_ids.reshape(n_assign)
    cb = ROW_TILE
    onehot = (e_flat[:, None] == jnp.arange(N_EXPERTS, dtype=jnp.int32)[None, :]).astype(F32)
    onehot = onehot.reshape(n_assign // cb, cb, N_EXPERTS)
    tri = jnp.asarray(np.tril(np.ones((cb, cb), np.float32)))
    within = jnp.einsum('ij,bjk->bik', tri, onehot, precision=HI)
    block_tot = jnp.sum(onehot, axis=1)
    block_off = jnp.cumsum(block_tot, axis=0) - block_tot
    csum = within + block_off[:, None, :]
    rank = (jnp.sum(onehot * csum, axis=2) - 1.0).astype(jnp.int32).reshape(n_assign)
    counts = jnp.sum(block_tot, axis=0).astype(jnp.int32)
    padded = (counts + MOE_BLOCK - 1) // MOE_BLOCK * MOE_BLOCK
    pend = jnp.cumsum(padded)
    pstart = pend - padded
    dest = jnp.sum(onehot.reshape(n_assign, N_EXPERTS) * pstart.astype(F32)[None, :], axis=1).astype(jnp.int32) + rank
    n_blocks = n_assign // MOE_BLOCK + N_EXPERTS
    blk_start = jnp.arange(n_blocks, dtype=jnp.int32) * MOE_BLOCK
    blk_e = jnp.minimum(jnp.sum((pend[None, :] <= blk_start[:, None]).astype(jnp.int32), axis=1), N_EXPERTS - 1)
    nused = (pend[-1] // MOE_BLOCK).astype(jnp.int32).reshape(1)
    return dest, blk_e, nused, n_blocks * MOE_BLOCK


def _split_w_in(w):
    off = {}
    o = 0
    for name, width in (("mq", M_QK_W), ("mk", M_QK_W), ("mv", M_V_W), ("mo", M_V_W), ("mi", M_HEADS),
                        ("mf", M_HEADS), ("aq", A_W), ("ak", A_W), ("av", A_W), ("iq", IDX_Q_W),
                        ("ik", IDX_DIM), ("iw", IDX_HEADS), ("gm", D_MODEL), ("ga", D_MODEL)):
        off[name] = (o, o + width)
        o += width
    seg = lambda n: w[:, off[n][0]:off[n][1]]
    big = jnp.concatenate([seg("gm"), seg("ga"), seg("mv"), seg("mo"), seg("aq") * (A_DIM ** -0.5 * LOG2E),
                           seg("ak"), seg("av"), seg("mq"), seg("mk"), seg("iq") * (IDX_DIM ** -0.5)],
                          axis=1).astype(BF16)
    small = jnp.concatenate([seg("mi"), seg("mf"), seg("iw") * (IDX_HEADS ** -0.5),
                             jnp.zeros((w.shape[0], SM_W - 2 * M_HEADS - IDX_HEADS - IDX_DIM), F32),
                             seg("ik")], axis=1)
    return big, small


def _layer(h, hb, p, rel_bias, topk, alpha):
    w_big, w_small = _split_w_in(p["w_in"])
    proj = in_proj_big(hb, w_big)
    small = in_proj_small(h, w_small)
    small_t = small[:, 0:16].T
    ki = small[:, SM_W - IDX_DIM:].astype(BF16)
    q_t = proj[:, COL_AQ:COL_AQ + A_W].T
    v_t = proj[:, COL_AV:COL_AV + A_W].T
    iq_t = proj[:, COL_IQ:COL_IQ + IDX_Q_W].T
    y_m = mlstm(proj, small, small_t, p["conv_w"], p["conv_b"], p["b_igate"], p["b_fgate"], p["mh_norm_g"])
    mask_t = index_mask(iq_t, small_t, ki, topk)
    y_a = attention(q_t, proj, v_t, mask_t, rel_bias)
    h, hb = merge_out_ln(y_m, y_a, proj, h, p["w_proj_m"].astype(BF16), p["w_proj_a"].astype(BF16),
                         p["w_out"].astype(BF16), p["ln1_g"], p["ln1_b"], alpha)
    w_r = jnp.concatenate([p["w_group"], p["w_router"],
                           jnp.zeros((D_MODEL, SM_W - N_GROUPS - N_EXPERTS), F32)], axis=1)
    b_r = jnp.concatenate([p["b_group"], p["b_router"],
                           jnp.zeros((SM_W - N_GROUPS - N_EXPERTS,), F32)]).reshape(1, SM_W)
    e_out, g_out = router(h, w_r, b_r)
    dest, blk_e, nused, n_rows = moe_plan(e_out[:, 0:2])
    xs = rows_to_experts(h, dest, n_rows)
    ys = expert_blocks(xs, blk_e, nused, p["w_gate_up"], p["w_down"])
    dest_ext = jnp.concatenate([dest, jnp.zeros((2 * TOK_TILE,), jnp.int32)])
    return gather_combine_ln(h, ys, dest_ext, g_out, p["ln2_g"], p["ln2_b"], alpha)


def _trunk(x2, meta_tokens, ln_emb_g, ln_emb_b, rel_bias, layers):
    depth = len(layers)
    alpha = (2 * depth) ** 0.25
    seq = x2.shape[0]
    length = N_META + seq
    lp = -(-length // PAD_TO) * PAD_TO
    topk = min(TOPK_MAX, length // 4)
    xp = jnp.concatenate([meta_tokens.astype(x2.dtype), x2, jnp.zeros((lp - length, D_MODEL), x2.dtype)], axis=0)
    h, hb = embed_ln(xp, ln_emb_g, ln_emb_b)
    for p in layers:
        h, hb = _layer(h, hb, p, rel_bias, topk, alpha)
    return h[N_META:length]


def kernel(x, meta_tokens, ln_emb_g, ln_emb_b, rel_bias, w_in, conv_w, conv_b, b_igate, b_fgate, mh_norm_g,
           w_proj_m, w_proj_a, w_out, ln1_g, ln1_b, w_group, b_group, w_router, b_router, w_gate_up, w_down,
           ln2_g, ln2_b):
    depth = w_in.shape[0]
    layers = [dict(w_in=w_in[l], conv_w=conv_w[l], conv_b=conv_b[l], b_igate=b_igate[l], b_fgate=b_fgate[l],
                   mh_norm_g=mh_norm_g[l], w_proj_m=w_proj_m[l], w_proj_a=w_proj_a[l], w_out=w_out[l],
                   ln1_g=ln1_g[l], ln1_b=ln1_b[l], w_group=w_group[l], b_group=b_group[l],
                   w_router=w_router[l], b_router=b_router[l], w_gate_up=w_gate_up[l], w_down=w_down[l],
                   ln2_g=ln2_g[l], ln2_b=ln2_b[l]) for l in range(depth)]
    outs = [_trunk(x[b], meta_tokens, ln_emb_g, ln_emb_b, rel_bias, layers) for b in range(x.shape[0])]
    return jnp.stack(outs, axis=0)
```

```python
import functools
import math

import jax
import jax.numpy as jnp
import numpy as np
from jax import lax
from jax.experimental import pallas as pl
from jax.experimental.pallas import tpu as pltpu

D_MODEL = 2048
N_META = 16
M_HEADS = 4
M_QK = 128
M_V = 256
CONV_W = 4
A_HEADS = 8
A_DIM = 128
TOPK_MAX = 256
IDX_HEADS = 8
IDX_DIM = 64
REL_BUCKETS = 32
REL_MAX_DIST = 128
N_GROUPS = 4
EXP_PER_GROUP = 8
N_EXPERTS = 32
D_FF = 512
MOE_BLOCK = 128
LN_EPS = 1e-5
LOG2E = math.log2(math.e)
NEG = -1e30
BIG = 1e30

M_QK_W = M_HEADS * M_QK
M_V_W = M_HEADS * M_V
A_W = A_HEADS * A_DIM
IDX_Q_W = IDX_HEADS * IDX_DIM

COL_GM = 0
COL_GA = 2048
COL_MV = 4096
COL_MO = 5120
COL_AQ = 6144
COL_AK = 7168
COL_AV = 8192
COL_MQK = 9216
COL_IQ = 10240
BIG_W = 10752
SM_W = 128

ROW_TILE = 256
Q_TILE = 512
PAD_TO = 512
F32 = jnp.float32
BF16 = jnp.bfloat16
VMEM_LIMIT = 56 * 1024 * 1024
HI = lax.Precision.HIGHEST


def _cparams(sem):
    return pltpu.CompilerParams(dimension_semantics=sem, vmem_limit_bytes=VMEM_LIMIT)


def _pick(n, cands):
    for c in cands:
        if n % c == 0:
            return c
    raise ValueError(f"no tile for {n}")


def _ln(x, g, b):
    mu = jnp.mean(x, axis=-1, keepdims=True)
    xc = x - mu
    var = jnp.mean(xc * xc, axis=-1, keepdims=True)
    return xc * lax.rsqrt(var + LN_EPS) * g + b


def _sigmoid(x):
    return 1.0 / (1.0 + jnp.exp(-x))


def _dot_split3(a, b):
    a_hi = a.astype(BF16)
    b_hi = b.astype(BF16)
    a_lo = (a - a_hi.astype(F32)).astype(BF16)
    b_lo = (b - b_hi.astype(F32)).astype(BF16)
    dot = functools.partial(jnp.dot, preferred_element_type=F32)
    return dot(a_hi, b_hi) + (dot(a_hi, b_lo) + dot(a_lo, b_hi))


def _embed_ln_kernel(x_ref, g_ref, b_ref, h_ref, hb_ref):
    y = _ln(x_ref[...], g_ref[...], b_ref[...])
    h_ref[...] = y
    hb_ref[...] = y.astype(BF16)


def embed_ln(xp, g, b):
    lp = xp.shape[0]
    tm = ROW_TILE
    return pl.pallas_call(
        _embed_ln_kernel,
        grid=(lp // tm,),
        in_specs=[pl.BlockSpec((tm, D_MODEL), lambda i: (i, 0)),
                  pl.BlockSpec((1, D_MODEL), lambda i: (0, 0)),
                  pl.BlockSpec((1, D_MODEL), lambda i: (0, 0))],
        out_specs=[pl.BlockSpec((tm, D_MODEL), lambda i: (i, 0)),
                   pl.BlockSpec((tm, D_MODEL), lambda i: (i, 0))],
        out_shape=[jax.ShapeDtypeStruct((lp, D_MODEL), F32),
                   jax.ShapeDtypeStruct((lp, D_MODEL), BF16)],
        compiler_params=_cparams(("parallel",)),
        name="embed_ln",
    )(xp, g.reshape(1, -1), b.reshape(1, -1))


def _mm_kernel(a_ref, b_ref, o_ref):
    o_ref[...] = jnp.dot(a_ref[...], b_ref[...], preferred_element_type=F32).astype(o_ref.dtype)


def in_proj_big(hb, w):
    lp = hb.shape[0]
    tm = _pick(lp, (1536, 1280, 1024, 768, 512))
    tn = 1536
    return pl.pallas_call(
        _mm_kernel,
        grid=(BIG_W // tn, lp // tm),
        in_specs=[pl.BlockSpec((tm, D_MODEL), lambda j, i: (i, 0)),
                  pl.BlockSpec((D_MODEL, tn), lambda j, i: (0, j))],
        out_specs=pl.BlockSpec((tm, tn), lambda j, i: (i, j)),
        out_shape=jax.ShapeDtypeStruct((lp, BIG_W), BF16),
        compiler_params=_cparams(("parallel", "parallel")),
        name="in_proj_big",
    )(hb, w)


def _mm_hi_kernel(a_ref, b_ref, o_ref):
    o_ref[...] = _dot_split3(a_ref[...], b_ref[...])


def in_proj_small(h, w):
    lp = h.shape[0]
    tm = ROW_TILE
    return pl.pallas_call(
        _mm_hi_kernel,
        grid=(lp // tm,),
        in_specs=[pl.BlockSpec((tm, D_MODEL), lambda i: (i, 0)),
                  pl.BlockSpec((D_MODEL, SM_W), lambda i: (0, 0))],
        out_specs=pl.BlockSpec((tm, SM_W), lambda i: (i, 0)),
        out_shape=jax.ShapeDtypeStruct((lp, SM_W), F32),
        compiler_params=_cparams(("parallel",)),
        name="in_proj_small",
    )(h, w)


def _log_sigmoid(x):
    return jnp.minimum(x, 0.0) - jnp.log1p(jnp.exp(-jnp.abs(x)))


def _mlstm_kernel(qk_ref, v_ref, o_ref, sm_ref, smt_ref, cw_ref, cb_ref, gb_ref, gbt_ref, mhg_ref,
                  y_ref, ubuf, c_ref, n_ref, m_ref):
    t = ROW_TILE
    step = pl.program_id(0)

    @pl.when(step == 0)
    def _():
        ubuf[0:8, :] = jnp.zeros((8, 2 * M_QK_W), F32)
        c_ref[...] = jnp.zeros_like(c_ref)
        n_ref[...] = jnp.zeros_like(n_ref)
        m_ref[...] = jnp.zeros_like(m_ref)

    ubuf[8:8 + t, :] = qk_ref[...].astype(F32)
    conv = cb_ref[...]
    for j in range(CONV_W):
        d = CONV_W - 1 - j
        conv = conv + cw_ref[j:j + 1, :] * ubuf[8 - d:8 - d + t, :]
    ubuf[0:8, :] = ubuf[t:t + 8, :]
    qk = conv * _sigmoid(conv)

    gcol = sm_ref[...] + gb_ref[...]
    grow = smt_ref[0:8, :] + gbt_ref[...]
    lane = lax.broadcasted_iota(jnp.int32, (t, SM_W), 1)
    gcol = jnp.where(lane < M_HEADS, gcol, _log_sigmoid(gcol))
    srow = lax.broadcasted_iota(jnp.int32, (8, t), 0)
    grow = jnp.where(srow < M_HEADS, grow, _log_sigmoid(grow))
    ri = lax.broadcasted_iota(jnp.int32, (t, t), 0)
    ci = lax.broadcasted_iota(jnp.int32, (t, t), 1)
    causal = ci <= ri
    tri = causal.astype(F32)
    bcol = jnp.dot(tri, gcol, preferred_element_type=F32, precision=HI)
    brow = lax.dot_general(grow, tri, (((1,), (1,)), ((), ())), preferred_element_type=F32,
                           precision=HI)

    v_all = v_ref[...]
    sig_o = _sigmoid(o_ref[...].astype(F32))
    qs = [qk[:, h * M_QK:(h + 1) * M_QK] * (M_QK ** -0.5) for h in range(M_HEADS)]
    ks = [qk[:, M_QK_W + h * M_QK:M_QK_W + (h + 1) * M_QK] for h in range(M_HEADS)]
    qbs = [q.astype(BF16) for q in qs]
    kbs = [k.astype(BF16) for k in ks]
    c_olds = [c_ref[h] for h in range(M_HEADS)]
    s_all = [lax.dot_general(qbs[h], kbs[h], (((1,), (1,)), ((), ())), preferred_element_type=F32)
             for h in range(M_HEADS)]
    qc_all = [jnp.dot(qbs[h], c_olds[h].astype(BF16), preferred_element_type=F32) for h in range(M_HEADS)]
    for h in range(M_HEADS):
        q, k, qb, kb = qs[h], ks[h], qbs[h], kbs[h]
        v = v_all[:, h * M_V:(h + 1) * M_V]
        b_c = bcol[:, M_HEADS + h:M_HEADS + h + 1]
        ig_c = gcol[:, h:h + 1]
        b_r = brow[M_HEADS + h:M_HEADS + h + 1, :]
        ig_r = grow[h:h + 1, :]
        m_prev = m_ref[h, 0:1, 0:1]
        dmat = jnp.where(causal, b_c + (ig_r - b_r), NEG)
        inter = b_c + m_prev
        m_t = jnp.maximum(jnp.max(dmat, axis=-1, keepdims=True), inter)
        w_intra = jnp.exp(dmat - m_t) * s_all[h]
        w_inter = jnp.exp(inter - m_t)
        c_old = c_olds[h]
        n_old = n_ref[h, 0:1, :]
        num = jnp.dot(w_intra.astype(BF16), v, preferred_element_type=F32) + w_inter * qc_all[h]
        den = (jnp.sum(w_intra, axis=-1, keepdims=True)
               + w_inter * jnp.sum(q * n_old, axis=-1, keepdims=True))
        hh = num / jnp.maximum(jnp.abs(den), jnp.exp(-m_t))
        mu = jnp.mean(hh, axis=-1, keepdims=True)
        hc = hh - mu
        var = jnp.mean(hc * hc, axis=-1, keepdims=True)
        hn = hc * lax.rsqrt(var + LN_EPS) * mhg_ref[:, h * M_V:(h + 1) * M_V]
        y_ref[:, h * M_V:(h + 1) * M_V] = (sig_o[:, h * M_V:(h + 1) * M_V] * hn).astype(y_ref.dtype)
        b_last = b_c[t - 1:t, :]
        dec = b_last - b_c + ig_c
        m_new = jnp.maximum(b_last + m_prev, jnp.max(dec, axis=0, keepdims=True))
        w_src = jnp.exp(dec - m_new)
        cs = jnp.exp(b_last + m_prev - m_new)
        wv = (w_src * v.astype(F32)).astype(BF16)
        c_ref[h] = cs * c_old + jnp.dot(k.T.astype(BF16), wv, preferred_element_type=F32)
        n_new = cs * n_old + jnp.sum(w_src * k, axis=0, keepdims=True)
        n_ref[h] = jnp.broadcast_to(n_new, (8, M_QK))
        m_ref[h] = jnp.broadcast_to(m_new, (8, 128))


def mlstm(proj, small, small_t, conv_w, conv_b, b_ig, b_fg, mh_g):
    lp = proj.shape[0]
    t = ROW_TILE
    gb = jnp.zeros((1, SM_W), F32).at[0, 0:M_HEADS].set(b_ig).at[0, M_HEADS:2 * M_HEADS].set(b_fg)
    gbt = gb[0, 0:8].reshape(8, 1)
    cq = COL_MQK // (2 * M_QK_W)
    cv = COL_MV // M_V_W
    co = COL_MO // M_V_W
    return pl.pallas_call(
        _mlstm_kernel,
        grid=(lp // t,),
        in_specs=[pl.BlockSpec((t, 2 * M_QK_W), lambda i: (i, cq)),
                  pl.BlockSpec((t, M_V_W), lambda i: (i, cv)),
                  pl.BlockSpec((t, M_V_W), lambda i: (i, co)),
                  pl.BlockSpec((t, SM_W), lambda i: (i, 0)),
                  pl.BlockSpec((16, t), lambda i: (0, i)),
                  pl.BlockSpec((CONV_W, 2 * M_QK_W), lambda i: (0, 0)),
                  pl.BlockSpec((1, 2 * M_QK_W), lambda i: (0, 0)),
                  pl.BlockSpec((1, SM_W), lambda i: (0, 0)),
                  pl.BlockSpec((8, 1), lambda i: (0, 0)),
                  pl.BlockSpec((1, M_V_W), lambda i: (0, 0))],
        out_specs=pl.BlockSpec((t, M_V_W), lambda i: (i, 0)),
        out_shape=jax.ShapeDtypeStruct((lp, M_V_W), BF16),
        scratch_shapes=[pltpu.VMEM((t + 8, 2 * M_QK_W), F32),
                        pltpu.VMEM((M_HEADS, M_QK, M_V), F32),
                        pltpu.VMEM((M_HEADS, 8, M_QK), F32),
                        pltpu.VMEM((M_HEADS, 8, 128), F32)],
        compiler_params=_cparams(("arbitrary",)),
        name="mlstm",
    )(proj, proj, proj, small, small_t, conv_w, conv_b.reshape(1, -1), gb, gbt, mh_g.reshape(1, -1))


def _f2key(f):
    b = pltpu.bitcast(f, jnp.int32)
    return jnp.where(b >= 0, b, b ^ jnp.int32(0x7FFFFFFF))


def _key2f(k):
    return pltpu.bitcast(jnp.where(k >= 0, k, k ^ jnp.int32(0x7FFFFFFF)), F32)


IDX_MAXIT = 128
IDX_VALUE_BISECT_ITERS = 48
KEY_MIN_NORMAL = 0x00800000
IDX_TQ = 256
IDX_LANES = 256


def _col_reduce(x, op):
    n = x.shape[0] // 8
    parts = [x[g * 8:(g + 1) * 8, :] for g in range(n)]
    while len(parts) > 1:
        nxt = [op(parts[a], parts[a + 1]) for a in range(0, len(parts) - 1, 2)]
        if len(parts) % 2:
            nxt.append(parts[-1])
        parts = nxt
    return parts[0]


def _index_mask_kernel(iqt_ref, wt_ref, ki_ref, mask_ref, s_ref, *, topk):
    tq = IDX_TQ
    tk = ROW_TILE
    i = pl.program_id(0)
    nk = i + 1
    nkt = pl.num_programs(0)
    q_pos = i * tq + lax.broadcasted_iota(jnp.int32, (1, tq), 1)
    k_iota = lax.broadcasted_iota(jnp.int32, (tk, 1), 0)

    def raw_scores(kj):
        kt = ki_ref[pl.ds(pl.multiple_of(kj * tk, tk), tk), :]
        acc = jnp.zeros((tk, tq), F32)
        for h in range(IDX_HEADS):
            x = jnp.dot(kt, iqt_ref[h * IDX_DIM:(h + 1) * IDX_DIM, :], preferred_element_type=F32)
            acc = acc + wt_ref[8 + h:9 + h, :] * jnp.maximum(x, 0.0)
        return acc

    def edge_tile(kj, vmin, vmax):
        acc = raw_scores(kj)
        key_pos = kj * tk + k_iota
        visible = key_pos <= q_pos
        ordinary = jnp.logical_and(visible, key_pos >= N_META)
        s = jnp.where(visible, jnp.where(key_pos < N_META, BIG, acc), NEG)
        s_ref[pl.ds(pl.multiple_of(kj * tk, tk), tk), :] = s
        vmin = jnp.minimum(vmin, _col_reduce(jnp.where(ordinary, acc, BIG), jnp.minimum))
        vmax = jnp.maximum(vmax, _col_reduce(jnp.where(ordinary, acc, NEG), jnp.maximum))
        return vmin, vmax

    def inner_tiles(kjs, vmin, vmax):
        accs = [raw_scores(kj) for kj in kjs]
        for kj, acc in zip(kjs, accs):
            s_ref[pl.ds(pl.multiple_of(kj * tk, tk), tk), :] = acc
            vmin = jnp.minimum(vmin, _col_reduce(acc, jnp.minimum))
            vmax = jnp.maximum(vmax, _col_reduce(acc, jnp.maximum))
        return vmin, vmax

    n_inner = jnp.maximum(i - 1, 0)
    vmin, vmax = edge_tile(0, jnp.full((8, tq), BIG, F32), jnp.full((8, tq), NEG, F32))
    vmin, vmax = lax.fori_loop(0, n_inner // 2, lambda p, c: inner_tiles([1 + 2 * p, 2 + 2 * p], *c),
                               (vmin, vmax))
    vmin, vmax = lax.cond(n_inner % 2 == 1, lambda a, b: inner_tiles([i - 1], a, b), lambda a, b: (a, b),
                          vmin, vmax)
    vmin, vmax = lax.cond(i > 0, lambda a, b: edge_tile(i, a, b), lambda a, b: (a, b), vmin, vmax)
    vmin = jnp.min(vmin, axis=0, keepdims=True)
    vmax = jnp.max(vmax, axis=0, keepdims=True)

    def count(x, ls, strict=False):
        width = x.shape[1]
        xb = jnp.broadcast_to(x, (8, width))

        def body(kjs, acc):
            hits = []
            for kj in kjs:
                tile = s_ref[pl.ds(pl.multiple_of(kj * tk, tk), tk), ls]
                for g in range(tk // 8):
                    row = tile[g * 8:(g + 1) * 8, :]
                    hit = (row > xb) if strict else (row >= xb)
                    hits.append(jnp.where(hit, 1.0, 0.0))
            while len(hits) > 1:
                hits = [hits[a] + hits[a + 1] for a in range(0, len(hits), 2)]
            return acc + hits[0]

        acc = lax.fori_loop(0, nk // 2, lambda p, a: body([2 * p, 2 * p + 1], a), jnp.zeros((8, width), F32))
        acc = lax.cond(nk % 2 == 1, lambda a: body([nk - 1], a), lambda a: a, acc)
        return jnp.sum(acc, axis=0, keepdims=True)

    kf = float(topk)
    n_vis_all = (q_pos + 1).astype(F32)
    few_all = n_vis_all <= kf

    def interp_frac(clo, chi):
        return jnp.clip(jnp.log(clo / (kf + 0.5)) / jnp.log(clo / jnp.maximum(chi, 0.5)), 0.0, 1.0)

    def cond(st):
        it, lo_k, hi_k, clo, chi, thr, done_i, cthr = st
        return jnp.logical_and(it < IDX_MAXIT, jnp.min(done_i) < 1)

    def step(ls, st):
        it, lo_k, hi_k, clo, chi, thr, done_i, cthr = st
        done = done_i > 0
        lo_v = _key2f(lo_k)
        hi_v = _key2f(hi_k)
        pk_i = _f2key(lo_v + (hi_v - lo_v) * interp_frac(clo, chi))
        pk_m = jnp.where(it < IDX_VALUE_BISECT_ITERS, _f2key(lo_v + 0.5 * (hi_v - lo_v)),
                         (lo_k >> 1) + (hi_k >> 1) + (lo_k & hi_k & 1))
        pk_m = jnp.where(jnp.logical_and(lo_k == 0, hi_k > KEY_MIN_NORMAL), KEY_MIN_NORMAL, pk_m)
        pk_m = jnp.where(jnp.logical_and(lo_k < 0, hi_k > 0), 0, pk_m)
        pk = jnp.where(it % 2 == 0, pk_i, pk_m)
        pk = jnp.minimum(jnp.maximum(pk, lo_k + 1), hi_k - 1)
        pk = jnp.where(done, lo_k, pk)
        x = _key2f(pk)
        c = count(x, ls)
        ge = c >= kf
        act = jnp.logical_not(done)
        up = jnp.logical_and(act, ge)
        dn = jnp.logical_and(act, jnp.logical_not(ge))
        lo_k = jnp.where(up, pk, lo_k)
        clo = jnp.where(up, c, clo)
        hi_k = jnp.where(dn, pk, hi_k)
        chi = jnp.where(dn, c, chi)
        exact = jnp.logical_and(act, c == kf)
        collapsed = jnp.logical_and(act, jnp.logical_or(
            hi_k <= lo_k + 1, jnp.logical_and(lo_k == 0, hi_k <= KEY_MIN_NORMAL)))
        fin = jnp.logical_or(exact, collapsed)
        thr = jnp.where(exact, x, jnp.where(collapsed, _key2f(lo_k), thr))
        cthr = jnp.where(exact, c, jnp.where(collapsed, clo, cthr))
        return it + 1, lo_k, hi_k, clo, chi, thr, jnp.where(jnp.logical_or(done, fin), 1, 0), cthr

    def search(ls):
        n_vis, few, lo_v0, hi_v0 = n_vis_all[:, ls], few_all[:, ls], vmin[:, ls], vmax[:, ls]
        n_meta = jnp.minimum(n_vis, float(N_META))
        lo_k0 = _f2key(lo_v0)
        hi_k0 = _f2key(hi_v0) + 1
        done0 = jnp.logical_or(few, n_vis - n_meta < 0.5)
        thr0 = jnp.full(n_vis.shape, 0.5 * NEG, F32)
        col0 = jnp.logical_and(jnp.logical_not(done0), hi_k0 <= lo_k0 + 1)
        thr0 = jnp.where(col0, lo_v0, thr0)
        st = (jnp.int32(0), lo_k0, hi_k0, n_vis, n_meta, thr0, jnp.where(jnp.logical_or(done0, col0), 1, 0),
              jnp.where(col0, n_vis, kf))
        _, _, _, _, _, thr, _, cthr = lax.while_loop(cond, functools.partial(step, ls), st)
        return jnp.where(few, 0.5 * NEG, thr), cthr

    found = [search(slice(a, a + IDX_LANES)) for a in range(0, tq, IDX_LANES)]
    thr = jnp.concatenate([f[0] for f in found], axis=1)
    cthr = jnp.concatenate([f[1] for f in found], axis=1)
    any_tie = jnp.max(jnp.where(jnp.logical_and(jnp.logical_not(few_all), cthr > kf), 1.0, 0.0)) > 0.5

    @pl.when(jnp.logical_not(any_tie))
    def _():
        def emit(kj, _):
            s = s_ref[pl.ds(pl.multiple_of(kj * tk, tk), tk), :]
            mask_ref[pl.ds(pl.multiple_of(kj * tk, tk), tk), :] = jnp.where(s >= thr, 1, 0).astype(jnp.int8)
            return 0
        lax.fori_loop(0, nk, emit, 0)

    @pl.when(any_tie)
    def _():
        allowed = kf - count(thr, slice(0, tq), strict=True)
        k_thr = _f2key(thr)
        thr_above = _key2f(jnp.where(jnp.logical_or(k_thr == 0, k_thr == -1), KEY_MIN_NORMAL, k_thr + 1))
        ri = lax.broadcasted_iota(jnp.int32, (tk, tk), 0)
        ci = lax.broadcasted_iota(jnp.int32, (tk, tk), 1)
        before = (ci < ri).astype(BF16)

        def emit(kjs, run):
            tiles = [s_ref[pl.ds(pl.multiple_of(kj * tk, tk), tk), :] for kj in kjs]
            eqfs = [jnp.where(s == thr, 1.0, 0.0) for s in tiles]
            prefix = [jnp.dot(before, e.astype(BF16), preferred_element_type=F32) for e in eqfs]
            for kj, s, e, pre in zip(kjs, tiles, eqfs, prefix):
                sel = s >= jnp.where(run + pre < allowed, thr, thr_above)
                mask_ref[pl.ds(pl.multiple_of(kj * tk, tk), tk), :] = jnp.where(sel, 1, 0).astype(jnp.int8)
                run = run + jnp.sum(e, axis=0, keepdims=True)
            return run

        run = lax.fori_loop(0, nk // 2, lambda p, r: emit([2 * p, 2 * p + 1], r), jnp.zeros((1, tq), F32))

        @pl.when(nk % 2 == 1)
        def _():
            emit([nk - 1], run)

    def clear(kj, _):
        mask_ref[pl.ds(pl.multiple_of(kj * tk, tk), tk), :] = jnp.zeros((tk, tq), jnp.int8)
        return 0
    lax.fori_loop(nk, nkt, clear, 0)


def index_mask(iq_t, small_t, ki, topk):
    lp = iq_t.shape[1]
    tq = IDX_TQ
    assert IDX_TQ == ROW_TILE
    return pl.pallas_call(
        functools.partial(_index_mask_kernel, topk=topk),
        grid=(lp // tq,),
        in_specs=[pl.BlockSpec((IDX_Q_W, tq), lambda i: (0, i)),
                  pl.BlockSpec((16, tq), lambda i: (0, i)),
                  pl.BlockSpec((lp, IDX_DIM), lambda i: (0, 0), pipeline_mode=pl.Buffered(1))],
        out_specs=pl.BlockSpec((lp, tq), lambda i: (0, i)),
        out_shape=jax.ShapeDtypeStruct((lp, lp), jnp.int8),
        scratch_shapes=[pltpu.VMEM((lp, tq), F32)],
        compiler_params=_cparams(("arbitrary",)),
        name="index_mask",
    )(iq_t, small_t, ki)


ATTN_AHEAD = 2
ONES_ROWS = 16


def _attn_kernel(qi_ref, kj_ref, qt_ref, k_ref, vt_ref, mask_ref, bp_ref, b0_ref, bm_ref, o_ref,
                 m_ref, l_ref, acc_ref, neg_ref):
    tq, tk = Q_TILE, ROW_TILE
    sidx = pl.program_id(0)
    qi = qi_ref[sidx]
    kj = kj_ref[sidx]
    delta = kj - 2 * qi

    @pl.when(kj == 0)
    def _():
        m_ref[...] = jnp.full(m_ref.shape, NEG, F32)
        l_ref[...] = jnp.zeros(l_ref.shape, F32)
        acc_ref[...] = jnp.zeros(acc_ref.shape, F32)

    def tile(bias_ref):
        neg_ref[...] = jnp.where(mask_ref[...].astype(F32) > 0.5, 0.0, NEG)
        ones = jnp.ones((ONES_ROWS, tk), BF16)

        def logits(h):
            hs = slice(h * A_DIM, (h + 1) * A_DIM)
            s = jnp.dot(k_ref[:, hs], qt_ref[hs, :], preferred_element_type=F32) + neg_ref[...]
            if bias_ref is not None:
                s = s + bias_ref[h]
            return s

        ahead = [logits(h) for h in range(min(ATTN_AHEAD, A_HEADS))]
        for h in range(A_HEADS):
            hs = slice(h * A_DIM, (h + 1) * A_DIM)
            if h + ATTN_AHEAD < A_HEADS:
                ahead.append(logits(h + ATTN_AHEAD))
            s = ahead.pop(0)
            m_old = m_ref[h]
            m_new = jnp.maximum(m_old, jnp.max(s, axis=0, keepdims=True))
            alpha = jnp.exp2(m_old - m_new)
            p = jnp.exp2(s - m_new).astype(BF16)
            pv = jnp.dot(jnp.concatenate([vt_ref[hs, :], ones], axis=0), p, preferred_element_type=F32)
            acc_ref[hs, :] = alpha * acc_ref[hs, :] + pv[0:A_DIM, :]
            l_ref[h] = alpha * l_ref[h] + pv[A_DIM:A_DIM + 1, :]
            m_ref[h] = m_new

    pl.when(delta == 1)(lambda: tile(bp_ref))
    pl.when(delta == 0)(lambda: tile(b0_ref))
    pl.when(delta == -1)(lambda: tile(bm_ref))
    pl.when(delta < -1)(lambda: tile(None))

    @pl.when(delta == 1)
    def _():
        for h in range(A_HEADS):
            hs = slice(h * A_DIM, (h + 1) * A_DIM)
            o_ref[:, hs] = (acc_ref[hs, :] / l_ref[h]).T.astype(o_ref.dtype)


def _t5_bucket_np(rel):
    rel = np.maximum(rel, 0)
    max_exact = REL_BUCKETS // 2
    rel_f = np.maximum(rel, 1).astype(np.float32)
    large = max_exact + (np.log(rel_f / np.float32(max_exact)) / np.float32(math.log(REL_MAX_DIST / max_exact))
                         * np.float32(REL_BUCKETS - max_exact)).astype(np.int32)
    large = np.minimum(large, REL_BUCKETS - 1)
    return np.where(rel < max_exact, rel, large)


def attention(q_t, proj, v_t, mask_t, rel_bias):
    lp = proj.shape[0]
    tq, tk = Q_TILE, ROW_TILE
    assert tq == 2 * tk and tk >= REL_MAX_DIST
    nq = lp // tq
    qi_l, kj_l = [], []
    for a in range(nq):
        for b in range(2 * a + 2):
            qi_l.append(a)
            kj_l.append(b)
    qi_arr = jnp.asarray(np.array(qi_l, np.int32))
    kj_arr = jnp.asarray(np.array(kj_l, np.int32))
    table = (rel_bias.astype(F32) - rel_bias[REL_BUCKETS - 1].astype(F32)) * LOG2E

    def btile(d):
        n = tq + tk
        j = np.arange(n)
        diag = np.where(j < tq, j, j - n)
        vec = jnp.take(table, jnp.asarray(_t5_bucket_np(diag - tk * d).astype(np.int32)), axis=0).T
        return jnp.tile(vec, (1, tk))[:, :tk * (n - 1)].reshape(A_HEADS, tk, n - 1)[:, :, :tq]

    ck = COL_AK // A_W
    const3 = lambda s, qi, kj: (0, 0, 0)
    grid_spec = pltpu.PrefetchScalarGridSpec(
        num_scalar_prefetch=2,
        grid=(len(qi_l),),
        in_specs=[pl.BlockSpec((A_W, tq), lambda s, qi, kj: (0, qi[s])),
                  pl.BlockSpec((tk, A_W), lambda s, qi, kj: (kj[s], ck)),
                  pl.BlockSpec((A_W, tk), lambda s, qi, kj: (0, kj[s])),
                  pl.BlockSpec((tk, tq), lambda s, qi, kj: (kj[s], qi[s])),
                  pl.BlockSpec((A_HEADS, tk, tq), const3, pipeline_mode=pl.Buffered(1)),
                  pl.BlockSpec((A_HEADS, tk, tq), const3, pipeline_mode=pl.Buffered(1)),
                  pl.BlockSpec((A_HEADS, tk, tq), const3, pipeline_mode=pl.Buffered(1))],
        out_specs=pl.BlockSpec((tq, A_W), lambda s, qi, kj: (qi[s], 0)),
        scratch_shapes=[pltpu.VMEM((A_HEADS, 1, tq), F32),
                        pltpu.VMEM((A_HEADS, 1, tq), F32),
                        pltpu.VMEM((A_W, tq), F32),
                        pltpu.VMEM((tk, tq), F32)],
    )
    return pl.pallas_call(
        _attn_kernel,
        grid_spec=grid_spec,
        out_shape=jax.ShapeDtypeStruct((lp, A_W), BF16),
        compiler_params=_cparams(("arbitrary",)),
        name="attention",
    )(qi_arr, kj_arr, q_t, proj, v_t, mask_t, btile(1), btile(0), btile(-1))


def _merge_kernel(ym_ref, ya_ref, gm_ref, ga_ref, h_ref, wpm_ref, wpa_ref, wo_ref, g_ref, b_ref,
                  hn_ref, hb_ref, *, alpha):
    pm = jnp.dot(ym_ref[...], wpm_ref[...], preferred_element_type=F32)
    pa = jnp.dot(ya_ref[...], wpa_ref[...], preferred_element_type=F32)
    merged = _sigmoid(gm_ref[...].astype(F32)) * pm + _sigmoid(ga_ref[...].astype(F32)) * pa
    mix = jnp.dot(merged.astype(BF16), wo_ref[...], preferred_element_type=F32)
    y = _ln(alpha * h_ref[...] + mix, g_ref[...], b_ref[...])
    hn_ref[...] = y
    hb_ref[...] = y.astype(BF16)


def merge_out_ln(ym, ya, proj, h, wpm, wpa, wo, g, b, alpha):
    lp = h.shape[0]
    tm = ROW_TILE
    cgm, cga = COL_GM // D_MODEL, COL_GA // D_MODEL
    const = lambda i: (0, 0)
    return pl.pallas_call(
        functools.partial(_merge_kernel, alpha=alpha),
        grid=(lp // tm,),
        in_specs=[pl.BlockSpec((tm, M_V_W), lambda i: (i, 0)),
                  pl.BlockSpec((tm, A_W), lambda i: (i, 0)),
                  pl.BlockSpec((tm, D_MODEL), lambda i: (i, cgm)),
                  pl.BlockSpec((tm, D_MODEL), lambda i: (i, cga)),
                  pl.BlockSpec((tm, D_MODEL), lambda i: (i, 0)),
                  pl.BlockSpec((M_V_W, D_MODEL), const, pipeline_mode=pl.Buffered(1)),
                  pl.BlockSpec((A_W, D_MODEL), const, pipeline_mode=pl.Buffered(1)),
                  pl.BlockSpec((D_MODEL, D_MODEL), const, pipeline_mode=pl.Buffered(1)),
                  pl.BlockSpec((1, D_MODEL), const),
                  pl.BlockSpec((1, D_MODEL), const)],
        out_specs=[pl.BlockSpec((tm, D_MODEL), lambda i: (i, 0)),
                   pl.BlockSpec((tm, D_MODEL), lambda i: (i, 0))],
        out_shape=[jax.ShapeDtypeStruct((lp, D_MODEL), F32),
                   jax.ShapeDtypeStruct((lp, D_MODEL), BF16)],
        compiler_params=_cparams(("parallel",)),
        name="merge_out_ln",
    )(ym, ya, proj, proj, h, wpm, wpa, wo, g.reshape(1, -1), b.reshape(1, -1))


def _router_kernel(h_ref, w_ref, b_ref, e_ref, g_ref):
    lg = _dot_split3(h_ref[...], w_ref[...]) + b_ref[...]
    tm = lg.shape[0]
    col = lax.broadcasted_iota(jnp.int32, (tm, SM_W), 1).astype(F32)
    far = 1e9

    def first_argmax(x):
        mx = jnp.max(x, axis=-1, keepdims=True)
        return mx, jnp.min(jnp.where(x == mx, col, far), axis=-1, keepdims=True)

    gl = jnp.where(col < N_GROUPS, lg, -jnp.inf)
    gmax, gsel = first_argmax(gl)
    gprob = 1.0 / jnp.sum(jnp.exp(gl - gmax), axis=-1, keepdims=True)
    c0 = N_GROUPS + EXP_PER_GROUP * gsel
    el = jnp.where(jnp.logical_and(col >= c0, col < c0 + EXP_PER_GROUP), lg, -jnp.inf)
    v1, i1 = first_argmax(el)
    v2, i2 = first_argmax(jnp.where(col == i1, -jnp.inf, el))
    e21 = jnp.exp(v2 - v1)
    g1 = gprob / (1.0 + e21)
    g2 = gprob * e21 / (1.0 + e21)
    e_ref[...] = jnp.where(col == 0, i1 - N_GROUPS, jnp.where(col == 1, i2 - N_GROUPS, 0.0)).astype(jnp.int32)
    g_ref[...] = jnp.where(col == 0, g1, jnp.where(col == 1, g2, 0.0))


def router(h, w, b):
    lp = h.shape[0]
    tm = ROW_TILE
    return pl.pallas_call(
        _router_kernel,
        grid=(lp // tm,),
        in_specs=[pl.BlockSpec((tm, D_MODEL), lambda i: (i, 0)),
                  pl.BlockSpec((D_MODEL, SM_W), lambda i: (0, 0)),
                  pl.BlockSpec((1, SM_W), lambda i: (0, 0))],
        out_specs=[pl.BlockSpec((tm, SM_W), lambda i: (i, 0)),
                   pl.BlockSpec((tm, SM_W), lambda i: (i, 0))],
        out_shape=[jax.ShapeDtypeStruct((lp, SM_W), jnp.int32),
                   jax.ShapeDtypeStruct((lp, SM_W), F32)],
        compiler_params=_cparams(("parallel",)),
        name="router",
    )(h, w, b)


TOK_TILE = 256


def _rows_to_experts_kernel(dest_ref, h_ref, xs_init_ref, xs_hbm, sem):
    del xs_init_ref
    i = pl.program_id(0)
    for r in range(TOK_TILE):
        for j in range(2):
            d = dest_ref[(i * TOK_TILE + r) * 2 + j]
            pltpu.make_async_copy(h_ref.at[pl.ds(r, 1), :], xs_hbm.at[pl.ds(d, 1), :], sem.at[0]).start()
    for _ in range(2):
        pltpu.make_async_copy(h_ref, xs_hbm.at[pl.ds(0, TOK_TILE), :], sem.at[0]).wait()


def rows_to_experts(h, dest, n_rows):
    lp = h.shape[0]
    grid_spec = pltpu.PrefetchScalarGridSpec(
        num_scalar_prefetch=1,
        grid=(lp // TOK_TILE,),
        in_specs=[pl.BlockSpec((TOK_TILE, D_MODEL), lambda i, d: (i, 0)),
                  pl.BlockSpec(memory_space=pl.ANY)],
        out_specs=pl.BlockSpec(memory_space=pl.ANY),
        scratch_shapes=[pltpu.SemaphoreType.DMA((1,))],
    )
    return pl.pallas_call(
        _rows_to_experts_kernel,
        grid_spec=grid_spec,
        out_shape=jax.ShapeDtypeStruct((n_rows, D_MODEL), F32),
        input_output_aliases={2: 0},
        compiler_params=_cparams(("arbitrary",)),
        name="rows_to_experts",
    )(dest, h, jnp.zeros((n_rows, D_MODEL), F32))


def _expert_blocks_kernel(blk_e_ref, nused_ref, xs_ref, wgu_ref, wd_ref, ys_ref, wgu_b, wd_b):
    b = pl.program_id(0)
    nused = nused_ref[0]

    @pl.when(b < nused)
    def _():
        changed = jnp.logical_or(b == 0, blk_e_ref[b] != blk_e_ref[jnp.maximum(b - 1, 0)])

        @pl.when(changed)
        def _():
            wgu_b[...] = wgu_ref[0].astype(BF16)
            wd_b[...] = wd_ref[0].astype(BF16)

        gu = jnp.dot(xs_ref[...].astype(BF16), wgu_b[...], preferred_element_type=F32)
        g = gu[:, :D_FF]
        hb = (g * _sigmoid(g)) * gu[:, D_FF:]
        ys_ref[...] = jnp.dot(hb.astype(BF16), wd_b[...], preferred_element_type=F32)

    @pl.when(b >= nused)
    def _():
        ys_ref[...] = jnp.zeros(ys_ref.shape, F32)


def expert_blocks(xs, blk_e, nused, wgu, wd):
    n_blocks = xs.shape[0] // MOE_BLOCK
    last = lambda b, n: jnp.minimum(b, jnp.maximum(n[0] - 1, 0))
    grid_spec = pltpu.PrefetchScalarGridSpec(
        num_scalar_prefetch=2,
        grid=(n_blocks,),
        in_specs=[pl.BlockSpec((MOE_BLOCK, D_MODEL), lambda b, e, n: (last(b, n), 0)),
                  pl.BlockSpec((1, D_MODEL, 2 * D_FF), lambda b, e, n: (e[b], 0, 0)),
                  pl.BlockSpec((1, D_FF, D_MODEL), lambda b, e, n: (e[b], 0, 0))],
        out_specs=pl.BlockSpec((MOE_BLOCK, D_MODEL), lambda b, e, n: (b, 0)),
        scratch_shapes=[pltpu.VMEM((D_MODEL, 2 * D_FF), BF16),
                        pltpu.VMEM((D_FF, D_MODEL), BF16)],
    )
    return pl.pallas_call(
        _expert_blocks_kernel,
        grid_spec=grid_spec,
        out_shape=jax.ShapeDtypeStruct(xs.shape, F32),
        compiler_params=_cparams(("arbitrary",)),
        name="expert_blocks",
    )(blk_e, nused, xs, wgu, wd)


def _gather_combine_kernel(dest_ref, h_ref, gate_ref, g_ref, b_ref, ys_hbm, hn_ref, hb_ref, ybuf, sem, *, alpha):
    i = pl.program_id(0)
    n = pl.num_programs(0)

    def start_gather(tile, slot):
        for r in range(TOK_TILE):
            for j in range(2):
                d = dest_ref[(tile * TOK_TILE + r) * 2 + j]
                pltpu.make_async_copy(ys_hbm.at[pl.ds(d, 1), :], ybuf.at[slot, pl.ds(j * TOK_TILE + r, 1), :],
                                      sem.at[slot]).start()

    def wait_gather(slot):
        pltpu.make_async_copy(ys_hbm.at[pl.ds(0, 2 * TOK_TILE), :], ybuf.at[slot], sem.at[slot]).wait()

    @pl.when(i == 0)
    def _():
        start_gather(0, 0)

    slot = i % 2
    start_gather(i + 1, 1 - slot)
    wait_gather(slot)
    ffn = (gate_ref[:, 0:1] * ybuf[slot, 0:TOK_TILE, :] + gate_ref[:, 1:2] * ybuf[slot, TOK_TILE:2 * TOK_TILE, :])
    y = _ln(alpha * h_ref[...] + ffn, g_ref[...], b_ref[...])
    hn_ref[...] = y
    hb_ref[...] = y.astype(BF16)

    @pl.when(i == n - 1)
    def _():
        wait_gather(1 - slot)


def gather_combine_ln(h, ys, dest_ext, gates, g, b, alpha):
    lp = h.shape[0]
    assert dest_ext.shape[0] == 2 * (lp + TOK_TILE)
    grid_spec = pltpu.PrefetchScalarGridSpec(
        num_scalar_prefetch=1,
        grid=(lp // TOK_TILE,),
        in_specs=[pl.BlockSpec((TOK_TILE, D_MODEL), lambda i, d: (i, 0)),
                  pl.BlockSpec((TOK_TILE, SM_W), lambda i, d: (i, 0)),
                  pl.BlockSpec((1, D_MODEL), lambda i, d: (0, 0)),
                  pl.BlockSpec((1, D_MODEL), lambda i, d: (0, 0)),
                  pl.BlockSpec(memory_space=pl.ANY)],
        out_specs=[pl.BlockSpec((TOK_TILE, D_MODEL), lambda i, d: (i, 0)),
                   pl.BlockSpec((TOK_TILE, D_MODEL), lambda i, d: (i, 0))],
        scratch_shapes=[pltpu.VMEM((2, 2 * TOK_TILE, D_MODEL), F32),
                        pltpu.SemaphoreType.DMA((2,))],
    )
    return pl.pallas_call(
        functools.partial(_gather_combine_kernel, alpha=alpha),
        grid_spec=grid_spec,
        out_shape=[jax.ShapeDtypeStruct((lp, D_MODEL), F32),
                   jax.ShapeDtypeStruct((lp, D_MODEL), BF16)],
        compiler_params=_cparams(("arbitrary",)),
        name="gather_combine_ln",
    )(dest_ext, h, gates, g.reshape(1, -1), b.reshape(1, -1), ys)


def moe_plan(e_ids):
    n_assign = e_ids.shape[0] * 2
    e_flat = e_ids.reshape(n_assign)
    cb = ROW_TILE
    onehot = (e_flat[:, None] == jnp.arange(N_EXPERTS, dtype=jnp.int32)[None, :]).astype(F32)
    onehot = onehot.reshape(n_assign // cb, cb, N_EXPERTS)
    tri = jnp.asarray(np.tril(np.ones((cb, cb), np.float32)))
    within = jnp.einsum('ij,bjk->bik', tri, onehot, precision=HI)
    block_tot = jnp.sum(onehot, axis=1)
    block_off = jnp.cumsum(block_tot, axis=0) - block_tot
    csum = within + block_off[:, None, :]
    rank = (jnp.sum(onehot * csum, axis=2) - 1.0).astype(jnp.int32).reshape(n_assign)
    counts = jnp.sum(block_tot, axis=0).astype(jnp.int32)
    padded = (counts + MOE_BLOCK - 1) // MOE_BLOCK * MOE_BLOCK
    pend = jnp.cumsum(padded)
    pstart = pend - padded
    dest = jnp.sum(onehot.reshape(n_assign, N_EXPERTS) * pstart.astype(F32)[None, :], axis=1).astype(jnp.int32) + rank
    n_blocks = n_assign // MOE_BLOCK + N_EXPERTS
    blk_start = jnp.arange(n_blocks, dtype=jnp.int32) * MOE_BLOCK
    blk_e = jnp.minimum(jnp.sum((pend[None, :] <= blk_start[:, None]).astype(jnp.int32), axis=1), N_EXPERTS - 1)
    nused = (pend[-1] // MOE_BLOCK).astype(jnp.int32).reshape(1)
    return dest, blk_e, nused, n_blocks * MOE_BLOCK


def _split_w_in(w):
    off = {}
    o = 0
    for name, width in (("mq", M_QK_W), ("mk", M_QK_W), ("mv", M_V_W), ("mo", M_V_W), ("mi", M_HEADS),
                        ("mf", M_HEADS), ("aq", A_W), ("ak", A_W), ("av", A_W), ("iq", IDX_Q_W),
                        ("ik", IDX_DIM), ("iw", IDX_HEADS), ("gm", D_MODEL), ("ga", D_MODEL)):
        off[name] = (o, o + width)
        o += width
    seg = lambda n: w[:, off[n][0]:off[n][1]]
    big = jnp.concatenate([seg("gm"), seg("ga"), seg("mv"), seg("mo"), seg("aq") * (A_DIM ** -0.5 * LOG2E),
                           seg("ak"), seg("av"), seg("mq"), seg("mk"), seg("iq") * (IDX_DIM ** -0.5)],
                          axis=1).astype(BF16)
    small = jnp.concatenate([seg("mi"), seg("mf"), seg("iw") * (IDX_HEADS ** -0.5),
                             jnp.zeros((w.shape[0], SM_W - 2 * M_HEADS - IDX_HEADS - IDX_DIM), F32),
                             seg("ik")], axis=1)
    return big, small


def _layer(h, hb, p, rel_bias, topk, alpha):
    w_big, w_small = _split_w_in(p["w_in"])
    proj = in_proj_big(hb, w_big)
    small = in_proj_small(h, w_small)
    small_t = small[:, 0:16].T
    ki = small[:, SM_W - IDX_DIM:].astype(BF16)
    q_t = proj[:, COL_AQ:COL_AQ + A_W].T
    v_t = proj[:, COL_AV:COL_AV + A_W].T
    iq_t = proj[:, COL_IQ:COL_IQ + IDX_Q_W].T
    y_m = mlstm(proj, small, small_t, p["conv_w"], p["conv_b"], p["b_igate"], p["b_fgate"], p["mh_norm_g"])
    mask_t = index_mask(iq_t, small_t, ki, topk)
    y_a = attention(q_t, proj, v_t, mask_t, rel_bias)
    h, hb = merge_out_ln(y_m, y_a, proj, h, p["w_proj_m"].astype(BF16), p["w_proj_a"].astype(BF16),
                         p["w_out"].astype(BF16), p["ln1_g"], p["ln1_b"], alpha)
    w_r = jnp.concatenate([p["w_group"], p["w_router"],
                           jnp.zeros((D_MODEL, SM_W - N_GROUPS - N_EXPERTS), F32)], axis=1)
    b_r = jnp.concatenate([p["b_group"], p["b_router"],
                           jnp.zeros((SM_W - N_GROUPS - N_EXPERTS,), F32)]).reshape(1, SM_W)
    e_out, g_out = router(h, w_r, b_r)
    dest, blk_e, nused, n_rows = moe_plan(e_out[:, 0:2])
    xs = rows_to_experts(h, dest, n_rows)
    ys = expert_blocks(xs, blk_e, nused, p["w_gate_up"], p["w_down"])
    dest_ext = jnp.concatenate([dest, jnp.zeros((2 * TOK_TILE,), jnp.int32)])
    return gather_combine_ln(h, ys, dest_ext, g_out, p["ln2_g"], p["ln2_b"], alpha)


def _trunk(x2, meta_tokens, ln_emb_g, ln_emb_b, rel_bias, layers):
    depth = len(layers)
    alpha = (2 * depth) ** 0.25
    seq = x2.shape[0]
    length = N_META + seq
    lp = -(-length // PAD_TO) * PAD_TO
    topk = min(TOPK_MAX, length // 4)
    xp = jnp.concatenate([meta_tokens.astype(x2.dtype), x2, jnp.zeros((lp - length, D_MODEL), x2.dtype)], axis=0)
    h, hb = embed_ln(xp, ln_emb_g, ln_emb_b)
    for p in layers:
        h, hb = _layer(h, hb, p, rel_bias, topk, alpha)
    return h[N_META:length]


def kernel(x, meta_tokens, ln_emb_g, ln_emb_b, rel_bias, w_in, conv_w, conv_b, b_igate, b_fgate, mh_norm_g,
           w_proj_m, w_proj_a, w_out, ln1_g, ln1_b, w_group, b_group, w_router, b_router, w_gate_up, w_down,
           ln2_g, ln2_b):
    depth = w_in.shape[0]
    layers = [dict(w_in=w_in[l], conv_w=conv_w[l], conv_b=conv_b[l], b_igate=b_igate[l], b_fgate=b_fgate[l],
                   mh_norm_g=mh_norm_g[l], w_proj_m=w_proj_m[l], w_proj_a=w_proj_a[l], w_out=w_out[l],
                   ln1_g=ln1_g[l], ln1_b=ln1_b[l], w_group=w_group[l], b_group=b_group[l],
                   w_router=w_router[l], b_router=b_router[l], w_gate_up=w_gate_up[l], w_down=w_down[l],
                   ln2_g=ln2_g[l], ln2_b=ln2_b[l]) for l in range(depth)]
    outs = [_trunk(x[b], meta_tokens, ln_emb_g, ln_emb_b, rel_bias, layers) for b in range(x.shape[0])]
    return jnp.stack(outs, axis=0)
```

```python
import functools
import math

import jax
import jax.numpy as jnp
import numpy as np
from jax import lax
from jax.experimental import pallas as pl
from jax.experimental.pallas import tpu as pltpu

D_MODEL = 2048
N_META = 16
M_HEADS = 4
M_QK = 128
M_V = 256
CONV_W = 4
A_HEADS = 8
A_DIM = 128
TOPK_MAX = 256
IDX_HEADS = 8
IDX_DIM = 64
REL_BUCKETS = 32
REL_MAX_DIST = 128
N_GROUPS = 4
EXP_PER_GROUP = 8
N_EXPERTS = 32
D_FF = 512
MOE_BLOCK = 128
LN_EPS = 1e-5
LOG2E = math.log2(math.e)
NEG = -1e30
BIG = 1e30

M_QK_W = M_HEADS * M_QK
M_V_W = M_HEADS * M_V
A_W = A_HEADS * A_DIM
IDX_Q_W = IDX_HEADS * IDX_DIM

COL_GM = 0
COL_GA = 2048
COL_MV = 4096
COL_MO = 5120
COL_AQ = 6144
COL_AK = 7168
COL_AV = 8192
COL_MQK = 9216
COL_IQ = 10240
BIG_W = 10752
SM_W = 128

ROW_TILE = 256
Q_TILE = 512
PAD_TO = 512
F32 = jnp.float32
BF16 = jnp.bfloat16
VMEM_LIMIT = 56 * 1024 * 1024
HI = lax.Precision.HIGHEST


def _cparams(sem):
    return pltpu.CompilerParams(dimension_semantics=sem, vmem_limit_bytes=VMEM_LIMIT)


def _pick(n, cands):
    for c in cands:
        if n % c == 0:
            return c
    raise ValueError(f"no tile for {n}")


def _ln(x, g, b):
    mu = jnp.mean(x, axis=-1, keepdims=True)
    xc = x - mu
    var = jnp.mean(xc * xc, axis=-1, keepdims=True)
    return xc * lax.rsqrt(var + LN_EPS) * g + b


def _sigmoid(x):
    return 1.0 / (1.0 + jnp.exp(-x))


def _dot_split3(a, b):
    a_hi = a.astype(BF16)
    b_hi = b.astype(BF16)
    a_lo = (a - a_hi.astype(F32)).astype(BF16)
    b_lo = (b - b_hi.astype(F32)).astype(BF16)
    dot = functools.partial(jnp.dot, preferred_element_type=F32)
    return dot(a_hi, b_hi) + (dot(a_hi, b_lo) + dot(a_lo, b_hi))


def _embed_ln_kernel(x_ref, g_ref, b_ref, h_ref, hb_ref):
    y = _ln(x_ref[...], g_ref[...], b_ref[...])
    h_ref[...] = y
    hb_ref[...] = y.astype(BF16)


def embed_ln(xp, g, b):
    lp = xp.shape[0]
    tm = PAD_TO
    return pl.pallas_call(
        _embed_ln_kernel,
        grid=(lp // tm,),
        in_specs=[pl.BlockSpec((tm, D_MODEL), lambda i: (i, 0)),
                  pl.BlockSpec((1, D_MODEL), lambda i: (0, 0)),
                  pl.BlockSpec((1, D_MODEL), lambda i: (0, 0))],
        out_specs=[pl.BlockSpec((tm, D_MODEL), lambda i: (i, 0)),
                   pl.BlockSpec((tm, D_MODEL), lambda i: (i, 0))],
        out_shape=[jax.ShapeDtypeStruct((lp, D_MODEL), F32),
                   jax.ShapeDtypeStruct((lp, D_MODEL), BF16)],
        compiler_params=_cparams(("parallel",)),
        name="embed_ln",
    )(xp, g.reshape(1, -1), b.reshape(1, -1))


def _mm_kernel(a_ref, b_ref, o_ref):
    o_ref[...] = jnp.dot(a_ref[...], b_ref[...], preferred_element_type=F32).astype(o_ref.dtype)


def in_proj_big(hb, w):
    lp = hb.shape[0]
    tm = _pick(lp, (1536, 1280, 1024, 768, 512))
    tn = 1536
    return pl.pallas_call(
        _mm_kernel,
        grid=(BIG_W // tn, lp // tm),
        in_specs=[pl.BlockSpec((tm, D_MODEL), lambda j, i: (i, 0)),
                  pl.BlockSpec((D_MODEL, tn), lambda j, i: (0, j))],
        out_specs=pl.BlockSpec((tm, tn), lambda j, i: (i, j)),
        out_shape=jax.ShapeDtypeStruct((lp, BIG_W), BF16),
        compiler_params=_cparams(("parallel", "parallel")),
        name="in_proj_big",
    )(hb, w)


def _mm_hi_kernel(a_ref, b_ref, o_ref):
    o_ref[...] = _dot_split3(a_ref[...], b_ref[...])


def in_proj_small(h, w):
    lp = h.shape[0]
    tm = PAD_TO
    return pl.pallas_call(
        _mm_hi_kernel,
        grid=(lp // tm,),
        in_specs=[pl.BlockSpec((tm, D_MODEL), lambda i: (i, 0)),
                  pl.BlockSpec((D_MODEL, SM_W), lambda i: (0, 0))],
        out_specs=pl.BlockSpec((tm, SM_W), lambda i: (i, 0)),
        out_shape=jax.ShapeDtypeStruct((lp, SM_W), F32),
        compiler_params=_cparams(("parallel",)),
        name="in_proj_small",
    )(h, w)


def _log_sigmoid(x):
    return jnp.minimum(x, 0.0) - jnp.log1p(jnp.exp(-jnp.abs(x)))


def _mlstm_kernel(qk_ref, v_ref, o_ref, sm_ref, smt_ref, cw_ref, cb_ref, gb_ref, gbt_ref, mhg_ref,
                  y_ref, ubuf, c_ref, n_ref, m_ref):
    t = ROW_TILE
    step = pl.program_id(0)

    @pl.when(step == 0)
    def _():
        ubuf[0:8, :] = jnp.zeros((8, 2 * M_QK_W), F32)
        c_ref[...] = jnp.zeros_like(c_ref)
        n_ref[...] = jnp.zeros_like(n_ref)
        m_ref[...] = jnp.zeros_like(m_ref)

    ubuf[8:8 + t, :] = qk_ref[...].astype(F32)
    conv = cb_ref[...]
    for j in range(CONV_W):
        d = CONV_W - 1 - j
        conv = conv + cw_ref[j:j + 1, :] * ubuf[8 - d:8 - d + t, :]
    ubuf[0:8, :] = ubuf[t:t + 8, :]
    qk = conv * _sigmoid(conv)

    gcol = sm_ref[...] + gb_ref[...]
    grow = smt_ref[0:8, :] + gbt_ref[...]
    lane = lax.broadcasted_iota(jnp.int32, (t, SM_W), 1)
    gcol = jnp.where(lane < M_HEADS, gcol, _log_sigmoid(gcol))
    srow = lax.broadcasted_iota(jnp.int32, (8, t), 0)
    grow = jnp.where(srow < M_HEADS, grow, _log_sigmoid(grow))
    ri = lax.broadcasted_iota(jnp.int32, (t, t), 0)
    ci = lax.broadcasted_iota(jnp.int32, (t, t), 1)
    causal = ci <= ri
    tri = causal.astype(F32)
    bcol = jnp.dot(tri, gcol, preferred_element_type=F32, precision=HI)
    brow = lax.dot_general(grow, tri, (((1,), (1,)), ((), ())), preferred_element_type=F32,
                           precision=HI)

    v_all = v_ref[...]
    sig_o = _sigmoid(o_ref[...].astype(F32))
    qs = [qk[:, h * M_QK:(h + 1) * M_QK] * (M_QK ** -0.5) for h in range(M_HEADS)]
    ks = [qk[:, M_QK_W + h * M_QK:M_QK_W + (h + 1) * M_QK] for h in range(M_HEADS)]
    qbs = [q.astype(BF16) for q in qs]
    kbs = [k.astype(BF16) for k in ks]
    c_olds = [c_ref[h] for h in range(M_HEADS)]
    s_all = [lax.dot_general(qbs[h], kbs[h], (((1,), (1,)), ((), ())), preferred_element_type=F32)
             for h in range(M_HEADS)]
    qc_all = [jnp.dot(qbs[h], c_olds[h].astype(BF16), preferred_element_type=F32) for h in range(M_HEADS)]
    for h in range(M_HEADS):
        q, k, qb, kb = qs[h], ks[h], qbs[h], kbs[h]
        v = v_all[:, h * M_V:(h + 1) * M_V]
        b_c = bcol[:, M_HEADS + h:M_HEADS + h + 1]
        ig_c = gcol[:, h:h + 1]
        b_r = brow[M_HEADS + h:M_HEADS + h + 1, :]
        ig_r = grow[h:h + 1, :]
        m_prev = m_ref[h, 0:1, 0:1]
        dmat = jnp.where(causal, b_c + (ig_r - b_r), NEG)
        inter = b_c + m_prev
        m_t = jnp.maximum(jnp.max(dmat, axis=-1, keepdims=True), inter)
        w_intra = jnp.exp(dmat - m_t) * s_all[h]
        w_inter = jnp.exp(inter - m_t)
        c_old = c_olds[h]
        n_old = n_ref[h, 0:1, :]
        num = jnp.dot(w_intra.astype(BF16), v, preferred_element_type=F32) + w_inter * qc_all[h]
        den = (jnp.sum(w_intra, axis=-1, keepdims=True)
               + w_inter * jnp.sum(q * n_old, axis=-1, keepdims=True))
        hh = num / jnp.maximum(jnp.abs(den), jnp.exp(-m_t))
        mu = jnp.mean(hh, axis=-1, keepdims=True)
        hc = hh - mu
        var = jnp.mean(hc * hc, axis=-1, keepdims=True)
        hn = hc * lax.rsqrt(var + LN_EPS) * mhg_ref[:, h * M_V:(h + 1) * M_V]
        y_ref[:, h * M_V:(h + 1) * M_V] = (sig_o[:, h * M_V:(h + 1) * M_V] * hn).astype(y_ref.dtype)
        b_last = b_c[t - 1:t, :]
        dec = b_last - b_c + ig_c
        m_new = jnp.maximum(b_last + m_prev, jnp.max(dec, axis=0, keepdims=True))
        w_src = jnp.exp(dec - m_new)
        cs = jnp.exp(b_last + m_prev - m_new)
        wv = (w_src * v.astype(F32)).astype(BF16)
        c_ref[h] = cs * c_old + jnp.dot(k.T.astype(BF16), wv, preferred_element_type=F32)
        n_new = cs * n_old + jnp.sum(w_src * k, axis=0, keepdims=True)
        n_ref[h] = jnp.broadcast_to(n_new, (8, M_QK))
        m_ref[h] = jnp.broadcast_to(m_new, (8, 128))


def mlstm(proj, small, small_t, conv_w, conv_b, b_ig, b_fg, mh_g):
    lp = proj.shape[0]
    t = ROW_TILE
    gb = jnp.zeros((1, SM_W), F32).at[0, 0:M_HEADS].set(b_ig).at[0, M_HEADS:2 * M_HEADS].set(b_fg)
    gbt = gb[0, 0:8].reshape(8, 1)
    cq = COL_MQK // (2 * M_QK_W)
    cv = COL_MV // M_V_W
    co = COL_MO // M_V_W
    return pl.pallas_call(
        _mlstm_kernel,
        grid=(lp // t,),
        in_specs=[pl.BlockSpec((t, 2 * M_QK_W), lambda i: (i, cq)),
                  pl.BlockSpec((t, M_V_W), lambda i: (i, cv)),
                  pl.BlockSpec((t, M_V_W), lambda i: (i, co)),
                  pl.BlockSpec((t, SM_W), lambda i: (i, 0)),
                  pl.BlockSpec((16, t), lambda i: (0, i)),
                  pl.BlockSpec((CONV_W, 2 * M_QK_W), lambda i: (0, 0)),
                  pl.BlockSpec((1, 2 * M_QK_W), lambda i: (0, 0)),
                  pl.BlockSpec((1, SM_W), lambda i: (0, 0)),
                  pl.BlockSpec((8, 1), lambda i: (0, 0)),
                  pl.BlockSpec((1, M_V_W), lambda i: (0, 0))],
        out_specs=pl.BlockSpec((t, M_V_W), lambda i: (i, 0)),
        out_shape=jax.ShapeDtypeStruct((lp, M_V_W), BF16),
        scratch_shapes=[pltpu.VMEM((t + 8, 2 * M_QK_W), F32),
                        pltpu.VMEM((M_HEADS, M_QK, M_V), F32),
                        pltpu.VMEM((M_HEADS, 8, M_QK), F32),
                        pltpu.VMEM((M_HEADS, 8, 128), F32)],
        compiler_params=_cparams(("arbitrary",)),
        name="mlstm",
    )(proj, proj, proj, small, small_t, conv_w, conv_b.reshape(1, -1), gb, gbt, mh_g.reshape(1, -1))


def _f2key(f):
    b = pltpu.bitcast(f, jnp.int32)
    return jnp.where(b >= 0, b, b ^ jnp.int32(0x7FFFFFFF))


def _key2f(k):
    return pltpu.bitcast(jnp.where(k >= 0, k, k ^ jnp.int32(0x7FFFFFFF)), F32)


IDX_MAXIT = 128
IDX_VALUE_BISECT_ITERS = 48
KEY_MIN_NORMAL = 0x00800000
IDX_TQ = 256
IDX_LANES = 256


def _col_reduce(x, op):
    n = x.shape[0] // 8
    parts = [x[g * 8:(g + 1) * 8, :] for g in range(n)]
    while len(parts) > 1:
        nxt = [op(parts[a], parts[a + 1]) for a in range(0, len(parts) - 1, 2)]
        if len(parts) % 2:
            nxt.append(parts[-1])
        parts = nxt
    return parts[0]


def _index_mask_kernel(iqt_ref, wt_ref, ki_ref, mask_ref, s_ref, *, topk):
    tq = IDX_TQ
    tk = ROW_TILE
    i = pl.program_id(0)
    nk = i + 1
    nkt = pl.num_programs(0)
    q_pos = i * tq + lax.broadcasted_iota(jnp.int32, (1, tq), 1)
    k_iota = lax.broadcasted_iota(jnp.int32, (tk, 1), 0)

    def raw_scores(kj):
        kt = ki_ref[pl.ds(pl.multiple_of(kj * tk, tk), tk), :]
        acc = jnp.zeros((tk, tq), F32)
        for h in range(IDX_HEADS):
            x = jnp.dot(kt, iqt_ref[h * IDX_DIM:(h + 1) * IDX_DIM, :], preferred_element_type=F32)
            acc = acc + wt_ref[8 + h:9 + h, :] * jnp.maximum(x, 0.0)
        return acc

    def edge_tile(kj, vmin, vmax):
        acc = raw_scores(kj)
        key_pos = kj * tk + k_iota
        visible = key_pos <= q_pos
        ordinary = jnp.logical_and(visible, key_pos >= N_META)
        s = jnp.where(visible, jnp.where(key_pos < N_META, BIG, acc), NEG)
        s_ref[pl.ds(pl.multiple_of(kj * tk, tk), tk), :] = s
        vmin = jnp.minimum(vmin, _col_reduce(jnp.where(ordinary, acc, BIG), jnp.minimum))
        vmax = jnp.maximum(vmax, _col_reduce(jnp.where(ordinary, acc, NEG), jnp.maximum))
        return vmin, vmax

    def inner_tiles(kjs, vmin, vmax):
        accs = [raw_scores(kj) for kj in kjs]
        for kj, acc in zip(kjs, accs):
            s_ref[pl.ds(pl.multiple_of(kj * tk, tk), tk), :] = acc
            vmin = jnp.minimum(vmin, _col_reduce(acc, jnp.minimum))
            vmax = jnp.maximum(vmax, _col_reduce(acc, jnp.maximum))
        return vmin, vmax

    n_inner = jnp.maximum(i - 1, 0)
    vmin, vmax = edge_tile(0, jnp.full((8, tq), BIG, F32), jnp.full((8, tq), NEG, F32))
    vmin, vmax = lax.fori_loop(0, n_inner // 2, lambda p, c: inner_tiles([1 + 2 * p, 2 + 2 * p], *c),
                               (vmin, vmax))
    vmin, vmax = lax.cond(n_inner % 2 == 1, lambda a, b: inner_tiles([i - 1], a, b), lambda a, b: (a, b),
                          vmin, vmax)
    vmin, vmax = lax.cond(i > 0, lambda a, b: edge_tile(i, a, b), lambda a, b: (a, b), vmin, vmax)
    vmin = jnp.min(vmin, axis=0, keepdims=True)
    vmax = jnp.max(vmax, axis=0, keepdims=True)

    def count(x, ls, strict=False):
        width = x.shape[1]
        xb = jnp.broadcast_to(x, (8, width))

        def body(kjs, acc):
            hits = []
            for kj in kjs:
                tile = s_ref[pl.ds(pl.multiple_of(kj * tk, tk), tk), ls]
                for g in range(tk // 8):
                    row = tile[g * 8:(g + 1) * 8, :]
                    hit = (row > xb) if strict else (row >= xb)
                    hits.append(jnp.where(hit, 1.0, 0.0))
            while len(hits) > 1:
                hits = [hits[a] + hits[a + 1] for a in range(0, len(hits), 2)]
            return acc + hits[0]

        acc = lax.fori_loop(0, nk // 2, lambda p, a: body([2 * p, 2 * p + 1], a), jnp.zeros((8, width), F32))
        acc = lax.cond(nk % 2 == 1, lambda a: body([nk - 1], a), lambda a: a, acc)
        return jnp.sum(acc, axis=0, keepdims=True)

    kf = float(topk)
    n_vis_all = (q_pos + 1).astype(F32)
    few_all = n_vis_all <= kf

    def interp_frac(clo, chi):
        log_lo = jnp.log(clo)
        return jnp.clip((log_lo - math.log(kf + 0.5)) / (log_lo - jnp.log(jnp.maximum(chi, 0.5))), 0.0, 1.0)

    def cond(st):
        it, lo_k, hi_k, clo, chi, thr, done_i, cthr = st
        return jnp.logical_and(it < IDX_MAXIT, jnp.min(done_i) < 1)

    def step(ls, st):
        it, lo_k, hi_k, clo, chi, thr, done_i, cthr = st
        done = done_i > 0
        lo_v = _key2f(lo_k)
        hi_v = _key2f(hi_k)
        pk_i = _f2key(lo_v + (hi_v - lo_v) * interp_frac(clo, chi))
        pk_m = jnp.where(it < IDX_VALUE_BISECT_ITERS, _f2key(lo_v + 0.5 * (hi_v - lo_v)),
                         (lo_k >> 1) + (hi_k >> 1) + (lo_k & hi_k & 1))
        pk_m = jnp.where(jnp.logical_and(lo_k == 0, hi_k > KEY_MIN_NORMAL), KEY_MIN_NORMAL, pk_m)
        pk_m = jnp.where(jnp.logical_and(lo_k < 0, hi_k > 0), 0, pk_m)
        pk = jnp.where(it % 2 == 0, pk_i, pk_m)
        pk = jnp.minimum(jnp.maximum(pk, lo_k + 1), hi_k - 1)
        pk = jnp.where(done, lo_k, pk)
        x = _key2f(pk)
        c = count(x, ls)
        ge = c >= kf
        act = jnp.logical_not(done)
        up = jnp.logical_and(act, ge)
        dn = jnp.logical_and(act, jnp.logical_not(ge))
        lo_k = jnp.where(up, pk, lo_k)
        clo = jnp.where(up, c, clo)
        hi_k = jnp.where(dn, pk, hi_k)
        chi = jnp.where(dn, c, chi)
        exact = jnp.logical_and(act, c == kf)
        collapsed = jnp.logical_and(act, jnp.logical_or(
            hi_k <= lo_k + 1, jnp.logical_and(lo_k == 0, hi_k <= KEY_MIN_NORMAL)))
        fin = jnp.logical_or(exact, collapsed)
        thr = jnp.where(exact, x, jnp.where(collapsed, _key2f(lo_k), thr))
        cthr = jnp.where(exact, c, jnp.where(collapsed, clo, cthr))
        return it + 1, lo_k, hi_k, clo, chi, thr, jnp.where(jnp.logical_or(done, fin), 1, 0), cthr

    def search(ls):
        n_vis, few, lo_v0, hi_v0 = n_vis_all[:, ls], few_all[:, ls], vmin[:, ls], vmax[:, ls]
        n_meta = jnp.minimum(n_vis, float(N_META))
        lo_k0 = _f2key(lo_v0)
        hi_k0 = _f2key(hi_v0) + 1
        done0 = jnp.logical_or(few, n_vis - n_meta < 0.5)
        thr0 = jnp.full(n_vis.shape, 0.5 * NEG, F32)
        col0 = jnp.logical_and(jnp.logical_not(done0), hi_k0 <= lo_k0 + 1)
        thr0 = jnp.where(col0, lo_v0, thr0)
        st = (jnp.int32(0), lo_k0, hi_k0, n_vis, n_meta, thr0, jnp.where(jnp.logical_or(done0, col0), 1, 0),
              jnp.where(col0, n_vis, kf))
        _, _, _, _, _, thr, _, cthr = lax.while_loop(cond, functools.partial(step, ls), st)
        return jnp.where(few, 0.5 * NEG, thr), cthr

    found = [search(slice(a, a + IDX_LANES)) for a in range(0, tq, IDX_LANES)]
    thr = jnp.concatenate([f[0] for f in found], axis=1)
    cthr = jnp.concatenate([f[1] for f in found], axis=1)
    any_tie = jnp.max(jnp.where(jnp.logical_and(jnp.logical_not(few_all), cthr > kf), 1.0, 0.0)) > 0.5

    @pl.when(jnp.logical_not(any_tie))
    def _():
        def emit(kj, _):
            s = s_ref[pl.ds(pl.multiple_of(kj * tk, tk), tk), :]
            mask_ref[pl.ds(pl.multiple_of(kj * tk, tk), tk), :] = jnp.where(s >= thr, 1, 0).astype(jnp.int8)
            return 0
        lax.fori_loop(0, nk, emit, 0)

    @pl.when(any_tie)
    def _():
        allowed = kf - count(thr, slice(0, tq), strict=True)
        k_thr = _f2key(thr)
        thr_above = _key2f(jnp.where(jnp.logical_or(k_thr == 0, k_thr == -1), KEY_MIN_NORMAL, k_thr + 1))
        ri = lax.broadcasted_iota(jnp.int32, (tk, tk), 0)
        ci = lax.broadcasted_iota(jnp.int32, (tk, tk), 1)
        before = (ci < ri).astype(BF16)

        def emit(kjs, run):
            tiles = [s_ref[pl.ds(pl.multiple_of(kj * tk, tk), tk), :] for kj in kjs]
            eqfs = [jnp.where(s == thr, 1.0, 0.0) for s in tiles]
            prefix = [jnp.dot(before, e.astype(BF16), preferred_element_type=F32) for e in eqfs]
            for kj, s, e, pre in zip(kjs, tiles, eqfs, prefix):
                sel = s >= jnp.where(run + pre < allowed, thr, thr_above)
                mask_ref[pl.ds(pl.multiple_of(kj * tk, tk), tk), :] = jnp.where(sel, 1, 0).astype(jnp.int8)
                run = run + jnp.sum(e, axis=0, keepdims=True)
            return run

        run = lax.fori_loop(0, nk // 2, lambda p, r: emit([2 * p, 2 * p + 1], r), jnp.zeros((1, tq), F32))

        @pl.when(nk % 2 == 1)
        def _():
            emit([nk - 1], run)

    def clear(kj, _):
        mask_ref[pl.ds(pl.multiple_of(kj * tk, tk), tk), :] = jnp.zeros((tk, tq), jnp.int8)
        return 0
    lax.fori_loop(nk, nkt, clear, 0)


def index_mask(iq_t, small_t, ki, topk):
    lp = iq_t.shape[1]
    tq = IDX_TQ
    assert IDX_TQ == ROW_TILE
    return pl.pallas_call(
        functools.partial(_index_mask_kernel, topk=topk),
        grid=(lp // tq,),
        in_specs=[pl.BlockSpec((IDX_Q_W, tq), lambda i: (0, i)),
                  pl.BlockSpec((16, tq), lambda i: (0, i)),
                  pl.BlockSpec((lp, IDX_DIM), lambda i: (0, 0), pipeline_mode=pl.Buffered(1))],
        out_specs=pl.BlockSpec((lp, tq), lambda i: (0, i)),
        out_shape=jax.ShapeDtypeStruct((lp, lp), jnp.int8),
        scratch_shapes=[pltpu.VMEM((lp, tq), F32)],
        compiler_params=_cparams(("arbitrary",)),
        name="index_mask",
    )(iq_t, small_t, ki)


ATTN_AHEAD = 2
ONES_ROWS = 16


def _attn_kernel(qi_ref, kj_ref, qt_ref, k_ref, vt_ref, mask_ref, bp_ref, b0_ref, bm_ref, o_ref,
                 m_ref, l_ref, acc_ref, neg_ref):
    tq, tk = Q_TILE, ROW_TILE
    sidx = pl.program_id(0)
    qi = qi_ref[sidx]
    kj = kj_ref[sidx]
    delta = kj - 2 * qi

    @pl.when(kj == 0)
    def _():
        m_ref[...] = jnp.full(m_ref.shape, NEG, F32)
        l_ref[...] = jnp.zeros(l_ref.shape, F32)
        acc_ref[...] = jnp.zeros(acc_ref.shape, F32)

    def tile(bias_ref):
        neg_ref[...] = jnp.where(mask_ref[...].astype(F32) > 0.5, 0.0, NEG)
        ones = jnp.ones((ONES_ROWS, tk), BF16)

        def logits(h):
            hs = slice(h * A_DIM, (h + 1) * A_DIM)
            s = jnp.dot(k_ref[:, hs], qt_ref[hs, :], preferred_element_type=F32) + neg_ref[...]
            if bias_ref is not None:
                s = s + bias_ref[h]
            return s

        ahead = [logits(h) for h in range(min(ATTN_AHEAD, A_HEADS))]
        for h in range(A_HEADS):
            hs = slice(h * A_DIM, (h + 1) * A_DIM)
            if h + ATTN_AHEAD < A_HEADS:
                ahead.append(logits(h + ATTN_AHEAD))
            s = ahead.pop(0)
            m_old = m_ref[h]
            m_new = jnp.maximum(m_old, jnp.max(s, axis=0, keepdims=True))
            alpha = jnp.exp2(m_old - m_new)
            p = jnp.exp2(s - m_new).astype(BF16)
            pv = jnp.dot(jnp.concatenate([vt_ref[hs, :], ones], axis=0), p, preferred_element_type=F32)
            acc_ref[hs, :] = alpha * acc_ref[hs, :] + pv[0:A_DIM, :]
            l_ref[h] = alpha * l_ref[h] + pv[A_DIM:A_DIM + 1, :]
            m_ref[h] = m_new

    pl.when(delta == 1)(lambda: tile(bp_ref))
    pl.when(delta == 0)(lambda: tile(b0_ref))
    pl.when(delta == -1)(lambda: tile(bm_ref))
    pl.when(delta < -1)(lambda: tile(None))

    @pl.when(delta == 1)
    def _():
        for h in range(A_HEADS):
            hs = slice(h * A_DIM, (h + 1) * A_DIM)
            o_ref[:, hs] = (acc_ref[hs, :] / l_ref[h]).T.astype(o_ref.dtype)


def _t5_bucket_np(rel):
    rel = np.maximum(rel, 0)
    max_exact = REL_BUCKETS // 2
    rel_f = np.maximum(rel, 1).astype(np.float32)
    large = max_exact + (np.log(rel_f / np.float32(max_exact)) / np.float32(math.log(REL_MAX_DIST / max_exact))
                         * np.float32(REL_BUCKETS - max_exact)).astype(np.int32)
    large = np.minimum(large, REL_BUCKETS - 1)
    return np.where(rel < max_exact, rel, large)


def attention(q_t, proj, v_t, mask_t, rel_bias):
    lp = proj.shape[0]
    tq, tk = Q_TILE, ROW_TILE
    assert tq == 2 * tk and tk >= REL_MAX_DIST
    nq = lp // tq
    qi_l, kj_l = [], []
    for a in range(nq):
        for b in range(2 * a + 2):
            qi_l.append(a)
            kj_l.append(b)
    qi_arr = jnp.asarray(np.array(qi_l, np.int32))
    kj_arr = jnp.asarray(np.array(kj_l, np.int32))
    table = (rel_bias.astype(F32) - rel_bias[REL_BUCKETS - 1].astype(F32)) * LOG2E

    def btile(d):
        n = tq + tk
        j = np.arange(n)
        diag = np.where(j < tq, j, j - n)
        vec = jnp.take(table, jnp.asarray(_t5_bucket_np(diag - tk * d).astype(np.int32)), axis=0).T
        return jnp.tile(vec, (1, tk))[:, :tk * (n - 1)].reshape(A_HEADS, tk, n - 1)[:, :, :tq]

    ck = COL_AK // A_W
    const3 = lambda s, qi, kj: (0, 0, 0)
    grid_spec = pltpu.PrefetchScalarGridSpec(
        num_scalar_prefetch=2,
        grid=(len(qi_l),),
        in_specs=[pl.BlockSpec((A_W, tq), lambda s, qi, kj: (0, qi[s])),
                  pl.BlockSpec((tk, A_W), lambda s, qi, kj: (kj[s], ck)),
                  pl.BlockSpec((A_W, tk), lambda s, qi, kj: (0, kj[s])),
                  pl.BlockSpec((tk, tq), lambda s, qi, kj: (kj[s], qi[s])),
                  pl.BlockSpec((A_HEADS, tk, tq), const3, pipeline_mode=pl.Buffered(1)),
                  pl.BlockSpec((A_HEADS, tk, tq), const3, pipeline_mode=pl.Buffered(1)),
                  pl.BlockSpec((A_HEADS, tk, tq), const3, pipeline_mode=pl.Buffered(1))],
        out_specs=pl.BlockSpec((tq, A_W), lambda s, qi, kj: (qi[s], 0)),
        scratch_shapes=[pltpu.VMEM((A_HEADS, 1, tq), F32),
                        pltpu.VMEM((A_HEADS, 1, tq), F32),
                        pltpu.VMEM((A_W, tq), F32),
                        pltpu.VMEM((tk, tq), F32)],
    )
    return pl.pallas_call(
        _attn_kernel,
        grid_spec=grid_spec,
        out_shape=jax.ShapeDtypeStruct((lp, A_W), BF16),
        compiler_params=_cparams(("arbitrary",)),
        name="attention",
    )(qi_arr, kj_arr, q_t, proj, v_t, mask_t, btile(1), btile(0), btile(-1))


def _merge_kernel(ym_ref, ya_ref, gm_ref, ga_ref, h_ref, wpm_ref, wpa_ref, wo_ref, g_ref, b_ref,
                  hn_ref, hb_ref, *, alpha):
    pm = jnp.dot(ym_ref[...], wpm_ref[...], preferred_element_type=F32)
    pa = jnp.dot(ya_ref[...], wpa_ref[...], preferred_element_type=F32)
    merged = _sigmoid(gm_ref[...].astype(F32)) * pm + _sigmoid(ga_ref[...].astype(F32)) * pa
    mix = jnp.dot(merged.astype(BF16), wo_ref[...], preferred_element_type=F32)
    y = _ln(alpha * h_ref[...] + mix, g_ref[...], b_ref[...])
    hn_ref[...] = y
    hb_ref[...] = y.astype(BF16)


def merge_out_ln(ym, ya, proj, h, wpm, wpa, wo, g, b, alpha):
    lp = h.shape[0]
    tm = ROW_TILE
    cgm, cga = COL_GM // D_MODEL, COL_GA // D_MODEL
    const = lambda i: (0, 0)
    return pl.pallas_call(
        functools.partial(_merge_kernel, alpha=alpha),
        grid=(lp // tm,),
        in_specs=[pl.BlockSpec((tm, M_V_W), lambda i: (i, 0)),
                  pl.BlockSpec((tm, A_W), lambda i: (i, 0)),
                  pl.BlockSpec((tm, D_MODEL), lambda i: (i, cgm)),
                  pl.BlockSpec((tm, D_MODEL), lambda i: (i, cga)),
                  pl.BlockSpec((tm, D_MODEL), lambda i: (i, 0)),
                  pl.BlockSpec((M_V_W, D_MODEL), const, pipeline_mode=pl.Buffered(1)),
                  pl.BlockSpec((A_W, D_MODEL), const, pipeline_mode=pl.Buffered(1)),
                  pl.BlockSpec((D_MODEL, D_MODEL), const, pipeline_mode=pl.Buffered(1)),
                  pl.BlockSpec((1, D_MODEL), const),
                  pl.BlockSpec((1, D_MODEL), const)],
        out_specs=[pl.BlockSpec((tm, D_MODEL), lambda i: (i, 0)),
                   pl.BlockSpec((tm, D_MODEL), lambda i: (i, 0))],
        out_shape=[jax.ShapeDtypeStruct((lp, D_MODEL), F32),
                   jax.ShapeDtypeStruct((lp, D_MODEL), BF16)],
        compiler_params=_cparams(("parallel",)),
        name="merge_out_ln",
    )(ym, ya, proj, proj, h, wpm, wpa, wo, g.reshape(1, -1), b.reshape(1, -1))


def _router_kernel(h_ref, w_ref, b_ref, e_ref, g_ref):
    lg = _dot_split3(h_ref[...], w_ref[...]) + b_ref[...]
    tm = lg.shape[0]
    col = lax.broadcasted_iota(jnp.int32, (tm, SM_W), 1).astype(F32)
    far = 1e9

    def first_argmax(x):
        mx = jnp.max(x, axis=-1, keepdims=True)
        return mx, jnp.min(jnp.where(x == mx, col, far), axis=-1, keepdims=True)

    gl = jnp.where(col < N_GROUPS, lg, -jnp.inf)
    gmax, gsel = first_argmax(gl)
    gprob = 1.0 / jnp.sum(jnp.exp(gl - gmax), axis=-1, keepdims=True)
    c0 = N_GROUPS + EXP_PER_GROUP * gsel
    el = jnp.where(jnp.logical_and(col >= c0, col < c0 + EXP_PER_GROUP), lg, -jnp.inf)
    v1, i1 = first_argmax(el)
    v2, i2 = first_argmax(jnp.where(col == i1, -jnp.inf, el))
    e21 = jnp.exp(v2 - v1)
    g1 = gprob / (1.0 + e21)
    g2 = gprob * e21 / (1.0 + e21)
    e_ref[...] = jnp.where(col == 0, i1 - N_GROUPS, jnp.where(col == 1, i2 - N_GROUPS, 0.0)).astype(jnp.int32)
    g_ref[...] = jnp.where(col == 0, g1, jnp.where(col == 1, g2, 0.0))


def router(h, w, b):
    lp = h.shape[0]
    tm = PAD_TO
    return pl.pallas_call(
        _router_kernel,
        grid=(lp // tm,),
        in_specs=[pl.BlockSpec((tm, D_MODEL), lambda i: (i, 0)),
                  pl.BlockSpec((D_MODEL, SM_W), lambda i: (0, 0)),
                  pl.BlockSpec((1, SM_W), lambda i: (0, 0))],
        out_specs=[pl.BlockSpec((tm, SM_W), lambda i: (i, 0)),
                   pl.BlockSpec((tm, SM_W), lambda i: (i, 0))],
        out_shape=[jax.ShapeDtypeStruct((lp, SM_W), jnp.int32),
                   jax.ShapeDtypeStruct((lp, SM_W), F32)],
        compiler_params=_cparams(("parallel",)),
        name="router",
    )(h, w, b)


TOK_TILE = 256


def _rows_to_experts_kernel(dest_ref, h_ref, xs_init_ref, xs_hbm, sem):
    del xs_init_ref
    i = pl.program_id(0)
    for r in range(TOK_TILE):
        for j in range(2):
            d = dest_ref[(i * TOK_TILE + r) * 2 + j]
            pltpu.make_async_copy(h_ref.at[pl.ds(r, 1), :], xs_hbm.at[pl.ds(d, 1), :], sem.at[0]).start()
    for _ in range(2):
        pltpu.make_async_copy(h_ref, xs_hbm.at[pl.ds(0, TOK_TILE), :], sem.at[0]).wait()


def rows_to_experts(h, dest, n_rows):
    lp = h.shape[0]
    grid_spec = pltpu.PrefetchScalarGridSpec(
        num_scalar_prefetch=1,
        grid=(lp // TOK_TILE,),
        in_specs=[pl.BlockSpec((TOK_TILE, D_MODEL), lambda i, d: (i, 0)),
                  pl.BlockSpec(memory_space=pl.ANY)],
        out_specs=pl.BlockSpec(memory_space=pl.ANY),
        scratch_shapes=[pltpu.SemaphoreType.DMA((1,))],
    )
    return pl.pallas_call(
        _rows_to_experts_kernel,
        grid_spec=grid_spec,
        out_shape=jax.ShapeDtypeStruct((n_rows, D_MODEL), F32),
        input_output_aliases={2: 0},
        compiler_params=_cparams(("arbitrary",)),
        name="rows_to_experts",
    )(dest, h, jnp.zeros((n_rows, D_MODEL), F32))


def _expert_blocks_kernel(blk_e_ref, nused_ref, xs_ref, wgu_ref, wd_ref, ys_ref, wgu_b, wd_b):
    b = pl.program_id(0)
    nused = nused_ref[0]

    @pl.when(b < nused)
    def _():
        changed = jnp.logical_or(b == 0, blk_e_ref[b] != blk_e_ref[jnp.maximum(b - 1, 0)])

        @pl.when(changed)
        def _():
            wgu_b[...] = wgu_ref[0].astype(BF16)
            wd_b[...] = wd_ref[0].astype(BF16)

        gu = jnp.dot(xs_ref[...].astype(BF16), wgu_b[...], preferred_element_type=F32)
        g = gu[:, :D_FF]
        hb = (g * _sigmoid(g)) * gu[:, D_FF:]
        ys_ref[...] = jnp.dot(hb.astype(BF16), wd_b[...], preferred_element_type=F32)

    @pl.when(b >= nused)
    def _():
        ys_ref[...] = jnp.zeros(ys_ref.shape, F32)


def expert_blocks(xs, blk_e, nused, wgu, wd):
    n_blocks = xs.shape[0] // MOE_BLOCK
    last = lambda b, n: jnp.minimum(b, jnp.maximum(n[0] - 1, 0))
    grid_spec = pltpu.PrefetchScalarGridSpec(
        num_scalar_prefetch=2,
        grid=(n_blocks,),
        in_specs=[pl.BlockSpec((MOE_BLOCK, D_MODEL), lambda b, e, n: (last(b, n), 0)),
                  pl.BlockSpec((1, D_MODEL, 2 * D_FF), lambda b, e, n: (e[b], 0, 0)),
                  pl.BlockSpec((1, D_FF, D_MODEL), lambda b, e, n: (e[b], 0, 0))],
        out_specs=pl.BlockSpec((MOE_BLOCK, D_MODEL), lambda b, e, n: (b, 0)),
        scratch_shapes=[pltpu.VMEM((D_MODEL, 2 * D_FF), BF16),
                        pltpu.VMEM((D_FF, D_MODEL), BF16)],
    )
    return pl.pallas_call(
        _expert_blocks_kernel,
        grid_spec=grid_spec,
        out_shape=jax.ShapeDtypeStruct(xs.shape, F32),
        compiler_params=_cparams(("arbitrary",)),
        name="expert_blocks",
    )(blk_e, nused, xs, wgu, wd)


def _gather_combine_kernel(dest_ref, h_ref, gate_ref, g_ref, b_ref, ys_hbm, hn_ref, hb_ref, ybuf, sem, *, alpha):
    i = pl.program_id(0)
    n = pl.num_programs(0)

    def start_gather(tile, slot):
        for r in range(TOK_TILE):
            for j in range(2):
                d = dest_ref[(tile * TOK_TILE + r) * 2 + j]
                pltpu.make_async_copy(ys_hbm.at[pl.ds(d, 1), :], ybuf.at[slot, pl.ds(j * TOK_TILE + r, 1), :],
                                      sem.at[slot]).start()

    def wait_gather(slot):
        pltpu.make_async_copy(ys_hbm.at[pl.ds(0, 2 * TOK_TILE), :], ybuf.at[slot], sem.at[slot]).wait()

    @pl.when(i == 0)
    def _():
        start_gather(0, 0)

    slot = i % 2
    start_gather(i + 1, 1 - slot)
    wait_gather(slot)
    ffn = (gate_ref[:, 0:1] * ybuf[slot, 0:TOK_TILE, :] + gate_ref[:, 1:2] * ybuf[slot, TOK_TILE:2 * TOK_TILE, :])
    y = _ln(alpha * h_ref[...] + ffn, g_ref[...], b_ref[...])
    hn_ref[...] = y
    hb_ref[...] = y.astype(BF16)

    @pl.when(i == n - 1)
    def _():
        wait_gather(1 - slot)


def gather_combine_ln(h, ys, dest_ext, gates, g, b, alpha):
    lp = h.shape[0]
    assert dest_ext.shape[0] == 2 * (lp + TOK_TILE)
    grid_spec = pltpu.PrefetchScalarGridSpec(
        num_scalar_prefetch=1,
        grid=(lp // TOK_TILE,),
        in_specs=[pl.BlockSpec((TOK_TILE, D_MODEL), lambda i, d: (i, 0)),
                  pl.BlockSpec((TOK_TILE, SM_W), lambda i, d: (i, 0)),
                  pl.BlockSpec((1, D_MODEL), lambda i, d: (0, 0)),
                  pl.BlockSpec((1, D_MODEL), lambda i, d: (0, 0)),
                  pl.BlockSpec(memory_space=pl.ANY)],
        out_specs=[pl.BlockSpec((TOK_TILE, D_MODEL), lambda i, d: (i, 0)),
                   pl.BlockSpec((TOK_TILE, D_MODEL), lambda i, d: (i, 0))],
        scratch_shapes=[pltpu.VMEM((2, 2 * TOK_TILE, D_MODEL), F32),
                        pltpu.SemaphoreType.DMA((2,))],
    )
    return pl.pallas_call(
        functools.partial(_gather_combine_kernel, alpha=alpha),
        grid_spec=grid_spec,
        out_shape=[jax.ShapeDtypeStruct((lp, D_MODEL), F32),
                   jax.ShapeDtypeStruct((lp, D_MODEL), BF16)],
        compiler_params=_cparams(("arbitrary",)),
        name="gather_combine_ln",
    )(dest_ext, h, gates, g.reshape(1, -1), b.reshape(1, -1), ys)


def moe_plan(e_ids):
    n_assign = e_ids.shape[0] * 2
    e_flat = e_ids.reshape(n_assign)
    cb = ROW_TILE
    onehot = (e_flat[:, None] == jnp.arange(N_EXPERTS, dtype=jnp.int32)[None, :]).astype(F32)
    onehot = onehot.reshape(n_assign // cb, cb, N_EXPERTS)
    tri = jnp.asarray(np.tril(np.ones((cb, cb), np.float32)))
    within = jnp.einsum('ij,bjk->bik', tri, onehot, precision=HI)
    block_tot = jnp.sum(onehot, axis=1)
    block_off = jnp.cumsum(block_tot, axis=0) - block_tot
    csum = within + block_off[:, None, :]
    rank = (jnp.sum(onehot * csum, axis=2) - 1.0).astype(jnp.int32).reshape(n_assign)
    counts = jnp.sum(block_tot, axis=0).astype(jnp.int32)
    padded = (counts + MOE_BLOCK - 1) // MOE_BLOCK * MOE_BLOCK
    pend = jnp.cumsum(padded)
    pstart = pend - padded
    dest = jnp.sum(onehot.reshape(n_assign, N_EXPERTS) * pstart.astype(F32)[None, :], axis=1).astype(jnp.int32) + rank
    n_blocks = n_assign // MOE_BLOCK + N_EXPERTS
    blk_start = jnp.arange(n_blocks, dtype=jnp.int32) * MOE_BLOCK
    blk_e = jnp.minimum(jnp.sum((pend[None, :] <= blk_start[:, None]).astype(jnp.int32), axis=1), N_EXPERTS - 1)
    nused = (pend[-1] // MOE_BLOCK).astype(jnp.int32).reshape(1)
    return dest, blk_e, nused, n_blocks * MOE_BLOCK


def _split_w_in(w):
    off = {}
    o = 0
    for name, width in (("mq", M_QK_W), ("mk", M_QK_W), ("mv", M_V_W), ("mo", M_V_W), ("mi", M_HEADS),
                        ("mf", M_HEADS), ("aq", A_W), ("ak", A_W), ("av", A_W), ("iq", IDX_Q_W),
                        ("ik", IDX_DIM), ("iw", IDX_HEADS), ("gm", D_MODEL), ("ga", D_MODEL)):
        off[name] = (o, o + width)
        o += width
    seg = lambda n: w[:, off[n][0]:off[n][1]]
    big = jnp.concatenate([seg("gm"), seg("ga"), seg("mv"), seg("mo"), seg("aq") * (A_DIM ** -0.5 * LOG2E),
                           seg("ak"), seg("av"), seg("mq"), seg("mk"), seg("iq") * (IDX_DIM ** -0.5)],
                          axis=1).astype(BF16)
    small = jnp.concatenate([seg("mi"), seg("mf"), seg("iw") * (IDX_HEADS ** -0.5),
                             jnp.zeros((w.shape[0], SM_W - 2 * M_HEADS - IDX_HEADS - IDX_DIM), F32),
                             seg("ik")], axis=1)
    return big, small


def _layer(h, hb, p, rel_bias, topk, alpha):
    w_big, w_small = _split_w_in(p["w_in"])
    proj = in_proj_big(hb, w_big)
    small = in_proj_small(h, w_small)
    small_t = small[:, 0:16].T
    ki = small[:, SM_W - IDX_DIM:].astype(BF16)
    q_t = proj[:, COL_AQ:COL_AQ + A_W].T
    v_t = proj[:, COL_AV:COL_AV + A_W].T
    iq_t = proj[:, COL_IQ:COL_IQ + IDX_Q_W].T
    y_m = mlstm(proj, small, small_t, p["conv_w"], p["conv_b"], p["b_igate"], p["b_fgate"], p["mh_norm_g"])
    mask_t = index_mask(iq_t, small_t, ki, topk)
    y_a = attention(q_t, proj, v_t, mask_t, rel_bias)
    h, hb = merge_out_ln(y_m, y_a, proj, h, p["w_proj_m"].astype(BF16), p["w_proj_a"].astype(BF16),
                         p["w_out"].astype(BF16), p["ln1_g"], p["ln1_b"], alpha)
    w_r = jnp.concatenate([p["w_group"], p["w_router"],
                           jnp.zeros((D_MODEL, SM_W - N_GROUPS - N_EXPERTS), F32)], axis=1)
    b_r = jnp.concatenate([p["b_group"], p["b_router"],
                           jnp.zeros((SM_W - N_GROUPS - N_EXPERTS,), F32)]).reshape(1, SM_W)
    e_out, g_out = router(h, w_r, b_r)
    dest, blk_e, nused, n_rows = moe_plan(e_out[:, 0:2])
    xs = rows_to_experts(h, dest, n_rows)
    ys = expert_blocks(xs, blk_e, nused, p["w_gate_up"], p["w_down"])
    dest_ext = jnp.concatenate([dest, jnp.zeros((2 * TOK_TILE,), jnp.int32)])
    return gather_combine_ln(h, ys, dest_ext, g_out, p["ln2_g"], p["ln2_b"], alpha)


def _trunk(x2, meta_tokens, ln_emb_g, ln_emb_b, rel_bias, layers):
    depth = len(layers)
    alpha = (2 * depth) ** 0.25
    seq = x2.shape[0]
    length = N_META + seq
    lp = -(-length // PAD_TO) * PAD_TO
    topk = min(TOPK_MAX, length // 4)
    xp = jnp.concatenate([meta_tokens.astype(x2.dtype), x2, jnp.zeros((lp - length, D_MODEL), x2.dtype)], axis=0)
    h, hb = embed_ln(xp, ln_emb_g, ln_emb_b)
    for p in layers:
        h, hb = _layer(h, hb, p, rel_bias, topk, alpha)
    return h[N_META:length]


def kernel(x, meta_tokens, ln_emb_g, ln_emb_b, rel_bias, w_in, conv_w, conv_b, b_igate, b_fgate, mh_norm_g,
           w_proj_m, w_proj_a, w_out, ln1_g, ln1_b, w_group, b_group, w_router, b_router, w_gate_up, w_down,
           ln2_g, ln2_b):
    depth = w_in.shape[0]
    layers = [dict(w_in=w_in[l], conv_w=conv_w[l], conv_b=conv_b[l], b_igate=b_igate[l], b_fgate=b_fgate[l],
                   mh_norm_g=mh_norm_g[l], w_proj_m=w_proj_m[l], w_proj_a=w_proj_a[l], w_out=w_out[l],
                   ln1_g=ln1_g[l], ln1_b=ln1_b[l], w_group=w_group[l], b_group=b_group[l],
                   w_router=w_router[l], b_router=b_router[l], w_gate_up=w_gate_up[l], w_down=w_down[l],
                   ln2_g=ln2_g[l], ln2_b=ln2_b[l]) for l in range(depth)]
    outs = [_trunk(x[b], meta_tokens, ln_emb_g, ln_emb_b, rel_bias, layers) for b in range(x.shape[0])]
    return jnp.stack(outs, axis=0)
```
